```python
import math
import numpy as np
import jax
import jax.numpy as jnp
from jax import lax

D_MODEL = 1024
BATCH = 8
SEQ = 2048
DEPTH = 1

CHUNK = 64
HEAD_DIM = 64
D_FF = 2816
EPS = 1e-6
NEG_INF = -1e30
H_A = 8
KV_A = 2
G_A = H_A // KV_A
WINDOW_A = 128
BACK_A = WINDOW_A // CHUNK
H_B = 8
BACK_B = 8
REL_CLIP = 128
N_REL = 2 * REL_CLIP + 1
QA = H_A * HEAD_DIM
KVA = KV_A * HEAD_DIM
QB = H_B * HEAD_DIM
D_IN = QA + 2 * KVA + 3 * QB
SPLITS = (QA, QA + KVA, QA + 2 * KVA, QA + 2 * KVA + QB, QA + 2 * KVA + 2 * QB)
D_MIX = QA + QB
N_MOD = 9

kernel_name = "hybrid_chunk_causal_swa_sink_relbias_macaron"


def rms_norm(x, g):
    x32 = x.astype(jnp.float32)
    y = x32 * lax.rsqrt(jnp.mean(x32 * x32, axis=-1, keepdims=True) + EPS)
    return y.astype(x.dtype) * g


def swiglu(h, w_gate, w_up, w_down):
    return (jax.nn.silu(h @ w_gate) * (h @ w_up)) @ w_down


def chunk_band(t, n_back):
    b, s, h, d = t.shape
    n = s // CHUNK
    tc = t.reshape(b, n, CHUNK, h, d)
    tp = jnp.pad(tc, ((0, 0), (n_back, 0), (0, 0), (0, 0), (0, 0)))
    return jnp.concatenate([tp[:, j:j + n] for j in range(n_back + 1)], axis=2)


def band_distance(n_back):
    i = jnp.arange(CHUNK, dtype=jnp.int32)[:, None]
    j = jnp.arange((n_back + 1) * CHUNK, dtype=jnp.int32)[None, :]
    return n_back * CHUNK + i - j


def band_valid(n_chunks, n_back):
    kc = (jnp.arange(n_chunks, dtype=jnp.int32)[:, None] - n_back
          + jnp.arange((n_back + 1) * CHUNK, dtype=jnp.int32)[None, :] // CHUNK)
    return kc >= 0


def band_attention(q, k_band, v_band, bias, valid, sink):
    scale = q.shape[-1] ** -0.5
    s = jnp.einsum('bnqhgd,bnshd->bnhgqs', q, k_band).astype(jnp.float32) * scale
    s = jnp.where(valid[None, :, None, None, None, :], s + bias[None, None], NEG_INF)
    if sink is None:
        p = jax.nn.softmax(s, axis=-1)
    else:
        sk = sink.astype(jnp.float32)[None, None, :, :, None, None]
        m = jnp.maximum(jnp.max(s, axis=-1, keepdims=True), sk)
        e = jnp.exp(s - m)
        p = e / (jnp.sum(e, axis=-1, keepdims=True) + jnp.exp(sk - m))
    return jnp.einsum('bnhgqs,bnshd->bnqhgd', p.astype(v_band.dtype), v_band)


def alibi_slopes(n_heads):
    return jnp.asarray(np.array([2.0 ** (-8.0 * (i + 1) / n_heads) for i in range(n_heads)], dtype=np.float32))


def hybrid_mixer(h, w_in, b_in, sinks_a, rel_bias_b, g_grp_a, g_grp_b, w_out, b_out):
    b, s, _ = h.shape
    n = s // CHUNK
    proj = h @ w_in + b_in
    qa, ka, va, qb, kb, vb = jnp.split(proj, SPLITS, axis=-1)

    qa = qa.reshape(b, n, CHUNK, KV_A, G_A, HEAD_DIM)
    ka_band = chunk_band(ka.reshape(b, s, KV_A, HEAD_DIM), BACK_A)
    va_band = chunk_band(va.reshape(b, s, KV_A, HEAD_DIM), BACK_A)
    dist_a = jnp.abs(band_distance(BACK_A)).astype(jnp.float32)
    bias_a = (-alibi_slopes(H_A)[:, None, None] * dist_a[None]).reshape(KV_A, G_A, CHUNK, -1)
    oa = band_attention(qa, ka_band, va_band, bias_a, band_valid(n, BACK_A),
                        sinks_a.reshape(KV_A, G_A)).reshape(b, s, QA)

    qb = qb.reshape(b, n, CHUNK, H_B, 1, HEAD_DIM)
    kb_band = chunk_band(kb.reshape(b, s, H_B, HEAD_DIM), BACK_B)
    vb_band = chunk_band(vb.reshape(b, s, H_B, HEAD_DIM), BACK_B)
    rel_idx = jnp.clip(band_distance(BACK_B), -REL_CLIP, REL_CLIP) + REL_CLIP
    bias_b = rel_bias_b.astype(jnp.float32)[:, rel_idx][:, None]
    ob = band_attention(qb, kb_band, vb_band, bias_b, band_valid(n, BACK_B), None).reshape(b, s, QB)

    y = jnp.concatenate([rms_norm(oa, g_grp_a), rms_norm(ob, g_grp_b)], axis=-1)
    return y @ w_out + b_out


def sandwich(x, y_fn, g_pre, g_post, shift, scale, gate, weight):
    h = rms_norm(x, g_pre) * (1.0 + scale[:, None, :]) + shift[:, None, :]
    y = rms_norm(y_fn(h), g_post)
    return x + weight * gate[:, None, :] * y


def setup_inputs(seed: int = 0) -> dict:
    key = jax.random.key(seed)
    ks = jax.random.split(key, 32)
    f32 = jnp.float32

    def w(k, shape, fan_in, mult=1.0):
        return jax.random.normal(k, shape, f32) * (mult * fan_in ** -0.5)

    def gain(k, n):
        return 1.0 + 0.05 * jax.random.normal(k, (DEPTH, n), f32)

    def bias(k, n):
        return 0.02 * jax.random.normal(k, (DEPTH, n), f32)

    return {
        "x": jax.random.normal(ks[0], (BATCH, SEQ, D_MODEL), f32),
        "c": jax.random.normal(ks[1], (BATCH, D_MODEL), f32),
        "w_ada": w(ks[2], (DEPTH, D_MODEL, N_MOD * D_MODEL), D_MODEL, 0.3),
        "b_ada": bias(ks[3], N_MOD * D_MODEL),
        "g_pre_ffn1": gain(ks[4], D_MODEL),
        "w_gate1": w(ks[5], (DEPTH, D_MODEL, D_FF), D_MODEL),
        "w_up1": w(ks[6], (DEPTH, D_MODEL, D_FF), D_MODEL),
        "w_down1": w(ks[7], (DEPTH, D_FF, D_MODEL), D_FF),
        "g_post_ffn1": gain(ks[8], D_MODEL),
        "g_pre_mix": gain(ks[9], D_MODEL),
        "w_in": w(ks[10], (DEPTH, D_MODEL, D_IN), D_MODEL),
        "b_in": bias(ks[11], D_IN),
        "sinks_a": jax.random.normal(ks[12], (DEPTH, H_A), f32),
        "rel_bias_b": 0.5 * jax.random.normal(ks[13], (DEPTH, H_B, N_REL), f32),
        "g_grp_a": gain(ks[14], QA),
        "g_grp_b": gain(ks[15], QB),
        "w_out": w(ks[16], (DEPTH, D_MIX, D_MODEL), D_MIX),
        "b_out": bias(ks[17], D_MODEL),
        "g_post_mix": gain(ks[18], D_MODEL),
        "g_pre_ffn2": gain(ks[19], D_MODEL),
        "w_gate2": w(ks[20], (DEPTH, D_MODEL, D_FF), D_MODEL),
        "w_up2": w(ks[21], (DEPTH, D_MODEL, D_FF), D_MODEL),
        "w_down2": w(ks[22], (DEPTH, D_FF, D_MODEL), D_FF),
        "g_post_ffn2": gain(ks[23], D_MODEL),
    }


def reference(x, c, w_ada, b_ada, g_pre_ffn1, w_gate1, w_up1, w_down1, g_post_ffn1,
              g_pre_mix, w_in, b_in, sinks_a, rel_bias_b, g_grp_a, g_grp_b, w_out, b_out,
              g_post_mix, g_pre_ffn2, w_gate2, w_up2, w_down2, g_post_ffn2):
    bsz = c.shape[0]
    for l in range(DEPTH):
        mod = (jax.nn.silu(c) @ w_ada[l] + b_ada[l]).reshape(bsz, N_MOD, D_MODEL)
        x = sandwich(x, lambda h: swiglu(h, w_gate1[l], w_up1[l], w_down1[l]),
                     g_pre_ffn1[l], g_post_ffn1[l], mod[:, 0], mod[:, 1], mod[:, 2], 0.5)
        x = sandwich(x, lambda h: hybrid_mixer(h, w_in[l], b_in[l], sinks_a[l], rel_bias_b[l],
                                               g_grp_a[l], g_grp_b[l], w_out[l], b_out[l]),
                     g_pre_mix[l], g_post_mix[l], mod[:, 3], mod[:, 4], mod[:, 5], 1.0)
        x = sandwich(x, lambda h: swiglu(h, w_gate2[l], w_up2[l], w_down2[l]),
                     g_pre_ffn2[l], g_post_ffn2[l], mod[:, 6], mod[:, 7], mod[:, 8], 0.5)
    return x
```

```python
import functools

import numpy as np
import jax
import jax.numpy as jnp
from jax import lax
from jax.experimental import pallas as pl
from jax.experimental.pallas import tpu as pltpu

F32 = jnp.float32
BF16 = jnp.bfloat16

CHUNK = 64
HEAD_DIM = 64
EPS = 1e-6
NEG_INF = -1e30
H_A, KV_A = 8, 2
G_A = H_A // KV_A
BACK_A = 2
H_B = 8
BACK_B = 8
REL_CLIP = 128
N_REL = 2 * REL_CLIP + 1
QA = H_A * HEAD_DIM
KVA = KV_A * HEAD_DIM
QB = H_B * HEAD_DIM
N_MOD = 9

LANES = 128
TQ = 256
PAD = BACK_B * CHUNK
WIN_A = TQ + BACK_A * CHUNK
WIN_B = TQ + BACK_B * CHUNK
TM_FFN = 512
TM_PROJ = 512
FF_CHUNK = 256
VMEM_LIMIT = 56 * 1024 * 1024


def _rms(x, g):
    ms = jnp.mean(x * x, axis=-1, keepdims=True)
    return x * lax.rsqrt(ms + EPS) * g


def _const_spec(shape):
    nd = len(shape)
    return pl.BlockSpec(shape, lambda *_: (0,) * nd, pipeline_mode=pl.Buffered(1))


def _mod_kernel(c_ref, w_ref, b_ref, o_ref):
    c = c_ref[...]
    a = (c * jax.nn.sigmoid(c)).astype(BF16)
    o_ref[...] = jnp.dot(a, w_ref[...].astype(BF16), preferred_element_type=F32) + b_ref[...]


def _modulation(c, w_ada, b_ada):
    bsz, d = c.shape
    n = w_ada.shape[1]
    tn = 1024
    return pl.pallas_call(
        _mod_kernel,
        grid=(n // tn,),
        in_specs=[pl.BlockSpec((bsz, d), lambda j: (0, 0)),
                  pl.BlockSpec((d, tn), lambda j: (0, j)),
                  pl.BlockSpec((1, tn), lambda j: (0, j))],
        out_specs=pl.BlockSpec((bsz, tn), lambda j: (0, j)),
        out_shape=jax.ShapeDtypeStruct((bsz, n), F32),
        compiler_params=pltpu.CompilerParams(dimension_semantics=("arbitrary",),
                                             vmem_limit_bytes=VMEM_LIMIT),
        name="adaln_mod",
    )(c, w_ada, b_ada.reshape(1, n))


def _ffn_kernel(x_ref, mod_ref, gpre_ref, gpost_ref, wg_ref, wu_ref, wd_ref, o_ref, *, mod0):
    x = x_ref[...]
    shift = mod_ref[0, pl.ds(mod0, 1), :]
    scale = mod_ref[0, pl.ds(mod0 + 1, 1), :]
    gate = mod_ref[0, pl.ds(mod0 + 2, 1), :]
    h = (_rms(x, gpre_ref[...]) * (1.0 + scale) + shift).astype(BF16)
    d_ff = wg_ref.shape[1]
    acc = jnp.zeros(x.shape, F32)
    for f0 in range(0, d_ff, FF_CHUNK):
        g = jnp.dot(h, wg_ref[:, f0:f0 + FF_CHUNK], preferred_element_type=F32)
        u = jnp.dot(h, wu_ref[:, f0:f0 + FF_CHUNK], preferred_element_type=F32)
        a = (g * jax.nn.sigmoid(g) * u).astype(BF16)
        acc = acc + jnp.dot(a, wd_ref[f0:f0 + FF_CHUNK, :], preferred_element_type=F32)
    o_ref[...] = x + (0.5 * gate) * _rms(acc, gpost_ref[...])


def _ffn(x2d, mod, g_pre, g_post, wg, wu, wd, *, mod0, seq):
    m, d = x2d.shape
    d_ff = wg.shape[1]
    assert m % TM_FFN == 0 and seq % TM_FFN == 0 and d_ff % FF_CHUNK == 0
    per_seq = seq // TM_FFN
    return pl.pallas_call(
        functools.partial(_ffn_kernel, mod0=mod0),
        grid=(m // TM_FFN,),
        in_specs=[pl.BlockSpec((TM_FFN, d), lambda i: (i, 0)),
                  pl.BlockSpec((1, N_MOD, d), lambda i: (i // per_seq, 0, 0)),
                  _const_spec((1, d)), _const_spec((1, d)),
                  _const_spec((d, d_ff)), _const_spec((d, d_ff)), _const_spec((d_ff, d))],
        out_specs=pl.BlockSpec((TM_FFN, d), lambda i: (i, 0)),
        out_shape=jax.ShapeDtypeStruct((m, d), F32),
        compiler_params=pltpu.CompilerParams(dimension_semantics=("arbitrary",),
                                             vmem_limit_bytes=VMEM_LIMIT),
        name="ffn",
    )(x2d, mod, g_pre.reshape(1, d), g_post.reshape(1, d), wg, wu, wd)


def _proj_kernel(x_ref, mod_ref, gpre_ref, wq_ref, bq_ref, wkv_ref, bkv_ref, q_ref, kv_ref):
    j = pl.program_id(1)

    @pl.when(j == 0)
    def _():
        kv_ref[...] = jnp.zeros(kv_ref.shape, kv_ref.dtype)

    @pl.when(j > 0)
    def _():
        shift = mod_ref[0, pl.ds(3, 1), :]
        scale = mod_ref[0, pl.ds(4, 1), :]
        h = (_rms(x_ref[0], gpre_ref[...]) * (1.0 + scale) + shift).astype(BF16)
        q = jnp.dot(h, wq_ref[...], preferred_element_type=F32) + bq_ref[...]
        q_ref[0] = (q * (HEAD_DIM ** -0.5)).astype(BF16)
        kv = jnp.dot(h, wkv_ref[...], preferred_element_type=F32) + bkv_ref[...]
        kv_ref[0] = kv.astype(BF16)


def _projection(x, mod, g_pre, wq, bq, wkv, bkv):
    bsz, seq, d = x.shape
    nq, nkv = wq.shape[1], wkv.shape[1]
    assert PAD == TM_PROJ and seq % TM_PROJ == 0
    nblk = seq // TM_PROJ
    prev = lambda b, j: (b, jnp.maximum(j - 1, 0), 0)
    return pl.pallas_call(
        _proj_kernel,
        grid=(bsz, nblk + 1),
        in_specs=[pl.BlockSpec((1, TM_PROJ, d), prev),
                  pl.BlockSpec((1, N_MOD, d), lambda b, j: (b, 0, 0)),
                  _const_spec((1, d)),
                  _const_spec((d, nq)), _const_spec((1, nq)),
                  _const_spec((d, nkv)), _const_spec((1, nkv))],
        out_specs=[pl.BlockSpec((1, TM_PROJ, nq), prev),
                   pl.BlockSpec((1, TM_PROJ, nkv), lambda b, j: (b, j, 0))],
        out_shape=[jax.ShapeDtypeStruct((bsz, seq, nq), BF16),
                   jax.ShapeDtypeStruct((bsz, seq + PAD, nkv), BF16)],
        compiler_params=pltpu.CompilerParams(dimension_semantics=("arbitrary", "arbitrary"),
                                             vmem_limit_bytes=VMEM_LIMIT),
        name="qkv_proj",
    )(x, mod, g_pre.reshape(1, d), wq, bq, wkv, bkv)


def _bias_kernel(rel_ref, ba_ref, bb_ref):
    r = lax.broadcasted_iota(jnp.int32, (G_A * TQ, WIN_A), 0)
    j = lax.broadcasted_iota(jnp.int32, (G_A * TQ, WIN_A), 1)
    g = r // TQ
    i = r % TQ
    dist = jnp.abs(BACK_A * CHUNK + i - j).astype(F32)
    in_band = (j // CHUNK >= i // CHUNK) & (j // CHUNK <= i // CHUNK + BACK_A)
    for k in range(KV_A):
        slope = jnp.zeros(r.shape, F32)
        for gg in range(G_A):
            slope = jnp.where(g == gg, 2.0 ** (-8.0 * (k * G_A + gg + 1) / H_A), slope)
        ba_ref[k] = jnp.where(in_band, -slope * dist, NEG_INF)

    width = 1024
    table = rel_ref[...]
    dcol = lax.broadcasted_iota(jnp.int32, (H_B, width), 1)
    dist_b = jnp.where(dcol < WIN_B, PAD - dcol, PAD + width - dcol)
    idx = jnp.clip(dist_b, -REL_CLIP, REL_CLIP) + REL_CLIP
    row = jnp.zeros((H_B, width), F32)
    for n in range(N_REL):
        row = jnp.where(idx == n, table[:, n:n + 1], row)
    ib = lax.broadcasted_iota(jnp.int32, (TQ, WIN_B), 0)
    jb = lax.broadcasted_iota(jnp.int32, (TQ, WIN_B), 1)
    in_band_b = (jb // CHUNK >= ib // CHUNK) & (jb // CHUNK <= ib // CHUNK + BACK_B)
    for h in range(H_B):
        tile = jnp.broadcast_to(row[h:h + 1, :], (TQ, width))
        tile = pltpu.roll(tile, 0, 1, stride=1, stride_axis=0)[:, :WIN_B]
        bb_ref[h // 2, pl.ds((h % 2) * TQ, TQ), :] = jnp.where(in_band_b, tile, NEG_INF)


def _bias_tiles(rel_bias):
    return pl.pallas_call(
        _bias_kernel,
        out_shape=[jax.ShapeDtypeStruct((KV_A, G_A * TQ, WIN_A), F32),
                   jax.ShapeDtypeStruct((H_B // 2, 2 * TQ, WIN_B), F32)],
        compiler_params=pltpu.CompilerParams(vmem_limit_bytes=VMEM_LIMIT),
        name="score_bias",
    )(rel_bias)


def _mixer_kernel(sink_ref, q_ref, kv_ref, ba_ref, bb_ref, gga_ref, ggb_ref, wo_ref, bo_ref,
                  gpost_ref, mod_ref, x_ref, o_ref):
    i = pl.program_id(1)
    ws_b = pl.multiple_of(i * TQ, TQ)
    ws_a = pl.multiple_of(i * TQ + (BACK_B - BACK_A) * CHUNK, LANES)
    lane = lax.broadcasted_iota(jnp.int32, (1, LANES), 1)
    low = lane < HEAD_DIM
    halves = (low, jnp.logical_not(low))
    key_a = jnp.where(ws_a + lax.broadcasted_iota(jnp.int32, (1, WIN_A), 1) >= PAD, 0.0, NEG_INF)
    key_b = jnp.where(ws_b + lax.broadcasted_iota(jnp.int32, (1, WIN_B), 1) >= PAD, 0.0, NEG_INF)
    nt = (((1,), (1,)), ((), ()))

    k_a = kv_ref[0, pl.ds(ws_a, WIN_A), 0:LANES]
    v_a = kv_ref[0, pl.ds(ws_a, WIN_A), LANES:2 * LANES]
    row_g = lax.broadcasted_iota(jnp.int32, (G_A * TQ, 1), 0) // TQ
    oa = [None] * G_A
    for k in range(KV_A):
        qs = [jnp.where(halves[k], q_ref[0, :, g * LANES:(g + 1) * LANES], 0) for g in range(G_A)]
        s = lax.dot_general(jnp.concatenate(qs, axis=0), k_a, nt, preferred_element_type=F32)
        s = s + ba_ref[k] + key_a
        sink = jnp.zeros((G_A * TQ, 1), F32)
        for g in range(G_A):
            sink = jnp.where(row_g == g, sink_ref[k * G_A + g], sink)
        m = jnp.maximum(jnp.max(s, axis=-1, keepdims=True), sink)
        e = jnp.exp(s - m)
        den = jnp.sum(e, axis=-1, keepdims=True) + jnp.exp(sink - m)
        o = jnp.dot(e.astype(BF16), v_a, preferred_element_type=F32) * (1.0 / den)
        for g in range(G_A):
            part = o[g * TQ:(g + 1) * TQ]
            oa[g] = part if k == 0 else jnp.where(low, oa[g], part)

    ob = []
    for t in range(H_B // 2):
        q_t = q_ref[0, :, QA + t * LANES:QA + (t + 1) * LANES]
        k_t = kv_ref[0, pl.ds(ws_b, WIN_B), 2 * LANES + t * LANES:2 * LANES + (t + 1) * LANES]
        v_t = kv_ref[0, pl.ds(ws_b, WIN_B), 2 * LANES + QB + t * LANES:2 * LANES + QB + (t + 1) * LANES]
        q2 = jnp.concatenate([jnp.where(halves[0], q_t, 0), jnp.where(halves[1], q_t, 0)], axis=0)
        s = lax.dot_general(q2, k_t, nt, preferred_element_type=F32) + bb_ref[t] + key_b
        m = jnp.max(s, axis=-1, keepdims=True)
        e = jnp.exp(s - m)
        den = jnp.sum(e, axis=-1, keepdims=True)
        o = jnp.dot(e.astype(BF16), v_t, preferred_element_type=F32) * (1.0 / den)
        ob.append(jnp.where(low, o[:TQ], o[TQ:]))

    ya = _rms(jnp.concatenate(oa, axis=1), gga_ref[...])
    yb = _rms(jnp.concatenate(ob, axis=1), ggb_ref[...])
    y = jnp.concatenate([ya, yb], axis=1).astype(BF16)
    z = jnp.dot(y, wo_ref[...], preferred_element_type=F32) + bo_ref[...]
    gate = mod_ref[0, pl.ds(5, 1), :]
    o_ref[0] = x_ref[0] + gate * _rms(z, gpost_ref[...])


def _mixer(x, mod, q, kv, bias_a, bias_b, sinks, gga, ggb, wo, bo, g_post):
    bsz, seq, d = x.shape
    assert seq % TQ == 0
    tile = lambda b, i: (b, i, 0)
    return pl.pallas_call(
        _mixer_kernel,
        grid=(bsz, seq // TQ),
        in_specs=[pl.BlockSpec(memory_space=pltpu.SMEM),
                  pl.BlockSpec((1, TQ, q.shape[2]), tile),
                  pl.BlockSpec((1, seq + PAD, kv.shape[2]), lambda b, i: (b, 0, 0)),
                  _const_spec(bias_a.shape), _const_spec(bias_b.shape),
                  _const_spec((1, QA)), _const_spec((1, QB)),
                  _const_spec(wo.shape), _const_spec((1, d)), _const_spec((1, d)),
                  pl.BlockSpec((1, N_MOD, d), lambda b, i: (b, 0, 0)),
                  pl.BlockSpec((1, TQ, d), tile)],
        out_specs=pl.BlockSpec((1, TQ, d), tile),
        out_shape=jax.ShapeDtypeStruct((bsz, seq, d), F32),
        compiler_params=pltpu.CompilerParams(dimension_semantics=("arbitrary", "arbitrary"),
                                             vmem_limit_bytes=VMEM_LIMIT),
        name="mixer",
    )(sinks, q, kv, bias_a, bias_b, gga.reshape(1, QA), ggb.reshape(1, QB), wo,
      bo.reshape(1, d), g_post.reshape(1, d), mod, x)


_PERM_A = np.array([(k * G_A + g) * HEAD_DIM + c
                    for g in range(G_A) for k in range(KV_A) for c in range(HEAD_DIM)])


def kernel(x, c, w_ada, b_ada, g_pre_ffn1, w_gate1, w_up1, w_down1, g_post_ffn1, g_pre_mix, w_in, b_in, sinks_a, rel_bias_b, g_grp_a, g_grp_b, w_out, b_out, g_post_mix, g_pre_ffn2, w_gate2, w_up2, w_down2, g_post_ffn2):
    bsz, seq, d = x.shape
    depth = w_ada.shape[0]
    for l in range(depth):
        mod = _modulation(c, w_ada[l], b_ada[l]).reshape(bsz, N_MOD, d)

        x2d = _ffn(x.reshape(bsz * seq, d), mod, g_pre_ffn1[l], g_post_ffn1[l],
                   w_gate1[l].astype(BF16), w_up1[l].astype(BF16), w_down1[l].astype(BF16),
                   mod0=0, seq=seq)
        x = x2d.reshape(bsz, seq, d)

        wi, bi = w_in[l], b_in[l]
        o_ka, o_qb, o_kb = QA, QA + 2 * KVA, QA + 2 * KVA + QB
        wq = jnp.concatenate([wi[:, :QA][:, _PERM_A], wi[:, o_qb:o_kb]], axis=1).astype(BF16)
        bq = jnp.concatenate([bi[:QA][_PERM_A], bi[o_qb:o_kb]]).reshape(1, -1)
        wkv = jnp.concatenate([wi[:, o_ka:o_qb], wi[:, o_kb:]], axis=1).astype(BF16)
        bkv = jnp.concatenate([bi[o_ka:o_qb], bi[o_kb:]]).reshape(1, -1)
        q, kv = _projection(x, mod, g_pre_mix[l], wq, bq, wkv, bkv)

        bias_a, bias_b = _bias_tiles(rel_bias_b[l])
        wo = jnp.concatenate([w_out[l][:QA][_PERM_A], w_out[l][QA:]], axis=0).astype(BF16)
        x = _mixer(x, mod, q, kv, bias_a, bias_b, sinks_a[l], g_grp_a[l][_PERM_A], g_grp_b[l],
                   wo, b_out[l], g_post_mix[l])

        x2d = _ffn(x.reshape(bsz * seq, d), mod, g_pre_ffn2[l], g_post_ffn2[l],
                   w_gate2[l].astype(BF16), w_up2[l].astype(BF16), w_down2[l].astype(BF16),
                   mod0=6, seq=seq)
        x = x2d.reshape(bsz, seq, d)
    return x
```

```python
import functools
import math

import numpy as np
import jax
import jax.numpy as jnp
from jax import lax
from jax.experimental import pallas as pl
from jax.experimental.pallas import tpu as pltpu

F32 = jnp.float32
BF16 = jnp.bfloat16

CHUNK = 64
HEAD_DIM = 64
EPS = 1e-6
NEG_INF = -1e30
H_A, KV_A = 8, 2
G_A = H_A // KV_A
BACK_A = 2
H_B = 8
BACK_B = 8
REL_CLIP = 128
N_REL = 2 * REL_CLIP + 1
QA = H_A * HEAD_DIM
KVA = KV_A * HEAD_DIM
QB = H_B * HEAD_DIM
N_MOD = 9
LOG2E = math.log2(math.e)

LANES = 128
TQ = 256
PAD = BACK_B * CHUNK
PAD_A = BACK_A * CHUNK
WIN_A = TQ + PAD_A
WIN_B = TQ + PAD
ROLL_W = 1024
TM_FFN = 512
TM_PROJ = 512
TN_MOD = 1024
FF_CHUNK = 256
VMEM_LIMIT = 56 * 1024 * 1024

_NT = (((1,), (1,)), ((), ()))
_TN = (((0,), (0,)), ((), ()))


def _rms(x, g):
    ms = jnp.mean(x * x, axis=-1, keepdims=True)
    return x * lax.rsqrt(ms + EPS) * g


def _const_spec(shape):
    nd = len(shape)
    return pl.BlockSpec(shape, lambda *_: (0,) * nd, pipeline_mode=pl.Buffered(1))


def _mod_kernel(c_ref, w_ref, b_ref, o_ref):
    c = c_ref[...]
    a = (c * jax.nn.sigmoid(c)).astype(BF16)
    o_ref[...] = jnp.dot(a, w_ref[...].astype(BF16), preferred_element_type=F32) + b_ref[...]


def _modulation(c, w_ada, b_ada):
    bsz, d = c.shape
    n = w_ada.shape[1]
    assert n % TN_MOD == 0
    return pl.pallas_call(
        _mod_kernel,
        grid=(n // TN_MOD,),
        in_specs=[pl.BlockSpec((bsz, d), lambda j: (0, 0)),
                  pl.BlockSpec((d, TN_MOD), lambda j: (0, j)),
                  pl.BlockSpec((1, TN_MOD), lambda j: (0, j))],
        out_specs=pl.BlockSpec((bsz, TN_MOD), lambda j: (0, j)),
        out_shape=jax.ShapeDtypeStruct((bsz, n), F32),
        compiler_params=pltpu.CompilerParams(dimension_semantics=("arbitrary",),
                                             vmem_limit_bytes=VMEM_LIMIT),
        name="adaln_mod",
    )(c, w_ada, b_ada.reshape(1, n))


def _ffn_kernel(x_ref, mod_ref, gpre_ref, gpost_ref, wg_ref, wu_ref, wd_ref, o_ref, *, mod0):
    x = x_ref[...]
    shift = mod_ref[0, pl.ds(mod0, 1), :]
    scale = mod_ref[0, pl.ds(mod0 + 1, 1), :]
    gate = mod_ref[0, pl.ds(mod0 + 2, 1), :]
    h = (_rms(x, gpre_ref[...]) * (1.0 + scale) + shift).astype(BF16)
    d_ff = wg_ref.shape[1]
    acc = jnp.zeros(x.shape, F32)
    for f0 in range(0, d_ff, FF_CHUNK):
        g = jnp.dot(h, wg_ref[:, f0:f0 + FF_CHUNK], preferred_element_type=F32)
        u = jnp.dot(h, wu_ref[:, f0:f0 + FF_CHUNK], preferred_element_type=F32)
        a = (g * jax.nn.sigmoid(g) * u).astype(BF16)
        acc = acc + jnp.dot(a, wd_ref[f0:f0 + FF_CHUNK, :], preferred_element_type=F32)
    o_ref[...] = x + (0.5 * gate) * _rms(acc, gpost_ref[...])


def _ffn(x2d, mod, g_pre, g_post, wg, wu, wd, *, mod0, seq):
    m, d = x2d.shape
    d_ff = wg.shape[1]
    assert m % TM_FFN == 0 and seq % TM_FFN == 0 and d_ff % FF_CHUNK == 0
    per_seq = seq // TM_FFN
    return pl.pallas_call(
        functools.partial(_ffn_kernel, mod0=mod0),
        grid=(m // TM_FFN,),
        in_specs=[pl.BlockSpec((TM_FFN, d), lambda i: (i, 0)),
                  pl.BlockSpec((1, N_MOD, d), lambda i: (i // per_seq, 0, 0)),
                  _const_spec((1, d)), _const_spec((1, d)),
                  _const_spec((d, d_ff)), _const_spec((d, d_ff)), _const_spec((d_ff, d))],
        out_specs=pl.BlockSpec((TM_FFN, d), lambda i: (i, 0)),
        out_shape=jax.ShapeDtypeStruct((m, d), F32),
        compiler_params=pltpu.CompilerParams(dimension_semantics=("arbitrary",),
                                             vmem_limit_bytes=VMEM_LIMIT),
        name="ffn",
    )(x2d, mod, g_pre.reshape(1, d), g_post.reshape(1, d), wg, wu, wd)


def _proj_kernel(x_ref, mod_ref, gpre_ref, wq_ref, bq_ref, wk_ref, bk_ref, wvt_ref, bv_ref,
                 q_ref, k_ref, vt_ref):
    j = pl.program_id(1)

    @pl.when(j == 0)
    def _():
        k_ref[...] = jnp.zeros(k_ref.shape, k_ref.dtype)
        vt_ref[...] = jnp.zeros(vt_ref.shape, vt_ref.dtype)

    @pl.when(j > 0)
    def _():
        shift = mod_ref[0, pl.ds(3, 1), :]
        scale = mod_ref[0, pl.ds(4, 1), :]
        h = (_rms(x_ref[0], gpre_ref[...]) * (1.0 + scale) + shift).astype(BF16)
        q = jnp.dot(h, wq_ref[...], preferred_element_type=F32) + bq_ref[...]
        q_ref[0] = (q * (HEAD_DIM ** -0.5 * LOG2E)).astype(BF16)
        k = jnp.dot(h, wk_ref[...], preferred_element_type=F32) + bk_ref[...]
        k_ref[0] = k.astype(BF16)
        vt = lax.dot_general(wvt_ref[...], h, _NT, preferred_element_type=F32) + bv_ref[...]
        vt_ref[0] = vt.astype(BF16)


def _projection(x, mod, g_pre, wq, bq, wk, bk, wvt, bv):
    bsz, seq, d = x.shape
    nq, nk, nv = wq.shape[1], wk.shape[1], wvt.shape[0]
    assert PAD == TM_PROJ and seq % TM_PROJ == 0
    nblk = seq // TM_PROJ
    prev = lambda b, j: (b, jnp.maximum(j - 1, 0), 0)
    return pl.pallas_call(
        _proj_kernel,
        grid=(bsz, nblk + 1),
        in_specs=[pl.BlockSpec((1, TM_PROJ, d), prev),
                  pl.BlockSpec((1, N_MOD, d), lambda b, j: (b, 0, 0)),
                  _const_spec((1, d)),
                  _const_spec((d, nq)), _const_spec((1, nq)),
                  _const_spec((d, nk)), _const_spec((1, nk)),
                  _const_spec((nv, d)), _const_spec((nv, 1))],
        out_specs=[pl.BlockSpec((1, TM_PROJ, nq), prev),
                   pl.BlockSpec((1, TM_PROJ, nk), lambda b, j: (b, j, 0)),
                   pl.BlockSpec((1, nv, TM_PROJ), lambda b, j: (b, 0, j))],
        out_shape=[jax.ShapeDtypeStruct((bsz, seq, nq), BF16),
                   jax.ShapeDtypeStruct((bsz, seq + PAD, nk), BF16),
                   jax.ShapeDtypeStruct((bsz, nv, seq + PAD), BF16)],
        compiler_params=pltpu.CompilerParams(dimension_semantics=("arbitrary", "arbitrary"),
                                             vmem_limit_bytes=VMEM_LIMIT),
        name="qkv_proj",
    )(x, mod, g_pre.reshape(1, d), wq, bq, wk, bk, wvt, bv)


def _bias_kernel(rel_ref, ba_ref, bb_ref):
    j = lax.broadcasted_iota(jnp.int32, (WIN_A, G_A * TQ), 0)
    r = lax.broadcasted_iota(jnp.int32, (WIN_A, G_A * TQ), 1)
    g = r // TQ
    i = r % TQ
    dist = jnp.abs(PAD_A + i - j).astype(F32)
    in_band = (j // CHUNK >= i // CHUNK) & (j // CHUNK <= i // CHUNK + BACK_A)
    for k in range(KV_A):
        slope = jnp.zeros(r.shape, F32)
        for gg in range(G_A):
            slope = jnp.where(g == gg, 2.0 ** (-8.0 * (k * G_A + gg + 1) / H_A), slope)
        ba_ref[k] = jnp.where(in_band, (-slope * dist) * LOG2E, NEG_INF)

    table = rel_ref[...]
    dcol = lax.broadcasted_iota(jnp.int32, (H_B, ROLL_W), 1)
    d = jnp.where(dcol < TQ, dcol, dcol - ROLL_W)
    idx = jnp.clip(PAD + d, -REL_CLIP, REL_CLIP) + REL_CLIP
    row = jnp.zeros((H_B, ROLL_W), F32)
    for n in range(N_REL):
        row = jnp.where(idx == n, table[:, n:n + 1], row)
    jb = lax.broadcasted_iota(jnp.int32, (WIN_B, TQ), 0)
    ib = lax.broadcasted_iota(jnp.int32, (WIN_B, TQ), 1)
    in_band_b = (jb // CHUNK >= ib // CHUNK) & (jb // CHUNK <= ib // CHUNK + BACK_B)
    for h in range(H_B):
        tile = jnp.broadcast_to(row[h:h + 1, :], (WIN_B, ROLL_W))
        tile = pltpu.roll(tile, 0, 1, stride=1, stride_axis=0)[:, :TQ]
        bb_ref[h // 2, :, pl.ds((h % 2) * TQ, TQ)] = jnp.where(in_band_b, tile * LOG2E, NEG_INF)


def _bias_tiles(rel_bias):
    return pl.pallas_call(
        _bias_kernel,
        out_shape=[jax.ShapeDtypeStruct((KV_A, WIN_A, G_A * TQ), F32),
                   jax.ShapeDtypeStruct((H_B // 2, WIN_B, 2 * TQ), F32)],
        compiler_params=pltpu.CompilerParams(vmem_limit_bytes=VMEM_LIMIT),
        name="score_bias",
    )(rel_bias)


def _rms_rows(xt, g_col):
    ms = jnp.mean(xt * xt, axis=0, keepdims=True)
    return xt * lax.rsqrt(ms + EPS) * g_col


def _mixer_kernel(sink_ref, q_ref, k_ref, vt_ref, ba_ref, bb_ref, gga_ref, ggb_ref, wo_ref, bo_ref,
                  gpost_ref, mod_ref, x_ref, o_ref, ba_pad, bb_pad):
    i = pl.program_id(1)
    ws_b = pl.multiple_of(i * TQ, TQ)
    ws_a = pl.multiple_of(i * TQ + PAD - PAD_A, LANES)

    @pl.when(i <= PAD // TQ)
    def _():
        ja = lax.broadcasted_iota(jnp.int32, ba_pad.shape, 1)
        ba_pad[...] = jnp.where(ws_a + ja >= PAD, ba_ref[:, :PAD_A, :], NEG_INF)
        jb = lax.broadcasted_iota(jnp.int32, bb_pad.shape, 1)
        bb_pad[...] = jnp.where(ws_b + jb >= PAD, bb_ref[:, :PAD, :], NEG_INF)

    low = lax.broadcasted_iota(jnp.int32, (1, LANES), 1) < HEAD_DIM
    halves = (low, jnp.logical_not(low))

    col_g = lax.broadcasted_iota(jnp.int32, (1, G_A * TQ), 1) // TQ

    def scores_a(k):
        k_a = k_ref[0, pl.ds(ws_a, WIN_A), 0:LANES]
        qs = jnp.concatenate([jnp.where(halves[k], q_ref[0, :, g * LANES:(g + 1) * LANES], 0)
                              for g in range(G_A)], axis=0)
        s = lax.dot_general(k_a, qs, _NT, preferred_element_type=F32)
        return jnp.concatenate([s[:PAD_A] + ba_pad[k], s[PAD_A:] + ba_ref[k, PAD_A:, :]], axis=0)

    def attend_a(k, s):
        sink = jnp.zeros((1, G_A * TQ), F32)
        for g in range(G_A):
            sink = jnp.where(col_g == g, sink_ref[k * G_A + g] * LOG2E, sink)
        m = jnp.maximum(jnp.max(s, axis=0, keepdims=True), sink)
        e = jnp.exp2(s - m)
        den = jnp.sum(e, axis=0, keepdims=True) + jnp.exp2(sink - m)
        vt_a = vt_ref[0, 0:LANES, pl.ds(ws_a, WIN_A)]
        o = jnp.dot(vt_a, e.astype(BF16), preferred_element_type=F32)
        o = o[k * HEAD_DIM:(k + 1) * HEAD_DIM] * (1.0 / den)
        return [o[:, g * TQ:(g + 1) * TQ] for g in range(G_A)]

    def scores_b(t):
        q_t = q_ref[0, :, QA + t * LANES:QA + (t + 1) * LANES]
        k_t = k_ref[0, pl.ds(ws_b, WIN_B), (1 + t) * LANES:(2 + t) * LANES]
        q2 = jnp.concatenate([jnp.where(halves[0], q_t, 0), jnp.where(halves[1], q_t, 0)], axis=0)
        s = lax.dot_general(k_t, q2, _NT, preferred_element_type=F32)
        return jnp.concatenate([s[:PAD] + bb_pad[t], s[PAD:] + bb_ref[t, PAD:, :]], axis=0)

    def attend_b(t, s):
        m = jnp.max(s, axis=0, keepdims=True)
        e = jnp.exp2(s - m)
        rden = 1.0 / jnp.sum(e, axis=0, keepdims=True)
        vt_t = vt_ref[0, (1 + t) * LANES:(2 + t) * LANES, pl.ds(ws_b, WIN_B)]
        o = jnp.dot(vt_t, e.astype(BF16), preferred_element_type=F32)
        return [o[:HEAD_DIM, :TQ] * rden[:, :TQ], o[HEAD_DIM:, TQ:] * rden[:, TQ:]]

    groups = ([(scores_a, attend_a, k) for k in range(KV_A)]
              + [(scores_b, attend_b, t) for t in range(H_B // 2)])
    outs = []
    s_next = groups[0][0](groups[0][2])
    for n, (_, attend, idx) in enumerate(groups):
        s_cur = s_next
        if n + 1 < len(groups):
            s_next = groups[n + 1][0](groups[n + 1][2])
        outs.append(attend(idx, s_cur))
    oa_t = jnp.concatenate([outs[k][g] for g in range(G_A) for k in range(KV_A)], axis=0)
    ob_t = jnp.concatenate([part for pair in outs[KV_A:] for part in pair], axis=0)

    y_t = jnp.concatenate([_rms_rows(oa_t, gga_ref[...]), _rms_rows(ob_t, ggb_ref[...])], axis=0)
    z = lax.dot_general(y_t.astype(BF16), wo_ref[...], _TN, preferred_element_type=F32) + bo_ref[...]
    gate = mod_ref[0, pl.ds(5, 1), :]
    o_ref[0] = x_ref[0] + gate * _rms(z, gpost_ref[...])


def _mixer(x, mod, q, k, vt, bias_a, bias_b, sinks, gga, ggb, wo, bo, g_post):
    bsz, seq, d = x.shape
    assert seq % TQ == 0 and PAD % TQ == 0
    tile = lambda b, i: (b, i, 0)
    whole = lambda b, i: (b, 0, 0)
    return pl.pallas_call(
        _mixer_kernel,
        grid=(bsz, seq // TQ),
        in_specs=[pl.BlockSpec(memory_space=pltpu.SMEM),
                  pl.BlockSpec((1, TQ, q.shape[2]), tile),
                  pl.BlockSpec((1,) + k.shape[1:], whole),
                  pl.BlockSpec((1,) + vt.shape[1:], whole),
                  _const_spec(bias_a.shape), _const_spec(bias_b.shape),
                  _const_spec((QA, 1)), _const_spec((QB, 1)),
                  _const_spec(wo.shape), _const_spec((1, d)), _const_spec((1, d)),
                  pl.BlockSpec((1, N_MOD, d), whole),
                  pl.BlockSpec((1, TQ, d), tile)],
        out_specs=pl.BlockSpec((1, TQ, d), tile),
        out_shape=jax.ShapeDtypeStruct((bsz, seq, d), F32),
        scratch_shapes=[pltpu.VMEM((KV_A, PAD_A, G_A * TQ), F32),
                        pltpu.VMEM((H_B // 2, PAD, 2 * TQ), F32)],
        compiler_params=pltpu.CompilerParams(dimension_semantics=("arbitrary", "arbitrary"),
                                             vmem_limit_bytes=VMEM_LIMIT),
        name="mixer",
    )(sinks, q, k, vt, bias_a, bias_b, gga.reshape(QA, 1), ggb.reshape(QB, 1), wo,
      bo.reshape(1, d), g_post.reshape(1, d), mod, x)


_PERM_A = np.array([(k * G_A + g) * HEAD_DIM + c
                    for g in range(G_A) for k in range(KV_A) for c in range(HEAD_DIM)])


def kernel(x, c, w_ada, b_ada, g_pre_ffn1, w_gate1, w_up1, w_down1, g_post_ffn1, g_pre_mix, w_in, b_in, sinks_a, rel_bias_b, g_grp_a, g_grp_b, w_out, b_out, g_post_mix, g_pre_ffn2, w_gate2, w_up2, w_down2, g_post_ffn2):
    bsz, seq, d = x.shape
    depth = w_ada.shape[0]
    for l in range(depth):
        mod = _modulation(c, w_ada[l], b_ada[l]).reshape(bsz, N_MOD, d)

        x2d = _ffn(x.reshape(bsz * seq, d), mod, g_pre_ffn1[l], g_post_ffn1[l],
                   w_gate1[l].astype(BF16), w_up1[l].astype(BF16), w_down1[l].astype(BF16),
                   mod0=0, seq=seq)
        x = x2d.reshape(bsz, seq, d)

        wi, bi = w_in[l], b_in[l]
        o_ka, o_va, o_qb, o_kb, o_vb = QA, QA + KVA, QA + 2 * KVA, QA + 2 * KVA + QB, QA + 2 * KVA + 2 * QB
        wq = jnp.concatenate([wi[:, :QA][:, _PERM_A], wi[:, o_qb:o_kb]], axis=1).astype(BF16)
        bq = jnp.concatenate([bi[:QA][_PERM_A], bi[o_qb:o_kb]]).reshape(1, -1)
        wk = jnp.concatenate([wi[:, o_ka:o_va], wi[:, o_kb:o_vb]], axis=1).astype(BF16)
        bk = jnp.concatenate([bi[o_ka:o_va], bi[o_kb:o_vb]]).reshape(1, -1)
        wvt = jnp.concatenate([wi[:, o_va:o_qb], wi[:, o_vb:]], axis=1).T.astype(BF16)
        bv = jnp.concatenate([bi[o_va:o_qb], bi[o_vb:]]).reshape(-1, 1)
        q, k, vt = _projection(x, mod, g_pre_mix[l], wq, bq, wk, bk, wvt, bv)

        bias_a, bias_b = _bias_tiles(rel_bias_b[l])
        wo = jnp.concatenate([w_out[l][:QA][_PERM_A], w_out[l][QA:]], axis=0).astype(BF16)
        x = _mixer(x, mod, q, k, vt, bias_a, bias_b, sinks_a[l], g_grp_a[l][_PERM_A], g_grp_b[l],
                   wo, b_out[l], g_post_mix[l])

        x2d = _ffn(x.reshape(bsz * seq, d), mod, g_pre_ffn2[l], g_post_ffn2[l],
                   w_gate2[l].astype(BF16), w_up2[l].astype(BF16), w_down2[l].astype(BF16),
                   mod0=6, seq=seq)
        x = x2d.reshape(bsz, seq, d)
    return x
```

```python
import functools
import math

import numpy as np
import jax
import jax.numpy as jnp
from jax import lax
from jax.experimental import pallas as pl
from jax.experimental.pallas import tpu as pltpu

F32 = jnp.float32
BF16 = jnp.bfloat16

CHUNK = 64
HEAD_DIM = 64
EPS = 1e-6
NEG_INF = -1e30
H_A, KV_A = 8, 2
G_A = H_A // KV_A
BACK_A = 2
H_B = 8
BACK_B = 8
REL_CLIP = 128
N_REL = 2 * REL_CLIP + 1
QA = H_A * HEAD_DIM
KVA = KV_A * HEAD_DIM
QB = H_B * HEAD_DIM
N_MOD = 9
LOG2E = math.log2(math.e)

LANES = 128
TQ = 256
SUB = LANES
NSUB = TQ // SUB
SCORE_AHEAD = 3
PAD = BACK_B * CHUNK
PAD_A = BACK_A * CHUNK
WIN_A = SUB + PAD_A
WIN_B = SUB + PAD
ROLL_W = 1024
TM_FFN = 512
TM_PROJ = 512
TN_MOD = 1024
FF_CHUNK = 256
VMEM_LIMIT = 56 * 1024 * 1024

_NT = (((1,), (1,)), ((), ()))
_TN = (((0,), (0,)), ((), ()))


def _rms(x, g):
    ms = jnp.mean(x * x, axis=-1, keepdims=True)
    return x * lax.rsqrt(ms + EPS) * g


def _const_spec(shape):
    nd = len(shape)
    return pl.BlockSpec(shape, lambda *_: (0,) * nd, pipeline_mode=pl.Buffered(1))


def _mod_kernel(c_ref, w_ref, b_ref, o_ref):
    c = c_ref[...]
    a = (c * jax.nn.sigmoid(c)).astype(BF16)
    o_ref[...] = jnp.dot(a, w_ref[...].astype(BF16), preferred_element_type=F32) + b_ref[...]


def _modulation(c, w_ada, b_ada):
    bsz, d = c.shape
    n = w_ada.shape[1]
    assert n % TN_MOD == 0
    return pl.pallas_call(
        _mod_kernel,
        grid=(n // TN_MOD,),
        in_specs=[pl.BlockSpec((bsz, d), lambda j: (0, 0)),
                  pl.BlockSpec((d, TN_MOD), lambda j: (0, j)),
                  pl.BlockSpec((1, TN_MOD), lambda j: (0, j))],
        out_specs=pl.BlockSpec((bsz, TN_MOD), lambda j: (0, j)),
        out_shape=jax.ShapeDtypeStruct((bsz, n), F32),
        compiler_params=pltpu.CompilerParams(dimension_semantics=("arbitrary",),
                                             vmem_limit_bytes=VMEM_LIMIT),
        name="adaln_mod",
    )(c, w_ada, b_ada.reshape(1, n))


def _ffn_kernel(x_ref, mod_ref, gpre_ref, gpost_ref, wg_ref, wu_ref, wd_ref, o_ref, *, mod0):
    x = x_ref[...]
    shift = mod_ref[0, pl.ds(mod0, 1), :]
    scale = mod_ref[0, pl.ds(mod0 + 1, 1), :]
    gate = mod_ref[0, pl.ds(mod0 + 2, 1), :]
    h = (_rms(x, gpre_ref[...]) * (1.0 + scale) + shift).astype(BF16)
    d_ff = wg_ref.shape[1]
    acc = jnp.zeros(x.shape, F32)
    for f0 in range(0, d_ff, FF_CHUNK):
        g = jnp.dot(h, wg_ref[:, f0:f0 + FF_CHUNK], preferred_element_type=F32)
        u = jnp.dot(h, wu_ref[:, f0:f0 + FF_CHUNK], preferred_element_type=F32)
        a = (g * jax.nn.sigmoid(g) * u).astype(BF16)
        acc = acc + jnp.dot(a, wd_ref[f0:f0 + FF_CHUNK, :], preferred_element_type=F32)
    o_ref[...] = x + (0.5 * gate) * _rms(acc, gpost_ref[...])


def _ffn(x2d, mod, g_pre, g_post, wg, wu, wd, *, mod0, seq):
    m, d = x2d.shape
    d_ff = wg.shape[1]
    assert m % TM_FFN == 0 and seq % TM_FFN == 0 and d_ff % FF_CHUNK == 0
    per_seq = seq // TM_FFN
    return pl.pallas_call(
        functools.partial(_ffn_kernel, mod0=mod0),
        grid=(m // TM_FFN,),
        in_specs=[pl.BlockSpec((TM_FFN, d), lambda i: (i, 0)),
                  pl.BlockSpec((1, N_MOD, d), lambda i: (i // per_seq, 0, 0)),
                  _const_spec((1, d)), _const_spec((1, d)),
                  _const_spec((d, d_ff)), _const_spec((d, d_ff)), _const_spec((d_ff, d))],
        out_specs=pl.BlockSpec((TM_FFN, d), lambda i: (i, 0)),
        out_shape=jax.ShapeDtypeStruct((m, d), F32),
        compiler_params=pltpu.CompilerParams(dimension_semantics=("arbitrary",),
                                             vmem_limit_bytes=VMEM_LIMIT),
        name="ffn",
    )(x2d, mod, g_pre.reshape(1, d), g_post.reshape(1, d), wg, wu, wd)


def _proj_kernel(x_ref, mod_ref, gpre_ref, wq_ref, bq_ref, wk_ref, bk_ref, wvt_ref, bv_ref,
                 q_ref, k_ref, vt_ref):
    j = pl.program_id(1)

    @pl.when(j == 0)
    def _():
        k_ref[...] = jnp.zeros(k_ref.shape, k_ref.dtype)
        vt_ref[...] = jnp.zeros(vt_ref.shape, vt_ref.dtype)

    @pl.when(j > 0)
    def _():
        shift = mod_ref[0, pl.ds(3, 1), :]
        scale = mod_ref[0, pl.ds(4, 1), :]
        h = (_rms(x_ref[0], gpre_ref[...]) * (1.0 + scale) + shift).astype(BF16)
        q = jnp.dot(h, wq_ref[...], preferred_element_type=F32) + bq_ref[...]
        q_ref[0] = (q * (HEAD_DIM ** -0.5 * LOG2E)).astype(BF16)
        k = jnp.dot(h, wk_ref[...], preferred_element_type=F32) + bk_ref[...]
        k_ref[0] = k.astype(BF16)
        vt = lax.dot_general(wvt_ref[...], h, _NT, preferred_element_type=F32) + bv_ref[...]
        vt_ref[0] = vt.astype(BF16)


def _projection(x, mod, g_pre, wq, bq, wk, bk, wvt, bv):
    bsz, seq, d = x.shape
    nq, nk, nv = wq.shape[1], wk.shape[1], wvt.shape[0]
    assert PAD == TM_PROJ and seq % TM_PROJ == 0
    nblk = seq // TM_PROJ
    prev = lambda b, j: (b, jnp.maximum(j - 1, 0), 0)
    return pl.pallas_call(
        _proj_kernel,
        grid=(bsz, nblk + 1),
        in_specs=[pl.BlockSpec((1, TM_PROJ, d), prev),
                  pl.BlockSpec((1, N_MOD, d), lambda b, j: (b, 0, 0)),
                  _const_spec((1, d)),
                  _const_spec((d, nq)), _const_spec((1, nq)),
                  _const_spec((d, nk)), _const_spec((1, nk)),
                  _const_spec((nv, d)), _const_spec((nv, 1))],
        out_specs=[pl.BlockSpec((1, TM_PROJ, nq), prev),
                   pl.BlockSpec((1, TM_PROJ, nk), lambda b, j: (b, j, 0)),
                   pl.BlockSpec((1, nv, TM_PROJ), lambda b, j: (b, 0, j))],
        out_shape=[jax.ShapeDtypeStruct((bsz, seq, nq), BF16),
                   jax.ShapeDtypeStruct((bsz, seq + PAD, nk), BF16),
                   jax.ShapeDtypeStruct((bsz, nv, seq + PAD), BF16)],
        compiler_params=pltpu.CompilerParams(dimension_semantics=("arbitrary", "arbitrary"),
                                             vmem_limit_bytes=VMEM_LIMIT),
        name="qkv_proj",
    )(x, mod, g_pre.reshape(1, d), wq, bq, wk, bk, wvt, bv)


def _bias_kernel(rel_ref, ba_ref, bb_ref):
    j = lax.broadcasted_iota(jnp.int32, (WIN_A, G_A * SUB), 0)
    r = lax.broadcasted_iota(jnp.int32, (WIN_A, G_A * SUB), 1)
    g = r // SUB
    i = r % SUB
    dist = jnp.abs(PAD_A + i - j).astype(F32)
    in_band = (j // CHUNK >= i // CHUNK) & (j // CHUNK <= i // CHUNK + BACK_A)
    for k in range(KV_A):
        slope = jnp.zeros(r.shape, F32)
        for gg in range(G_A):
            slope = jnp.where(g == gg, 2.0 ** (-8.0 * (k * G_A + gg + 1) / H_A), slope)
        ba_ref[k] = jnp.where(in_band, (-slope * dist) * LOG2E, NEG_INF)

    table = rel_ref[...]
    dcol = lax.broadcasted_iota(jnp.int32, (H_B, ROLL_W), 1)
    d = jnp.where(dcol < SUB, dcol, dcol - ROLL_W)
    idx = jnp.clip(PAD + d, -REL_CLIP, REL_CLIP) + REL_CLIP
    row = jnp.zeros((H_B, ROLL_W), F32)
    for n in range(N_REL):
        row = jnp.where(idx == n, table[:, n:n + 1], row)
    jb = lax.broadcasted_iota(jnp.int32, (WIN_B, SUB), 0)
    ib = lax.broadcasted_iota(jnp.int32, (WIN_B, SUB), 1)
    in_band_b = (jb // CHUNK >= ib // CHUNK) & (jb // CHUNK <= ib // CHUNK + BACK_B)
    for h in range(H_B):
        tile = jnp.broadcast_to(row[h:h + 1, :], (WIN_B, ROLL_W))
        tile = pltpu.roll(tile, 0, 1, stride=1, stride_axis=0)[:, :SUB]
        bb_ref[h // 2, :, pl.ds((h % 2) * SUB, SUB)] = jnp.where(in_band_b, tile * LOG2E, NEG_INF)


def _bias_tiles(rel_bias):
    return pl.pallas_call(
        _bias_kernel,
        out_shape=[jax.ShapeDtypeStruct((KV_A, WIN_A, G_A * SUB), F32),
                   jax.ShapeDtypeStruct((H_B // 2, WIN_B, 2 * SUB), F32)],
        compiler_params=pltpu.CompilerParams(vmem_limit_bytes=VMEM_LIMIT),
        name="score_bias",
    )(rel_bias)


def _rms_rows(xt, g_col):
    ms = jnp.mean(xt * xt, axis=0, keepdims=True)
    return xt * lax.rsqrt(ms + EPS) * g_col


def _mixer_kernel(sink_ref, q_ref, k_ref, vt_ref, ba_ref, bb_ref, gga_ref, ggb_ref, wo_ref, bo_ref,
                  gpost_ref, mod_ref, x_ref, o_ref, ba_pad, bb_pad):
    i = pl.program_id(1)
    ws_b = [pl.multiple_of(i * TQ + u * SUB, SUB) for u in range(NSUB)]
    ws_a = [pl.multiple_of(i * TQ + u * SUB + PAD - PAD_A, LANES) for u in range(NSUB)]

    @pl.when(i * TQ <= PAD)
    def _():
        ja = lax.broadcasted_iota(jnp.int32, ba_pad.shape[1:], 1)
        jb = lax.broadcasted_iota(jnp.int32, bb_pad.shape[1:], 1)
        for u in range(NSUB):
            ba_pad[u] = jnp.where(ws_a[u] + ja >= PAD, ba_ref[:, :PAD_A, :], NEG_INF)
            bb_pad[u] = jnp.where(ws_b[u] + jb >= PAD, bb_ref[:, :PAD, :], NEG_INF)

    low = lax.broadcasted_iota(jnp.int32, (1, LANES), 1) < HEAD_DIM
    halves = (low, jnp.logical_not(low))

    col_g = lax.broadcasted_iota(jnp.int32, (1, G_A * SUB), 1) // SUB

    def scores_a(u, k):
        k_a = k_ref[0, pl.ds(ws_a[u], WIN_A), 0:LANES]
        qs = jnp.concatenate(
            [jnp.where(halves[k], q_ref[0, u * SUB:(u + 1) * SUB, g * LANES:(g + 1) * LANES], 0)
             for g in range(G_A)], axis=0)
        s = lax.dot_general(k_a, qs, _NT, preferred_element_type=F32)
        return jnp.concatenate([s[:PAD_A] + ba_pad[u, k], s[PAD_A:] + ba_ref[k, PAD_A:, :]], axis=0)

    def attend_a(u, k, s):
        sink = jnp.zeros((1, G_A * SUB), F32)
        for g in range(G_A):
            sink = jnp.where(col_g == g, sink_ref[k * G_A + g] * LOG2E, sink)
        m = jnp.maximum(jnp.max(s, axis=0, keepdims=True), sink)
        e = jnp.exp2(s - m)
        den = jnp.sum(e, axis=0, keepdims=True) + jnp.exp2(sink - m)
        vt_a = vt_ref[0, 0:LANES, pl.ds(ws_a[u], WIN_A)]
        o = jnp.dot(vt_a, e.astype(BF16), preferred_element_type=F32)
        o = o[k * HEAD_DIM:(k + 1) * HEAD_DIM] * (1.0 / den)
        return [o[:, g * SUB:(g + 1) * SUB] for g in range(G_A)]

    def scores_b(u, t):
        q_t = q_ref[0, u * SUB:(u + 1) * SUB, QA + t * LANES:QA + (t + 1) * LANES]
        k_t = k_ref[0, pl.ds(ws_b[u], WIN_B), (1 + t) * LANES:(2 + t) * LANES]
        q2 = jnp.concatenate([jnp.where(halves[0], q_t, 0), jnp.where(halves[1], q_t, 0)], axis=0)
        s = lax.dot_general(k_t, q2, _NT, preferred_element_type=F32)
        return jnp.concatenate([s[:PAD] + bb_pad[u, t], s[PAD:] + bb_ref[t, PAD:, :]], axis=0)

    def attend_b(u, t, s):
        m = jnp.max(s, axis=0, keepdims=True)
        e = jnp.exp2(s - m)
        rden = 1.0 / jnp.sum(e, axis=0, keepdims=True)
        vt_t = vt_ref[0, (1 + t) * LANES:(2 + t) * LANES, pl.ds(ws_b[u], WIN_B)]
        o = jnp.dot(vt_t, e.astype(BF16), preferred_element_type=F32)
        return [o[:HEAD_DIM, :SUB] * rden[:, :SUB], o[HEAD_DIM:, SUB:] * rden[:, SUB:]]

    groups = []
    for u in range(NSUB):
        groups += [(scores_a, attend_a, u, k) for k in range(KV_A)]
        groups += [(scores_b, attend_b, u, t) for t in range(H_B // 2)]
    outs = []
    pending = [grp[0](*grp[2:]) for grp in groups[:SCORE_AHEAD]]
    for n, (_, attend, u, idx) in enumerate(groups):
        if n + SCORE_AHEAD < len(groups):
            nxt = groups[n + SCORE_AHEAD]
            pending.append(nxt[0](*nxt[2:]))
        outs.append(attend(u, idx, pending.pop(0)))
    per_sub = len(groups) // NSUB
    oa_t, ob_t = [], []
    for u in range(NSUB):
        sub = outs[u * per_sub:(u + 1) * per_sub]
        oa_t.append(jnp.concatenate([sub[k][g] for g in range(G_A) for k in range(KV_A)], axis=0))
        ob_t.append(jnp.concatenate([part for pair in sub[KV_A:] for part in pair], axis=0))
    oa_t = jnp.concatenate(oa_t, axis=1)
    ob_t = jnp.concatenate(ob_t, axis=1)

    y_t = jnp.concatenate([_rms_rows(oa_t, gga_ref[...]), _rms_rows(ob_t, ggb_ref[...])], axis=0)
    z = lax.dot_general(y_t.astype(BF16), wo_ref[...], _TN, preferred_element_type=F32) + bo_ref[...]
    gate = mod_ref[0, pl.ds(5, 1), :]
    o_ref[0] = x_ref[0] + gate * _rms(z, gpost_ref[...])


def _mixer(x, mod, q, k, vt, bias_a, bias_b, sinks, gga, ggb, wo, bo, g_post):
    bsz, seq, d = x.shape
    assert seq % TQ == 0 and TQ % SUB == 0
    tile = lambda b, i: (b, i, 0)
    whole = lambda b, i: (b, 0, 0)
    return pl.pallas_call(
        _mixer_kernel,
        grid=(bsz, seq // TQ),
        in_specs=[pl.BlockSpec(memory_space=pltpu.SMEM),
                  pl.BlockSpec((1, TQ, q.shape[2]), tile),
                  pl.BlockSpec((1,) + k.shape[1:], whole),
                  pl.BlockSpec((1,) + vt.shape[1:], whole),
                  _const_spec(bias_a.shape), _const_spec(bias_b.shape),
                  _const_spec((QA, 1)), _const_spec((QB, 1)),
                  _const_spec(wo.shape), _const_spec((1, d)), _const_spec((1, d)),
                  pl.BlockSpec((1, N_MOD, d), whole),
                  pl.BlockSpec((1, TQ, d), tile)],
        out_specs=pl.BlockSpec((1, TQ, d), tile),
        out_shape=jax.ShapeDtypeStruct((bsz, seq, d), F32),
        scratch_shapes=[pltpu.VMEM((NSUB, KV_A, PAD_A, G_A * SUB), F32),
                        pltpu.VMEM((NSUB, H_B // 2, PAD, 2 * SUB), F32)],
        compiler_params=pltpu.CompilerParams(dimension_semantics=("arbitrary", "arbitrary"),
                                             vmem_limit_bytes=VMEM_LIMIT),
        name="mixer",
    )(sinks, q, k, vt, bias_a, bias_b, gga.reshape(QA, 1), ggb.reshape(QB, 1), wo,
      bo.reshape(1, d), g_post.reshape(1, d), mod, x)


_PERM_A = np.array([(k * G_A + g) * HEAD_DIM + c
                    for g in range(G_A) for k in range(KV_A) for c in range(HEAD_DIM)])


def kernel(x, c, w_ada, b_ada, g_pre_ffn1, w_gate1, w_up1, w_down1, g_post_ffn1, g_pre_mix, w_in, b_in, sinks_a, rel_bias_b, g_grp_a, g_grp_b, w_out, b_out, g_post_mix, g_pre_ffn2, w_gate2, w_up2, w_down2, g_post_ffn2):
    bsz, seq, d = x.shape
    depth = w_ada.shape[0]
    for l in range(depth):
        mod = _modulation(c, w_ada[l], b_ada[l]).reshape(bsz, N_MOD, d)

        x2d = _ffn(x.reshape(bsz * seq, d), mod, g_pre_ffn1[l], g_post_ffn1[l],
                   w_gate1[l].astype(BF16), w_up1[l].astype(BF16), w_down1[l].astype(BF16),
                   mod0=0, seq=seq)
        x = x2d.reshape(bsz, seq, d)

        wi, bi = w_in[l], b_in[l]
        o_ka, o_va, o_qb, o_kb, o_vb = QA, QA + KVA, QA + 2 * KVA, QA + 2 * KVA + QB, QA + 2 * KVA + 2 * QB
        wq = jnp.concatenate([wi[:, :QA][:, _PERM_A], wi[:, o_qb:o_kb]], axis=1).astype(BF16)
        bq = jnp.concatenate([bi[:QA][_PERM_A], bi[o_qb:o_kb]]).reshape(1, -1)
        wk = jnp.concatenate([wi[:, o_ka:o_va], wi[:, o_kb:o_vb]], axis=1).astype(BF16)
        bk = jnp.concatenate([bi[o_ka:o_va], bi[o_kb:o_vb]]).reshape(1, -1)
        wvt = jnp.concatenate([wi[:, o_va:o_qb], wi[:, o_vb:]], axis=1).T.astype(BF16)
        bv = jnp.concatenate([bi[o_va:o_qb], bi[o_vb:]]).reshape(-1, 1)
        q, k, vt = _projection(x, mod, g_pre_mix[l], wq, bq, wk, bk, wvt, bv)

        bias_a, bias_b = _bias_tiles(rel_bias_b[l])
        wo = jnp.concatenate([w_out[l][:QA][_PERM_A], w_out[l][QA:]], axis=0).astype(BF16)
        x = _mixer(x, mod, q, k, vt, bias_a, bias_b, sinks_a[l], g_grp_a[l][_PERM_A], g_grp_b[l],
                   wo, b_out[l], g_post_mix[l])

        x2d = _ffn(x.reshape(bsz * seq, d), mod, g_pre_ffn2[l], g_post_ffn2[l],
                   w_gate2[l].astype(BF16), w_up2[l].astype(BF16), w_down2[l].astype(BF16),
                   mod0=6, seq=seq)
        x = x2d.reshape(bsz, seq, d)
    return x
```

```python
import functools
import math

import jax
import jax.numpy as jnp
from jax import lax
from jax.experimental import pallas as pl
from jax.experimental.pallas import tpu as pltpu

F32 = jnp.float32
BF16 = jnp.bfloat16

CHUNK = 64
HEAD_DIM = 64
EPS = 1e-6
NEG_INF = -1e30
H_A, KV_A = 8, 2
G_A = H_A // KV_A
BACK_A = 2
H_B = 8
BACK_B = 8
REL_CLIP = 128
N_REL = 2 * REL_CLIP + 1
QA = H_A * HEAD_DIM
KVA = KV_A * HEAD_DIM
QB = H_B * HEAD_DIM
N_MOD = 9
LOG2E = math.log2(math.e)

LANES = 128
TQ = 256
SUB = LANES
NSUB = TQ // SUB
SCORE_AHEAD = 3
PAD = BACK_B * CHUNK
PAD_A = BACK_A * CHUNK
WIN_A = SUB + PAD_A
WIN_B = SUB + PAD
ROLL_W = 1024
TM_FFN = 1024
FFN_SLAB = 512
TM_PROJ = 512
TN_MOD = 1024
FF_CHUNK = 256
VMEM_LIMIT = 56 * 1024 * 1024

_NT = (((1,), (1,)), ((), ()))
_TN = (((0,), (0,)), ((), ()))


def _rms(x, g):
    ms = jnp.mean(x * x, axis=-1, keepdims=True)
    return x * lax.rsqrt(ms + EPS) * g


def _const_spec(shape):
    nd = len(shape)
    return pl.BlockSpec(shape, lambda *_: (0,) * nd, pipeline_mode=pl.Buffered(1))


def _mod_kernel(c_ref, w_ref, b_ref, o_ref):
    c = c_ref[...]
    a = (c * jax.nn.sigmoid(c)).astype(BF16)
    o_ref[...] = jnp.dot(a, w_ref[...].astype(BF16), preferred_element_type=F32) + b_ref[...]


def _modulation(c, w_ada, b_ada):
    bsz, d = c.shape
    n = w_ada.shape[1]
    assert n % TN_MOD == 0
    return pl.pallas_call(
        _mod_kernel,
        grid=(n // TN_MOD,),
        in_specs=[pl.BlockSpec((bsz, d), lambda j: (0, 0)),
                  pl.BlockSpec((d, TN_MOD), lambda j: (0, j)),
                  pl.BlockSpec((1, TN_MOD), lambda j: (0, j))],
        out_specs=pl.BlockSpec((bsz, TN_MOD), lambda j: (0, j)),
        out_shape=jax.ShapeDtypeStruct((bsz, n), F32),
        compiler_params=pltpu.CompilerParams(dimension_semantics=("arbitrary",),
                                             vmem_limit_bytes=VMEM_LIMIT),
        name="adaln_mod",
    )(c, w_ada, b_ada.reshape(1, n))


def _ffn_kernel(x_ref, mod_ref, gpre_ref, gpost_ref, wg_ref, wu_ref, wd_ref, o_ref, *, mod0):
    shift = mod_ref[0, pl.ds(mod0, 1), :]
    scale = mod_ref[0, pl.ds(mod0 + 1, 1), :]
    gate = mod_ref[0, pl.ds(mod0 + 2, 1), :]
    d_ff = wg_ref.shape[1]
    nslab = x_ref.shape[0] // FFN_SLAB
    nchunk = d_ff // FF_CHUNK

    def rows(s):
        return slice(s * FFN_SLAB, (s + 1) * FFN_SLAB)

    def pre(s):
        return (_rms(x_ref[rows(s), :], gpre_ref[...]) * (1.0 + scale) + shift).astype(BF16)

    def post(s, acc):
        o_ref[rows(s), :] = x_ref[rows(s), :] + (0.5 * gate) * _rms(acc, gpost_ref[...])

    h = pre(0)
    done = None
    for s in range(nslab):
        acc = jnp.zeros((FFN_SLAB, x_ref.shape[1]), F32)
        h_next = None
        for c in range(nchunk):
            if c == nchunk // 4 and done is not None:
                post(s - 1, done)
            if c == nchunk // 2 and s + 1 < nslab:
                h_next = pre(s + 1)
            f0 = c * FF_CHUNK
            g = jnp.dot(h, wg_ref[:, f0:f0 + FF_CHUNK], preferred_element_type=F32)
            u = jnp.dot(h, wu_ref[:, f0:f0 + FF_CHUNK], preferred_element_type=F32)
            a = (g * jax.nn.sigmoid(g) * u).astype(BF16)
            acc = acc + jnp.dot(a, wd_ref[f0:f0 + FF_CHUNK, :], preferred_element_type=F32)
        done, h = acc, h_next
    post(nslab - 1, done)


def _ffn(x2d, mod, g_pre, g_post, wg, wu, wd, *, mod0, seq):
    m, d = x2d.shape
    d_ff = wg.shape[1]
    assert m % TM_FFN == 0 and seq % TM_FFN == 0 and d_ff % FF_CHUNK == 0
    per_seq = seq // TM_FFN
    return pl.pallas_call(
        functools.partial(_ffn_kernel, mod0=mod0),
        grid=(m // TM_FFN,),
        in_specs=[pl.BlockSpec((TM_FFN, d), lambda i: (i, 0)),
                  pl.BlockSpec((1, N_MOD, d), lambda i: (i // per_seq, 0, 0)),
                  _const_spec((1, d)), _const_spec((1, d)),
                  _const_spec((d, d_ff)), _const_spec((d, d_ff)), _const_spec((d_ff, d))],
        out_specs=pl.BlockSpec((TM_FFN, d), lambda i: (i, 0)),
        out_shape=jax.ShapeDtypeStruct((m, d), F32),
        compiler_params=pltpu.CompilerParams(dimension_semantics=("arbitrary",),
                                             vmem_limit_bytes=VMEM_LIMIT),
        name="ffn",
    )(x2d, mod, g_pre.reshape(1, d), g_post.reshape(1, d), wg, wu, wd)


def _proj_kernel(x_ref, mod_ref, gpre_ref, wq_ref, bq_ref, wk_ref, bk_ref, wvt_ref, bv_ref,
                 q_ref, k_ref, vt_ref):
    j = pl.program_id(1)

    @pl.when(j == 0)
    def _():
        k_ref[...] = jnp.zeros(k_ref.shape, k_ref.dtype)
        vt_ref[...] = jnp.zeros(vt_ref.shape, vt_ref.dtype)

    @pl.when(j > 0)
    def _():
        shift = mod_ref[0, pl.ds(3, 1), :]
        scale = mod_ref[0, pl.ds(4, 1), :]
        h = (_rms(x_ref[0], gpre_ref[...]) * (1.0 + scale) + shift).astype(BF16)
        q = jnp.dot(h, wq_ref[...], preferred_element_type=F32) + bq_ref[...]
        q_ref[0] = (q * (HEAD_DIM ** -0.5 * LOG2E)).astype(BF16)
        k = jnp.dot(h, wk_ref[...], preferred_element_type=F32) + bk_ref[...]
        k_ref[0] = k.astype(BF16)
        vt = lax.dot_general(wvt_ref[...], h, _NT, preferred_element_type=F32) + bv_ref[...]
        vt_ref[0] = vt.astype(BF16)


def _projection(x, mod, g_pre, wq, bq, wk, bk, wvt, bv):
    bsz, seq, d = x.shape
    nq, nk, nv = wq.shape[1], wk.shape[1], wvt.shape[0]
    assert PAD == TM_PROJ and seq % TM_PROJ == 0
    nblk = seq // TM_PROJ
    prev = lambda b, j: (b, jnp.maximum(j - 1, 0), 0)
    return pl.pallas_call(
        _proj_kernel,
        grid=(bsz, nblk + 1),
        in_specs=[pl.BlockSpec((1, TM_PROJ, d), prev),
                  pl.BlockSpec((1, N_MOD, d), lambda b, j: (b, 0, 0)),
                  _const_spec((1, d)),
                  _const_spec((d, nq)), _const_spec((1, nq)),
                  _const_spec((d, nk)), _const_spec((1, nk)),
                  _const_spec((nv, d)), _const_spec((nv, 1))],
        out_specs=[pl.BlockSpec((1, TM_PROJ, nq), prev),
                   pl.BlockSpec((1, TM_PROJ, nk), lambda b, j: (b, j, 0)),
                   pl.BlockSpec((1, nv, TM_PROJ), lambda b, j: (b, 0, j))],
        out_shape=[jax.ShapeDtypeStruct((bsz, seq, nq), BF16),
                   jax.ShapeDtypeStruct((bsz, seq + PAD, nk), BF16),
                   jax.ShapeDtypeStruct((bsz, nv, seq + PAD), BF16)],
        compiler_params=pltpu.CompilerParams(dimension_semantics=("arbitrary", "arbitrary"),
                                             vmem_limit_bytes=VMEM_LIMIT),
        name="qkv_proj",
    )(x, mod, g_pre.reshape(1, d), wq, bq, wk, bk, wvt, bv)


def _bias_kernel(rel_ref, ba_ref, bb_ref):
    j = lax.broadcasted_iota(jnp.int32, (WIN_A, G_A * SUB), 0)
    r = lax.broadcasted_iota(jnp.int32, (WIN_A, G_A * SUB), 1)
    g = r // SUB
    i = r % SUB
    dist = jnp.abs(PAD_A + i - j).astype(F32)
    in_band = (j // CHUNK >= i // CHUNK) & (j // CHUNK <= i // CHUNK + BACK_A)
    for k in range(KV_A):
        slope = jnp.zeros(r.shape, F32)
        for gg in range(G_A):
            slope = jnp.where(g == gg, 2.0 ** (-8.0 * (k * G_A + gg + 1) / H_A), slope)
        ba_ref[k] = jnp.where(in_band, (-slope * dist) * LOG2E, NEG_INF)

    table = rel_ref[...]
    dcol = lax.broadcasted_iota(jnp.int32, (H_B, ROLL_W), 1)
    d = jnp.where(dcol < SUB, dcol, dcol - ROLL_W)
    idx = jnp.clip(PAD + d, -REL_CLIP, REL_CLIP) + REL_CLIP
    row = jnp.zeros((H_B, ROLL_W), F32)
    for n in range(N_REL):
        row = jnp.where(idx == n, table[:, n:n + 1], row)
    jb = lax.broadcasted_iota(jnp.int32, (WIN_B, SUB), 0)
    ib = lax.broadcasted_iota(jnp.int32, (WIN_B, SUB), 1)
    in_band_b = (jb // CHUNK >= ib // CHUNK) & (jb // CHUNK <= ib // CHUNK + BACK_B)
    for h in range(H_B):
        tile = jnp.broadcast_to(row[h:h + 1, :], (WIN_B, ROLL_W))
        tile = pltpu.roll(tile, 0, 1, stride=1, stride_axis=0)[:, :SUB]
        bb_ref[h // 2, :, pl.ds((h % 2) * SUB, SUB)] = jnp.where(in_band_b, tile * LOG2E, NEG_INF)


def _bias_tiles(rel_bias):
    return pl.pallas_call(
        _bias_kernel,
        out_shape=[jax.ShapeDtypeStruct((KV_A, WIN_A, G_A * SUB), F32),
                   jax.ShapeDtypeStruct((H_B // 2, WIN_B, 2 * SUB), F32)],
        compiler_params=pltpu.CompilerParams(vmem_limit_bytes=VMEM_LIMIT),
        name="score_bias",
    )(rel_bias)


def _rms_rows(xt, g_col):
    ms = jnp.mean(xt * xt, axis=0, keepdims=True)
    return xt * lax.rsqrt(ms + EPS) * g_col


def _mixer_kernel(sink_ref, q_ref, k_ref, vt_ref, ba_ref, bb_ref, gga_ref, ggb_ref, wo_ref, bo_ref,
                  gpost_ref, mod_ref, x_ref, o_ref, ba_pad, bb_pad):
    i = pl.program_id(1)
    ws_b = [pl.multiple_of(i * TQ + u * SUB, SUB) for u in range(NSUB)]
    ws_a = [pl.multiple_of(i * TQ + u * SUB + PAD - PAD_A, LANES) for u in range(NSUB)]

    @pl.when(i * TQ <= PAD)
    def _():
        ja = lax.broadcasted_iota(jnp.int32, ba_pad.shape[1:], 1)
        jb = lax.broadcasted_iota(jnp.int32, bb_pad.shape[1:], 1)
        for u in range(NSUB):
            ba_pad[u] = jnp.where(ws_a[u] + ja >= PAD, ba_ref[:, :PAD_A, :], NEG_INF)
            bb_pad[u] = jnp.where(ws_b[u] + jb >= PAD, bb_ref[:, :PAD, :], NEG_INF)

    low = lax.broadcasted_iota(jnp.int32, (1, LANES), 1) < HEAD_DIM
    halves = (low, jnp.logical_not(low))

    col_g = lax.broadcasted_iota(jnp.int32, (1, G_A * SUB), 1) // SUB

    def scores_a(u, k):
        k_a = k_ref[0, pl.ds(ws_a[u], WIN_A), 0:LANES]
        qs = jnp.concatenate(
            [jnp.where(halves[k], q_ref[0, u * SUB:(u + 1) * SUB, g * LANES:(g + 1) * LANES], 0)
             for g in range(G_A)], axis=0)
        s = lax.dot_general(k_a, qs, _NT, preferred_element_type=F32)
        return jnp.concatenate([s[:PAD_A] + ba_pad[u, k], s[PAD_A:] + ba_ref[k, PAD_A:, :]], axis=0)

    def attend_a(u, k, s):
        sink = jnp.zeros((1, G_A * SUB), F32)
        for g in range(G_A):
            sink = jnp.where(col_g == g, sink_ref[k * G_A + g] * LOG2E, sink)
        m = jnp.maximum(jnp.max(s, axis=0, keepdims=True), sink)
        e = jnp.exp2(s - m)
        den = jnp.sum(e, axis=0, keepdims=True) + jnp.exp2(sink - m)
        vt_a = vt_ref[0, 0:LANES, pl.ds(ws_a[u], WIN_A)]
        o = jnp.dot(vt_a, e.astype(BF16), preferred_element_type=F32)
        o = o[k * HEAD_DIM:(k + 1) * HEAD_DIM] * (1.0 / den)
        return [o[:, g * SUB:(g + 1) * SUB] for g in range(G_A)]

    def scores_b(u, t):
        q_t = q_ref[0, u * SUB:(u + 1) * SUB, QA + t * LANES:QA + (t + 1) * LANES]
        k_t = k_ref[0, pl.ds(ws_b[u], WIN_B), (1 + t) * LANES:(2 + t) * LANES]
        q2 = jnp.concatenate([jnp.where(halves[0], q_t, 0), jnp.where(halves[1], q_t, 0)], axis=0)
        s = lax.dot_general(k_t, q2, _NT, preferred_element_type=F32)
        return jnp.concatenate([s[:PAD] + bb_pad[u, t], s[PAD:] + bb_ref[t, PAD:, :]], axis=0)

    def attend_b(u, t, s):
        m = jnp.max(s, axis=0, keepdims=True)
        e = jnp.exp2(s - m)
        rden = 1.0 / jnp.sum(e, axis=0, keepdims=True)
        vt_t = vt_ref[0, (1 + t) * LANES:(2 + t) * LANES, pl.ds(ws_b[u], WIN_B)]
        o = jnp.dot(vt_t, e.astype(BF16), preferred_element_type=F32)
        return [o[:HEAD_DIM, :SUB] * rden[:, :SUB], o[HEAD_DIM:, SUB:] * rden[:, SUB:]]

    groups = []
    for u in range(NSUB):
        groups += [(scores_a, attend_a, u, k) for k in range(KV_A)]
        groups += [(scores_b, attend_b, u, t) for t in range(H_B // 2)]
    outs = []
    pending = [grp[0](*grp[2:]) for grp in groups[:SCORE_AHEAD]]
    for n, (_, attend, u, idx) in enumerate(groups):
        if n + SCORE_AHEAD < len(groups):
            nxt = groups[n + SCORE_AHEAD]
            pending.append(nxt[0](*nxt[2:]))
        outs.append(attend(u, idx, pending.pop(0)))
    per_sub = len(groups) // NSUB
    oa_t, ob_t = [], []
    for u in range(NSUB):
        sub = outs[u * per_sub:(u + 1) * per_sub]
        oa_t.append(jnp.concatenate([sub[k][g] for g in range(G_A) for k in range(KV_A)], axis=0))
        ob_t.append(jnp.concatenate([part for pair in sub[KV_A:] for part in pair], axis=0))
    oa_t = jnp.concatenate(oa_t, axis=1)
    ob_t = jnp.concatenate(ob_t, axis=1)

    y_t = jnp.concatenate([_rms_rows(oa_t, gga_ref[...]), _rms_rows(ob_t, ggb_ref[...])], axis=0)
    z = lax.dot_general(y_t.astype(BF16), wo_ref[...], _TN, preferred_element_type=F32) + bo_ref[...]
    gate = mod_ref[0, pl.ds(5, 1), :]
    o_ref[0] = x_ref[0] + gate * _rms(z, gpost_ref[...])


def _mixer(x, mod, q, k, vt, bias_a, bias_b, sinks, gga, ggb, wo, bo, g_post):
    bsz, seq, d = x.shape
    assert seq % TQ == 0 and TQ % SUB == 0
    tile = lambda b, i: (b, i, 0)
    whole = lambda b, i: (b, 0, 0)
    return pl.pallas_call(
        _mixer_kernel,
        grid=(bsz, seq // TQ),
        in_specs=[pl.BlockSpec(memory_space=pltpu.SMEM),
                  pl.BlockSpec((1, TQ, q.shape[2]), tile),
                  pl.BlockSpec((1,) + k.shape[1:], whole),
                  pl.BlockSpec((1,) + vt.shape[1:], whole),
                  _const_spec(bias_a.shape), _const_spec(bias_b.shape),
                  _const_spec((QA, 1)), _const_spec((QB, 1)),
                  _const_spec(wo.shape), _const_spec((1, d)), _const_spec((1, d)),
                  pl.BlockSpec((1, N_MOD, d), whole),
                  pl.BlockSpec((1, TQ, d), tile)],
        out_specs=pl.BlockSpec((1, TQ, d), tile),
        out_shape=jax.ShapeDtypeStruct((bsz, seq, d), F32),
        scratch_shapes=[pltpu.VMEM((NSUB, KV_A, PAD_A, G_A * SUB), F32),
                        pltpu.VMEM((NSUB, H_B // 2, PAD, 2 * SUB), F32)],
        compiler_params=pltpu.CompilerParams(dimension_semantics=("arbitrary", "arbitrary"),
                                             vmem_limit_bytes=VMEM_LIMIT),
        name="mixer",
    )(sinks, q, k, vt, bias_a, bias_b, gga.reshape(QA, 1), ggb.reshape(QB, 1), wo,
      bo.reshape(1, d), g_post.reshape(1, d), mod, x)


def _pair_heads_a(w, axis):
    shape = w.shape
    split = shape[:axis] + (KV_A, G_A, HEAD_DIM) + shape[axis + 1:]
    return jnp.swapaxes(w.reshape(split), axis, axis + 1).reshape(shape)


def kernel(x, c, w_ada, b_ada, g_pre_ffn1, w_gate1, w_up1, w_down1, g_post_ffn1, g_pre_mix, w_in, b_in, sinks_a, rel_bias_b, g_grp_a, g_grp_b, w_out, b_out, g_post_mix, g_pre_ffn2, w_gate2, w_up2, w_down2, g_post_ffn2):
    bsz, seq, d = x.shape
    depth = w_ada.shape[0]
    for l in range(depth):
        mod = _modulation(c, w_ada[l], b_ada[l]).reshape(bsz, N_MOD, d)

        x2d = _ffn(x.reshape(bsz * seq, d), mod, g_pre_ffn1[l], g_post_ffn1[l],
                   w_gate1[l].astype(BF16), w_up1[l].astype(BF16), w_down1[l].astype(BF16),
                   mod0=0, seq=seq)
        x = x2d.reshape(bsz, seq, d)

        wi, bi = w_in[l], b_in[l]
        o_ka, o_va, o_qb, o_kb, o_vb = QA, QA + KVA, QA + 2 * KVA, QA + 2 * KVA + QB, QA + 2 * KVA + 2 * QB
        wq = jnp.concatenate([_pair_heads_a(wi[:, :QA], 1), wi[:, o_qb:o_kb]], axis=1).astype(BF16)
        bq = jnp.concatenate([_pair_heads_a(bi[:QA], 0), bi[o_qb:o_kb]]).reshape(1, -1)
        wk = jnp.concatenate([wi[:, o_ka:o_va], wi[:, o_kb:o_vb]], axis=1).astype(BF16)
        bk = jnp.concatenate([bi[o_ka:o_va], bi[o_kb:o_vb]]).reshape(1, -1)
        wvt = jnp.concatenate([wi[:, o_va:o_qb], wi[:, o_vb:]], axis=1).T.astype(BF16)
        bv = jnp.concatenate([bi[o_va:o_qb], bi[o_vb:]]).reshape(-1, 1)
        q, k, vt = _projection(x, mod, g_pre_mix[l], wq, bq, wk, bk, wvt, bv)

        bias_a, bias_b = _bias_tiles(rel_bias_b[l])
        wo = jnp.concatenate([_pair_heads_a(w_out[l][:QA], 0), w_out[l][QA:]], axis=0).astype(BF16)
        x = _mixer(x, mod, q, k, vt, bias_a, bias_b, sinks_a[l], _pair_heads_a(g_grp_a[l], 0),
                   g_grp_b[l], wo, b_out[l], g_post_mix[l])

        x2d = _ffn(x.reshape(bsz * seq, d), mod, g_pre_ffn2[l], g_post_ffn2[l],
                   w_gate2[l].astype(BF16), w_up2[l].astype(BF16), w_down2[l].astype(BF16),
                   mod0=6, seq=seq)
        x = x2d.reshape(bsz, seq, d)
    return x
```

```python
import functools
import math

import jax
import jax.numpy as jnp
from jax import lax
from jax.experimental import pallas as pl
from jax.experimental.pallas import tpu as pltpu

F32 = jnp.float32
BF16 = jnp.bfloat16

CHUNK = 64
HEAD_DIM = 64
EPS = 1e-6
NEG_INF = -1e30
H_A, KV_A = 8, 2
G_A = H_A // KV_A
BACK_A = 2
H_B = 8
BACK_B = 8
REL_CLIP = 128
N_REL = 2 * REL_CLIP + 1
QA = H_A * HEAD_DIM
KVA = KV_A * HEAD_DIM
QB = H_B * HEAD_DIM
N_MOD = 9
LOG2E = math.log2(math.e)

LANES = 128
TQ = 256
SUB = LANES
NSUB = TQ // SUB
SCORE_AHEAD = 3
PAD = BACK_B * CHUNK
PAD_A = BACK_A * CHUNK
WIN_A = SUB + PAD_A
WIN_B = SUB + PAD
ROLL_W = 1024
TM_FFN = 1024
FFN_SLAB = 512
KV_LEAD = TM_FFN
TN_MOD = 1024
FF_CHUNK = 256
VMEM_LIMIT = 56 * 1024 * 1024

_NT = (((1,), (1,)), ((), ()))
_TN = (((0,), (0,)), ((), ()))


def _rms(x, g):
    ms = jnp.mean(x * x, axis=-1, keepdims=True)
    return x * lax.rsqrt(ms + EPS) * g


def _const_spec(shape):
    nd = len(shape)
    return pl.BlockSpec(shape, lambda *_: (0,) * nd, pipeline_mode=pl.Buffered(1))


def _mod_kernel(c_ref, w_ref, b_ref, o_ref):
    c = c_ref[...]
    a = (c * jax.nn.sigmoid(c)).astype(BF16)
    o_ref[...] = jnp.dot(a, w_ref[...].astype(BF16), preferred_element_type=F32) + b_ref[...]


def _modulation(c, w_ada, b_ada):
    bsz, d = c.shape
    n = w_ada.shape[1]
    assert n % TN_MOD == 0
    return pl.pallas_call(
        _mod_kernel,
        grid=(n // TN_MOD,),
        in_specs=[pl.BlockSpec((bsz, d), lambda j: (0, 0)),
                  pl.BlockSpec((d, TN_MOD), lambda j: (0, j)),
                  pl.BlockSpec((1, TN_MOD), lambda j: (0, j))],
        out_specs=pl.BlockSpec((bsz, TN_MOD), lambda j: (0, j)),
        out_shape=jax.ShapeDtypeStruct((bsz, n), F32),
        compiler_params=pltpu.CompilerParams(dimension_semantics=("arbitrary",),
                                             vmem_limit_bytes=VMEM_LIMIT),
        name="adaln_mod",
    )(c, w_ada, b_ada.reshape(1, n))


def _ffn_kernel(*refs, mod0, lead):
    if lead:
        (x_ref, mod_ref, gpre_ref, gpost_ref, wg_ref, wu_ref, wd_ref,
         gmix_ref, wq_ref, bq_ref, wk_ref, bk_ref, wvt_ref, bv_ref,
         o_ref, q_ref, k_ref, vt_ref) = refs
    else:
        x_ref, mod_ref, gpre_ref, gpost_ref, wg_ref, wu_ref, wd_ref, o_ref = refs
    shift = mod_ref[0, pl.ds(mod0, 1), :]
    scale = mod_ref[0, pl.ds(mod0 + 1, 1), :]
    gate = mod_ref[0, pl.ds(mod0 + 2, 1), :]
    d_ff = wg_ref.shape[1]
    nslab = x_ref.shape[1] // FFN_SLAB
    nchunk = d_ff // FF_CHUNK

    def rows(s):
        return slice(s * FFN_SLAB, (s + 1) * FFN_SLAB)

    def pre(s):
        return (_rms(x_ref[0, rows(s), :], gpre_ref[...]) * (1.0 + scale) + shift).astype(BF16)

    def post(s, acc):
        x1 = x_ref[0, rows(s), :] + (0.5 * gate) * _rms(acc, gpost_ref[...])
        o_ref[0, rows(s), :] = x1
        if not lead:
            return None
        shift_m = mod_ref[0, pl.ds(mod0 + 3, 1), :]
        scale_m = mod_ref[0, pl.ds(mod0 + 4, 1), :]
        return (_rms(x1, gmix_ref[...]) * (1.0 + scale_m) + shift_m).astype(BF16)

    def project(s, hm):
        q = jnp.dot(hm, wq_ref[...], preferred_element_type=F32) + bq_ref[...]
        q_ref[0, rows(s), :] = (q * (HEAD_DIM ** -0.5 * LOG2E)).astype(BF16)
        k = jnp.dot(hm, wk_ref[...], preferred_element_type=F32) + bk_ref[...]
        k_ref[0, rows(s), :] = k.astype(BF16)
        vt = lax.dot_general(wvt_ref[...], hm, _NT, preferred_element_type=F32) + bv_ref[...]
        vt_ref[0, :, rows(s)] = vt.astype(BF16)

    def body():
        h = pre(0)
        done = hm = None
        for s in range(nslab):
            acc = jnp.zeros((FFN_SLAB, x_ref.shape[2]), F32)
            h_next = None
            for c in range(nchunk):
                if c == nchunk // 4 and done is not None:
                    hm = post(s - 1, done)
                if c == nchunk // 2 and s + 1 < nslab:
                    h_next = pre(s + 1)
                if c == (3 * nchunk) // 4 and hm is not None:
                    project(s - 1, hm)
                f0 = c * FF_CHUNK
                g = jnp.dot(h, wg_ref[:, f0:f0 + FF_CHUNK], preferred_element_type=F32)
                u = jnp.dot(h, wu_ref[:, f0:f0 + FF_CHUNK], preferred_element_type=F32)
                a = (g * jax.nn.sigmoid(g) * u).astype(BF16)
                acc = acc + jnp.dot(a, wd_ref[f0:f0 + FF_CHUNK, :], preferred_element_type=F32)
            done, h = acc, h_next
        hm = post(nslab - 1, done)
        if hm is not None:
            project(nslab - 1, hm)

    if lead:
        j = pl.program_id(1)

        @pl.when(j < lead)
        def _():
            k_ref[...] = jnp.zeros(k_ref.shape, k_ref.dtype)
            vt_ref[...] = jnp.zeros(vt_ref.shape, vt_ref.dtype)

        pl.when(j >= lead)(body)
    else:
        body()


def _ffn(x, mod, g_pre, g_post, wg, wu, wd, *, mod0, proj=None):
    bsz, seq, d = x.shape
    d_ff = wg.shape[1]
    assert seq % TM_FFN == 0 and TM_FFN % FFN_SLAB == 0 and d_ff % FF_CHUNK == 0
    lead = KV_LEAD // TM_FFN if proj is not None else 0
    blk = lambda b, j: (b, jnp.maximum(j - lead, 0), 0)
    in_specs = [pl.BlockSpec((1, TM_FFN, d), blk),
                pl.BlockSpec((1, N_MOD, d), lambda b, j: (b, 0, 0)),
                _const_spec((1, d)), _const_spec((1, d)),
                _const_spec((d, d_ff)), _const_spec((d, d_ff)), _const_spec((d_ff, d))]
    out_specs = [pl.BlockSpec((1, TM_FFN, d), blk)]
    out_shape = [jax.ShapeDtypeStruct((bsz, seq, d), F32)]
    args = [x, mod, g_pre.reshape(1, d), g_post.reshape(1, d), wg, wu, wd]
    if proj is not None:
        g_mix, wq, bq, wk, bk, wvt, bv = proj
        nq, nk, nv = wq.shape[1], wk.shape[1], wvt.shape[0]
        in_specs += [_const_spec((1, d)), _const_spec((d, nq)), _const_spec((1, nq)),
                     _const_spec((d, nk)), _const_spec((1, nk)),
                     _const_spec((nv, d)), _const_spec((nv, 1))]
        out_specs += [pl.BlockSpec((1, TM_FFN, nq), blk),
                      pl.BlockSpec((1, TM_FFN, nk), lambda b, j: (b, j, 0)),
                      pl.BlockSpec((1, nv, TM_FFN), lambda b, j: (b, 0, j))]
        out_shape += [jax.ShapeDtypeStruct((bsz, seq, nq), BF16),
                      jax.ShapeDtypeStruct((bsz, seq + KV_LEAD, nk), BF16),
                      jax.ShapeDtypeStruct((bsz, nv, seq + KV_LEAD), BF16)]
        args += [g_mix.reshape(1, d), wq, bq, wk, bk, wvt, bv]
    out = pl.pallas_call(
        functools.partial(_ffn_kernel, mod0=mod0, lead=lead),
        grid=(bsz, seq // TM_FFN + lead),
        in_specs=in_specs,
        out_specs=out_specs,
        out_shape=out_shape,
        compiler_params=pltpu.CompilerParams(dimension_semantics=("arbitrary", "arbitrary"),
                                             vmem_limit_bytes=VMEM_LIMIT),
        name="ffn_proj" if proj is not None else "ffn",
    )(*args)
    return out if proj is not None else out[0]


def _bias_kernel(rel_ref, ba_ref, bb_ref):
    j = lax.broadcasted_iota(jnp.int32, (WIN_A, G_A * SUB), 0)
    r = lax.broadcasted_iota(jnp.int32, (WIN_A, G_A * SUB), 1)
    g = r // SUB
    i = r % SUB
    dist = jnp.abs(PAD_A + i - j).astype(F32)
    in_band = (j // CHUNK >= i // CHUNK) & (j // CHUNK <= i // CHUNK + BACK_A)
    for k in range(KV_A):
        slope = jnp.zeros(r.shape, F32)
        for gg in range(G_A):
            slope = jnp.where(g == gg, 2.0 ** (-8.0 * (k * G_A + gg + 1) / H_A), slope)
        ba_ref[k] = jnp.where(in_band, (-slope * dist) * LOG2E, NEG_INF)

    table = rel_ref[...]
    dcol = lax.broadcasted_iota(jnp.int32, (H_B, ROLL_W), 1)
    d = jnp.where(dcol < SUB, dcol, dcol - ROLL_W)
    idx = jnp.clip(PAD + d, -REL_CLIP, REL_CLIP) + REL_CLIP
    row = jnp.zeros((H_B, ROLL_W), F32)
    for n in range(N_REL):
        row = jnp.where(idx == n, table[:, n:n + 1], row)
    jb = lax.broadcasted_iota(jnp.int32, (WIN_B, SUB), 0)
    ib = lax.broadcasted_iota(jnp.int32, (WIN_B, SUB), 1)
    in_band_b = (jb // CHUNK >= ib // CHUNK) & (jb // CHUNK <= ib // CHUNK + BACK_B)
    for h in range(H_B):
        tile = jnp.broadcast_to(row[h:h + 1, :], (WIN_B, ROLL_W))
        tile = pltpu.roll(tile, 0, 1, stride=1, stride_axis=0)[:, :SUB]
        bb_ref[h // 2, :, pl.ds((h % 2) * SUB, SUB)] = jnp.where(in_band_b, tile * LOG2E, NEG_INF)


def _bias_tiles(rel_bias):
    return pl.pallas_call(
        _bias_kernel,
        out_shape=[jax.ShapeDtypeStruct((KV_A, WIN_A, G_A * SUB), F32),
                   jax.ShapeDtypeStruct((H_B // 2, WIN_B, 2 * SUB), F32)],
        compiler_params=pltpu.CompilerParams(vmem_limit_bytes=VMEM_LIMIT),
        name="score_bias",
    )(rel_bias)


def _rms_rows(xt, g_col):
    ms = jnp.mean(xt * xt, axis=0, keepdims=True)
    return xt * lax.rsqrt(ms + EPS) * g_col


def _mixer_kernel(sink_ref, q_ref, k_ref, vt_ref, ba_ref, bb_ref, gga_ref, ggb_ref, wo_ref, bo_ref,
                  gpost_ref, mod_ref, x_ref, o_ref, ba_pad, bb_pad):
    i = pl.program_id(1)
    ws_b = [pl.multiple_of(i * TQ + u * SUB + KV_LEAD - PAD, SUB) for u in range(NSUB)]
    ws_a = [pl.multiple_of(i * TQ + u * SUB + KV_LEAD - PAD_A, LANES) for u in range(NSUB)]

    @pl.when(i * TQ <= PAD)
    def _():
        ja = lax.broadcasted_iota(jnp.int32, ba_pad.shape[1:], 1)
        jb = lax.broadcasted_iota(jnp.int32, bb_pad.shape[1:], 1)
        for u in range(NSUB):
            ba_pad[u] = jnp.where(ws_a[u] + ja >= KV_LEAD, ba_ref[:, :PAD_A, :], NEG_INF)
            bb_pad[u] = jnp.where(ws_b[u] + jb >= KV_LEAD, bb_ref[:, :PAD, :], NEG_INF)

    low = lax.broadcasted_iota(jnp.int32, (1, LANES), 1) < HEAD_DIM
    halves = (low, jnp.logical_not(low))
    col_g = lax.broadcasted_iota(jnp.int32, (1, G_A * SUB), 1) // SUB

    def scores_a(u, k):
        k_a = k_ref[0, pl.ds(ws_a[u], WIN_A), 0:LANES]
        qs = jnp.concatenate(
            [jnp.where(halves[k], q_ref[0, u * SUB:(u + 1) * SUB, g * LANES:(g + 1) * LANES], 0)
             for g in range(G_A)], axis=0)
        s = lax.dot_general(k_a, qs, _NT, preferred_element_type=F32)
        return jnp.concatenate([s[:PAD_A] + ba_pad[u, k], s[PAD_A:] + ba_ref[k, PAD_A:, :]], axis=0)

    def attend_a(u, k, s):
        sink = jnp.zeros((1, G_A * SUB), F32)
        for g in range(G_A):
            sink = jnp.where(col_g == g, sink_ref[k * G_A + g] * LOG2E, sink)
        m = jnp.maximum(jnp.max(s, axis=0, keepdims=True), sink)
        e = jnp.exp2(s - m)
        den = jnp.sum(e, axis=0, keepdims=True) + jnp.exp2(sink - m)
        vt_a = vt_ref[0, 0:LANES, pl.ds(ws_a[u], WIN_A)]
        o = jnp.dot(vt_a, e.astype(BF16), preferred_element_type=F32)
        o = o[k * HEAD_DIM:(k + 1) * HEAD_DIM] * (1.0 / den)
        return [o[:, g * SUB:(g + 1) * SUB] for g in range(G_A)]

    def scores_b(u, t):
        q_t = q_ref[0, u * SUB:(u + 1) * SUB, QA + t * LANES:QA + (t + 1) * LANES]
        k_t = k_ref[0, pl.ds(ws_b[u], WIN_B), (1 + t) * LANES:(2 + t) * LANES]
        q2 = jnp.concatenate([jnp.where(halves[0], q_t, 0), jnp.where(halves[1], q_t, 0)], axis=0)
        s = lax.dot_general(k_t, q2, _NT, preferred_element_type=F32)
        return jnp.concatenate([s[:PAD] + bb_pad[u, t], s[PAD:] + bb_ref[t, PAD:, :]], axis=0)

    def attend_b(u, t, s):
        m = jnp.max(s, axis=0, keepdims=True)
        e = jnp.exp2(s - m)
        rden = 1.0 / jnp.sum(e, axis=0, keepdims=True)
        vt_t = vt_ref[0, (1 + t) * LANES:(2 + t) * LANES, pl.ds(ws_b[u], WIN_B)]
        o = jnp.dot(vt_t, e.astype(BF16), preferred_element_type=F32)
        return [o[:HEAD_DIM, :SUB] * rden[:, :SUB], o[HEAD_DIM:, SUB:] * rden[:, SUB:]]

    groups = []
    for u in range(NSUB):
        groups += [(scores_a, attend_a, u, k) for k in range(KV_A)]
        groups += [(scores_b, attend_b, u, t) for t in range(H_B // 2)]
    outs = []
    pending = [grp[0](*grp[2:]) for grp in groups[:SCORE_AHEAD]]
    for n, (_, attend, u, idx) in enumerate(groups):
        if n + SCORE_AHEAD < len(groups):
            nxt = groups[n + SCORE_AHEAD]
            pending.append(nxt[0](*nxt[2:]))
        outs.append(attend(u, idx, pending.pop(0)))
    per_sub = len(groups) // NSUB
    oa_t, ob_t = [], []
    for u in range(NSUB):
        sub = outs[u * per_sub:(u + 1) * per_sub]
        oa_t.append(jnp.concatenate([sub[k][g] for g in range(G_A) for k in range(KV_A)], axis=0))
        ob_t.append(jnp.concatenate([part for pair in sub[KV_A:] for part in pair], axis=0))
    oa_t = jnp.concatenate(oa_t, axis=1)
    ob_t = jnp.concatenate(ob_t, axis=1)

    y_t = jnp.concatenate([_rms_rows(oa_t, gga_ref[...]), _rms_rows(ob_t, ggb_ref[...])], axis=0)
    z = lax.dot_general(y_t.astype(BF16), wo_ref[...], _TN, preferred_element_type=F32) + bo_ref[...]
    gate = mod_ref[0, pl.ds(5, 1), :]
    o_ref[0] = x_ref[0] + gate * _rms(z, gpost_ref[...])


def _mixer(x, mod, q, k, vt, bias_a, bias_b, sinks, gga, ggb, wo, bo, g_post):
    bsz, seq, d = x.shape
    assert seq % TQ == 0 and TQ % SUB == 0
    tile = lambda b, i: (b, i, 0)
    whole = lambda b, i: (b, 0, 0)
    return pl.pallas_call(
        _mixer_kernel,
        grid=(bsz, seq // TQ),
        in_specs=[pl.BlockSpec(memory_space=pltpu.SMEM),
                  pl.BlockSpec((1, TQ, q.shape[2]), tile),
                  pl.BlockSpec((1,) + k.shape[1:], whole),
                  pl.BlockSpec((1,) + vt.shape[1:], whole),
                  _const_spec(bias_a.shape), _const_spec(bias_b.shape),
                  _const_spec((QA, 1)), _const_spec((QB, 1)),
                  _const_spec(wo.shape), _const_spec((1, d)), _const_spec((1, d)),
                  pl.BlockSpec((1, N_MOD, d), whole),
                  pl.BlockSpec((1, TQ, d), tile)],
        out_specs=pl.BlockSpec((1, TQ, d), tile),
        out_shape=jax.ShapeDtypeStruct((bsz, seq, d), F32),
        scratch_shapes=[pltpu.VMEM((NSUB, KV_A, PAD_A, G_A * SUB), F32),
                        pltpu.VMEM((NSUB, H_B // 2, PAD, 2 * SUB), F32)],
        compiler_params=pltpu.CompilerParams(dimension_semantics=("arbitrary", "arbitrary"),
                                             vmem_limit_bytes=VMEM_LIMIT),
        name="mixer",
    )(sinks, q, k, vt, bias_a, bias_b, gga.reshape(QA, 1), ggb.reshape(QB, 1), wo,
      bo.reshape(1, d), g_post.reshape(1, d), mod, x)


def _pair_heads_a(w, axis):
    shape = w.shape
    split = shape[:axis] + (KV_A, G_A, HEAD_DIM) + shape[axis + 1:]
    return jnp.swapaxes(w.reshape(split), axis, axis + 1).reshape(shape)


def kernel(x, c, w_ada, b_ada, g_pre_ffn1, w_gate1, w_up1, w_down1, g_post_ffn1, g_pre_mix, w_in, b_in, sinks_a, rel_bias_b, g_grp_a, g_grp_b, w_out, b_out, g_post_mix, g_pre_ffn2, w_gate2, w_up2, w_down2, g_post_ffn2):
    bsz, seq, d = x.shape
    depth = w_ada.shape[0]
    for l in range(depth):
        mod = _modulation(c, w_ada[l], b_ada[l]).reshape(bsz, N_MOD, d)

        wi, bi = w_in[l], b_in[l]
        o_ka, o_va, o_qb, o_kb, o_vb = QA, QA + KVA, QA + 2 * KVA, QA + 2 * KVA + QB, QA + 2 * KVA + 2 * QB
        wq = jnp.concatenate([_pair_heads_a(wi[:, :QA], 1), wi[:, o_qb:o_kb]], axis=1).astype(BF16)
        bq = jnp.concatenate([_pair_heads_a(bi[:QA], 0), bi[o_qb:o_kb]]).reshape(1, -1)
        wk = jnp.concatenate([wi[:, o_ka:o_va], wi[:, o_kb:o_vb]], axis=1).astype(BF16)
        bk = jnp.concatenate([bi[o_ka:o_va], bi[o_kb:o_vb]]).reshape(1, -1)
        wvt = jnp.concatenate([wi[:, o_va:o_qb], wi[:, o_vb:]], axis=1).T.astype(BF16)
        bv = jnp.concatenate([bi[o_va:o_qb], bi[o_vb:]]).reshape(-1, 1)
        x, q, k, vt = _ffn(x, mod, g_pre_ffn1[l], g_post_ffn1[l],
                           w_gate1[l].astype(BF16), w_up1[l].astype(BF16), w_down1[l].astype(BF16),
                           mod0=0, proj=(g_pre_mix[l], wq, bq, wk, bk, wvt, bv))

        bias_a, bias_b = _bias_tiles(rel_bias_b[l])
        wo = jnp.concatenate([_pair_heads_a(w_out[l][:QA], 0), w_out[l][QA:]], axis=0).astype(BF16)
        x = _mixer(x, mod, q, k, vt, bias_a, bias_b, sinks_a[l], _pair_heads_a(g_grp_a[l], 0),
                   g_grp_b[l], wo, b_out[l], g_post_mix[l])

        x = _ffn(x, mod, g_pre_ffn2[l], g_post_ffn2[l],
                 w_gate2[l].astype(BF16), w_up2[l].astype(BF16), w_down2[l].astype(BF16), mod0=6)
    return x
```

```python
import functools
import math

import jax
import jax.numpy as jnp
from jax import lax
from jax.experimental import pallas as pl
from jax.experimental.pallas import tpu as pltpu

F32 = jnp.float32
BF16 = jnp.bfloat16

CHUNK = 64
HEAD_DIM = 64
EPS = 1e-6
NEG_INF = -1e30
H_A, KV_A = 8, 2
G_A = H_A // KV_A
BACK_A = 2
H_B = 8
BACK_B = 8
REL_CLIP = 128
N_REL = 2 * REL_CLIP + 1
QA = H_A * HEAD_DIM
KVA = KV_A * HEAD_DIM
QB = H_B * HEAD_DIM
N_MOD = 9
LOG2E = math.log2(math.e)

LANES = 128
TQ = 256
SUB = LANES
NSUB = TQ // SUB
SCORE_AHEAD = 3
PAD = BACK_B * CHUNK
PAD_A = BACK_A * CHUNK
WIN_A = SUB + PAD_A
WIN_B = SUB + PAD
ROLL_W = 1024
TM_FFN = 1024
FFN_SLAB = 512
W_ROWS = 256
TM_PROJ = 512
KV_LEAD = TM_PROJ
TN_MOD = 1024
FF_CHUNK = 256
VMEM_LIMIT = 56 * 1024 * 1024

_NT = (((1,), (1,)), ((), ()))
_TN = (((0,), (0,)), ((), ()))


def _rms(x, g):
    ms = jnp.mean(x * x, axis=-1, keepdims=True)
    return x * lax.rsqrt(ms + EPS) * g


def _const_spec(shape):
    nd = len(shape)
    return pl.BlockSpec(shape, lambda *_: (0,) * nd, pipeline_mode=pl.Buffered(1))


def _mod_kernel(c_ref, w_ref, b_ref, o_ref):
    c = c_ref[...]
    a = (c * jax.nn.sigmoid(c)).astype(BF16)
    o_ref[...] = jnp.dot(a, w_ref[...].astype(BF16), preferred_element_type=F32) + b_ref[...]


def _modulation(c, w_ada, b_ada):
    bsz, d = c.shape
    n = w_ada.shape[1]
    assert n % TN_MOD == 0
    return pl.pallas_call(
        _mod_kernel,
        grid=(n // TN_MOD,),
        in_specs=[pl.BlockSpec((bsz, d), lambda j: (0, 0)),
                  pl.BlockSpec((d, TN_MOD), lambda j: (0, j)),
                  pl.BlockSpec((1, TN_MOD), lambda j: (0, j))],
        out_specs=pl.BlockSpec((bsz, TN_MOD), lambda j: (0, j)),
        out_shape=jax.ShapeDtypeStruct((bsz, n), F32),
        compiler_params=pltpu.CompilerParams(dimension_semantics=("arbitrary",),
                                             vmem_limit_bytes=VMEM_LIMIT),
        name="adaln_mod",
    )(c, w_ada, b_ada.reshape(1, n))


def _load_bf16(src_hbm, dst_ref, stage, sem):
    ncol = src_hbm.shape[1]
    nblk = src_hbm.shape[0] // W_ROWS

    def copy(n):
        return pltpu.make_async_copy(src_hbm.at[pl.ds(n * W_ROWS, W_ROWS), :],
                                     stage.at[n % 2, :, pl.ds(0, ncol)], sem.at[n % 2])

    copy(0).start()
    for n in range(nblk):
        if n + 1 < nblk:
            copy(n + 1).start()
        copy(n).wait()
        dst_ref[n * W_ROWS:(n + 1) * W_ROWS, :] = stage[n % 2, :, 0:ncol].astype(dst_ref.dtype)


def _ffn_kernel(x_ref, mod_ref, gpre_ref, gpost_ref, wg_hbm, wu_hbm, wd_hbm, o_ref,
                wg_ref, wu_ref, wd_ref, stage, sem, *, mod0):
    @pl.when((pl.program_id(0) == 0) & (pl.program_id(1) == 0))
    def _():
        _load_bf16(wg_hbm, wg_ref, stage, sem)
        _load_bf16(wu_hbm, wu_ref, stage, sem)
        _load_bf16(wd_hbm, wd_ref, stage, sem)

    shift = mod_ref[0, pl.ds(mod0, 1), :]
    scale = mod_ref[0, pl.ds(mod0 + 1, 1), :]
    gate = mod_ref[0, pl.ds(mod0 + 2, 1), :]
    d_ff = wg_ref.shape[1]
    nslab = x_ref.shape[1] // FFN_SLAB
    nchunk = d_ff // FF_CHUNK

    def rows(s):
        return slice(s * FFN_SLAB, (s + 1) * FFN_SLAB)

    def pre(s):
        return (_rms(x_ref[0, rows(s), :], gpre_ref[...]) * (1.0 + scale) + shift).astype(BF16)

    def post(s, acc):
        o_ref[0, rows(s), :] = x_ref[0, rows(s), :] + (0.5 * gate) * _rms(acc, gpost_ref[...])

    h = pre(0)
    done = None
    for s in range(nslab):
        acc = jnp.zeros((FFN_SLAB, x_ref.shape[2]), F32)
        h_next = None
        for c in range(nchunk):
            if c == nchunk // 4 and done is not None:
                post(s - 1, done)
            if c == nchunk // 2 and s + 1 < nslab:
                h_next = pre(s + 1)
            f0 = c * FF_CHUNK
            g = jnp.dot(h, wg_ref[:, f0:f0 + FF_CHUNK], preferred_element_type=F32)
            u = jnp.dot(h, wu_ref[:, f0:f0 + FF_CHUNK], preferred_element_type=F32)
            a = (g * jax.nn.sigmoid(g) * u).astype(BF16)
            acc = acc + jnp.dot(a, wd_ref[f0:f0 + FF_CHUNK, :], preferred_element_type=F32)
        done, h = acc, h_next
    post(nslab - 1, done)


def _ffn(x, mod, g_pre, g_post, wg, wu, wd, *, mod0):
    bsz, seq, d = x.shape
    d_ff = wg.shape[1]
    assert seq % TM_FFN == 0 and TM_FFN % FFN_SLAB == 0 and d_ff % FF_CHUNK == 0
    assert d % W_ROWS == 0 and d_ff % W_ROWS == 0
    blk = lambda b, j: (b, j, 0)
    hbm = pl.BlockSpec(memory_space=pl.ANY)
    return pl.pallas_call(
        functools.partial(_ffn_kernel, mod0=mod0),
        grid=(bsz, seq // TM_FFN),
        in_specs=[pl.BlockSpec((1, TM_FFN, d), blk),
                  pl.BlockSpec((1, N_MOD, d), lambda b, j: (b, 0, 0)),
                  _const_spec((1, d)), _const_spec((1, d)),
                  hbm, hbm, hbm],
        out_specs=pl.BlockSpec((1, TM_FFN, d), blk),
        out_shape=jax.ShapeDtypeStruct((bsz, seq, d), F32),
        scratch_shapes=[pltpu.VMEM((d, d_ff), BF16), pltpu.VMEM((d, d_ff), BF16),
                        pltpu.VMEM((d_ff, d), BF16),
                        pltpu.VMEM((2, W_ROWS, max(d, d_ff)), F32),
                        pltpu.SemaphoreType.DMA((2,))],
        compiler_params=pltpu.CompilerParams(dimension_semantics=("arbitrary", "arbitrary"),
                                             vmem_limit_bytes=VMEM_LIMIT),
        name="ffn",
    )(x, mod, g_pre.reshape(1, d), g_post.reshape(1, d), wg, wu, wd)


def _proj_kernel(x_ref, mod_ref, gpre_ref, wq_ref, bq_ref, wk_ref, bk_ref, wvt_ref, bv_ref,
                 q_ref, k_ref, vt_ref):
    j = pl.program_id(1)

    @pl.when(j == 0)
    def _():
        k_ref[...] = jnp.zeros(k_ref.shape, k_ref.dtype)
        vt_ref[...] = jnp.zeros(vt_ref.shape, vt_ref.dtype)

    @pl.when(j > 0)
    def _():
        shift = mod_ref[0, pl.ds(3, 1), :]
        scale = mod_ref[0, pl.ds(4, 1), :]
        h = (_rms(x_ref[0], gpre_ref[...]) * (1.0 + scale) + shift).astype(BF16)
        q = jnp.dot(h, wq_ref[...], preferred_element_type=F32) + bq_ref[...]
        q_ref[0] = (q * (HEAD_DIM ** -0.5 * LOG2E)).astype(BF16)
        k = jnp.dot(h, wk_ref[...], preferred_element_type=F32) + bk_ref[...]
        k_ref[0] = k.astype(BF16)
        vt = lax.dot_general(wvt_ref[...], h, _NT, preferred_element_type=F32) + bv_ref[...]
        vt_ref[0] = vt.astype(BF16)


def _projection(x, mod, g_pre, wq, bq, wk, bk, wvt, bv):
    bsz, seq, d = x.shape
    nq, nk, nv = wq.shape[1], wk.shape[1], wvt.shape[0]
    assert KV_LEAD == TM_PROJ and seq % TM_PROJ == 0
    nblk = seq // TM_PROJ
    prev = lambda b, j: (b, jnp.maximum(j - 1, 0), 0)
    return pl.pallas_call(
        _proj_kernel,
        grid=(bsz, nblk + 1),
        in_specs=[pl.BlockSpec((1, TM_PROJ, d), prev),
                  pl.BlockSpec((1, N_MOD, d), lambda b, j: (b, 0, 0)),
                  _const_spec((1, d)),
                  _const_spec((d, nq)), _const_spec((1, nq)),
                  _const_spec((d, nk)), _const_spec((1, nk)),
                  _const_spec((nv, d)), _const_spec((nv, 1))],
        out_specs=[pl.BlockSpec((1, TM_PROJ, nq), prev),
                   pl.BlockSpec((1, TM_PROJ, nk), lambda b, j: (b, j, 0)),
                   pl.BlockSpec((1, nv, TM_PROJ), lambda b, j: (b, 0, j))],
        out_shape=[jax.ShapeDtypeStruct((bsz, seq, nq), BF16),
                   jax.ShapeDtypeStruct((bsz, seq + KV_LEAD, nk), BF16),
                   jax.ShapeDtypeStruct((bsz, nv, seq + KV_LEAD), BF16)],
        compiler_params=pltpu.CompilerParams(dimension_semantics=("arbitrary", "arbitrary"),
                                             vmem_limit_bytes=VMEM_LIMIT),
        name="qkv_proj",
    )(x, mod, g_pre.reshape(1, d), wq, bq, wk, bk, wvt, bv)


def _bias_kernel(rel_ref, ba_ref, bb_ref):
    j = lax.broadcasted_iota(jnp.int32, (WIN_A, G_A * SUB), 0)
    r = lax.broadcasted_iota(jnp.int32, (WIN_A, G_A * SUB), 1)
    g = r // SUB
    i = r % SUB
    dist = jnp.abs(PAD_A + i - j).astype(F32)
    in_band = (j // CHUNK >= i // CHUNK) & (j // CHUNK <= i // CHUNK + BACK_A)
    for k in range(KV_A):
        slope = jnp.zeros(r.shape, F32)
        for gg in range(G_A):
            slope = jnp.where(g == gg, 2.0 ** (-8.0 * (k * G_A + gg + 1) / H_A), slope)
        ba_ref[k] = jnp.where(in_band, (-slope * dist) * LOG2E, NEG_INF)

    table = rel_ref[...]
    dcol = lax.broadcasted_iota(jnp.int32, (H_B, ROLL_W), 1)
    d = jnp.where(dcol < SUB, dcol, dcol - ROLL_W)
    idx = jnp.clip(PAD + d, -REL_CLIP, REL_CLIP) + REL_CLIP
    row = jnp.zeros((H_B, ROLL_W), F32)
    for n in range(N_REL):
        row = jnp.where(idx == n, table[:, n:n + 1], row)
    jb = lax.broadcasted_iota(jnp.int32, (WIN_B, SUB), 0)
    ib = lax.broadcasted_iota(jnp.int32, (WIN_B, SUB), 1)
    in_band_b = (jb // CHUNK >= ib // CHUNK) & (jb // CHUNK <= ib // CHUNK + BACK_B)
    for h in range(H_B):
        tile = jnp.broadcast_to(row[h:h + 1, :], (WIN_B, ROLL_W))
        tile = pltpu.roll(tile, 0, 1, stride=1, stride_axis=0)[:, :SUB]
        bb_ref[h // 2, :, pl.ds((h % 2) * SUB, SUB)] = jnp.where(in_band_b, tile * LOG2E, NEG_INF)


def _bias_tiles(rel_bias):
    return pl.pallas_call(
        _bias_kernel,
        out_shape=[jax.ShapeDtypeStruct((KV_A, WIN_A, G_A * SUB), F32),
                   jax.ShapeDtypeStruct((H_B // 2, WIN_B, 2 * SUB), F32)],
        compiler_params=pltpu.CompilerParams(vmem_limit_bytes=VMEM_LIMIT),
        name="score_bias",
    )(rel_bias)


def _rms_rows(xt, g_col):
    ms = jnp.mean(xt * xt, axis=0, keepdims=True)
    return xt * lax.rsqrt(ms + EPS) * g_col


def _mixer_kernel(sink_ref, q_ref, k_ref, vt_ref, ba_ref, bb_ref, gga_ref, ggb_ref, wo_ref, bo_ref,
                  gpost_ref, mod_ref, x_ref, o_ref, ba_pad, bb_pad):
    i = pl.program_id(1)
    ws_b = [pl.multiple_of(i * TQ + u * SUB + KV_LEAD - PAD, SUB) for u in range(NSUB)]
    ws_a = [pl.multiple_of(i * TQ + u * SUB + KV_LEAD - PAD_A, LANES) for u in range(NSUB)]

    @pl.when(i * TQ <= PAD)
    def _():
        ja = lax.broadcasted_iota(jnp.int32, ba_pad.shape[1:], 1)
        jb = lax.broadcasted_iota(jnp.int32, bb_pad.shape[1:], 1)
        for u in range(NSUB):
            ba_pad[u] = jnp.where(ws_a[u] + ja >= KV_LEAD, ba_ref[:, :PAD_A, :], NEG_INF)
            bb_pad[u] = jnp.where(ws_b[u] + jb >= KV_LEAD, bb_ref[:, :PAD, :], NEG_INF)

    low = lax.broadcasted_iota(jnp.int32, (1, LANES), 1) < HEAD_DIM
    halves = (low, jnp.logical_not(low))
    col_g = lax.broadcasted_iota(jnp.int32, (1, G_A * SUB), 1) // SUB

    def scores_a(u, k):
        k_a = k_ref[0, pl.ds(ws_a[u], WIN_A), 0:LANES]
        qs = jnp.concatenate(
            [jnp.where(halves[k], q_ref[0, u * SUB:(u + 1) * SUB, g * LANES:(g + 1) * LANES], 0)
             for g in range(G_A)], axis=0)
        s = lax.dot_general(k_a, qs, _NT, preferred_element_type=F32)
        return jnp.concatenate([s[:PAD_A] + ba_pad[u, k], s[PAD_A:] + ba_ref[k, PAD_A:, :]], axis=0)

    def attend_a(u, k, s):
        sink = jnp.zeros((1, G_A * SUB), F32)
        for g in range(G_A):
            sink = jnp.where(col_g == g, sink_ref[k * G_A + g] * LOG2E, sink)
        m = jnp.maximum(jnp.max(s, axis=0, keepdims=True), sink)
        e = jnp.exp2(s - m)
        den = jnp.sum(e, axis=0, keepdims=True) + jnp.exp2(sink - m)
        vt_a = vt_ref[0, 0:LANES, pl.ds(ws_a[u], WIN_A)]
        o = jnp.dot(vt_a, e.astype(BF16), preferred_element_type=F32)
        o = o[k * HEAD_DIM:(k + 1) * HEAD_DIM] * (1.0 / den)
        return [o[:, g * SUB:(g + 1) * SUB] for g in range(G_A)]

    def scores_b(u, t):
        q_t = q_ref[0, u * SUB:(u + 1) * SUB, QA + t * LANES:QA + (t + 1) * LANES]
        k_t = k_ref[0, pl.ds(ws_b[u], WIN_B), (1 + t) * LANES:(2 + t) * LANES]
        q2 = jnp.concatenate([jnp.where(halves[0], q_t, 0), jnp.where(halves[1], q_t, 0)], axis=0)
        s = lax.dot_general(k_t, q2, _NT, preferred_element_type=F32)
        return jnp.concatenate([s[:PAD] + bb_pad[u, t], s[PAD:] + bb_ref[t, PAD:, :]], axis=0)

    def attend_b(u, t, s):
        m = jnp.max(s, axis=0, keepdims=True)
        e = jnp.exp2(s - m)
        rden = 1.0 / jnp.sum(e, axis=0, keepdims=True)
        vt_t = vt_ref[0, (1 + t) * LANES:(2 + t) * LANES, pl.ds(ws_b[u], WIN_B)]
        o = jnp.dot(vt_t, e.astype(BF16), preferred_element_type=F32)
        return [o[:HEAD_DIM, :SUB] * rden[:, :SUB], o[HEAD_DIM:, SUB:] * rden[:, SUB:]]

    groups = []
    for u in range(NSUB):
        groups += [(scores_a, attend_a, u, k) for k in range(KV_A)]
        groups += [(scores_b, attend_b, u, t) for t in range(H_B // 2)]
    outs = []
    pending = [grp[0](*grp[2:]) for grp in groups[:SCORE_AHEAD]]
    for n, (_, attend, u, idx) in enumerate(groups):
        if n + SCORE_AHEAD < len(groups):
            nxt = groups[n + SCORE_AHEAD]
            pending.append(nxt[0](*nxt[2:]))
        outs.append(attend(u, idx, pending.pop(0)))
    per_sub = len(groups) // NSUB
    oa_t, ob_t = [], []
    for u in range(NSUB):
        sub = outs[u * per_sub:(u + 1) * per_sub]
        oa_t.append(jnp.concatenate([sub[k][g] for g in range(G_A) for k in range(KV_A)], axis=0))
        ob_t.append(jnp.concatenate([part for pair in sub[KV_A:] for part in pair], axis=0))
    oa_t = jnp.concatenate(oa_t, axis=1)
    ob_t = jnp.concatenate(ob_t, axis=1)

    y_t = jnp.concatenate([_rms_rows(oa_t, gga_ref[...]), _rms_rows(ob_t, ggb_ref[...])], axis=0)
    z = lax.dot_general(y_t.astype(BF16), wo_ref[...], _TN, preferred_element_type=F32) + bo_ref[...]
    gate = mod_ref[0, pl.ds(5, 1), :]
    o_ref[0] = x_ref[0] + gate * _rms(z, gpost_ref[...])


def _mixer(x, mod, q, k, vt, bias_a, bias_b, sinks, gga, ggb, wo, bo, g_post):
    bsz, seq, d = x.shape
    assert seq % TQ == 0 and TQ % SUB == 0
    tile = lambda b, i: (b, i, 0)
    whole = lambda b, i: (b, 0, 0)
    return pl.pallas_call(
        _mixer_kernel,
        grid=(bsz, seq // TQ),
        in_specs=[pl.BlockSpec(memory_space=pltpu.SMEM),
                  pl.BlockSpec((1, TQ, q.shape[2]), tile),
                  pl.BlockSpec((1,) + k.shape[1:], whole),
                  pl.BlockSpec((1,) + vt.shape[1:], whole),
                  _const_spec(bias_a.shape), _const_spec(bias_b.shape),
                  _const_spec((QA, 1)), _const_spec((QB, 1)),
                  _const_spec(wo.shape), _const_spec((1, d)), _const_spec((1, d)),
                  pl.BlockSpec((1, N_MOD, d), whole),
                  pl.BlockSpec((1, TQ, d), tile)],
        out_specs=pl.BlockSpec((1, TQ, d), tile),
        out_shape=jax.ShapeDtypeStruct((bsz, seq, d), F32),
        scratch_shapes=[pltpu.VMEM((NSUB, KV_A, PAD_A, G_A * SUB), F32),
                        pltpu.VMEM((NSUB, H_B // 2, PAD, 2 * SUB), F32)],
        compiler_params=pltpu.CompilerParams(dimension_semantics=("arbitrary", "arbitrary"),
                                             vmem_limit_bytes=VMEM_LIMIT),
        name="mixer",
    )(sinks, q, k, vt, bias_a, bias_b, gga.reshape(QA, 1), ggb.reshape(QB, 1), wo,
      bo.reshape(1, d), g_post.reshape(1, d), mod, x)


def _pair_heads_a(w, axis):
    shape = w.shape
    split = shape[:axis] + (KV_A, G_A, HEAD_DIM) + shape[axis + 1:]
    return jnp.swapaxes(w.reshape(split), axis, axis + 1).reshape(shape)


def kernel(x, c, w_ada, b_ada, g_pre_ffn1, w_gate1, w_up1, w_down1, g_post_ffn1, g_pre_mix, w_in, b_in, sinks_a, rel_bias_b, g_grp_a, g_grp_b, w_out, b_out, g_post_mix, g_pre_ffn2, w_gate2, w_up2, w_down2, g_post_ffn2):
    bsz, seq, d = x.shape
    depth = w_ada.shape[0]
    for l in range(depth):
        mod = _modulation(c, w_ada[l], b_ada[l]).reshape(bsz, N_MOD, d)

        x = _ffn(x, mod, g_pre_ffn1[l], g_post_ffn1[l], w_gate1[l], w_up1[l], w_down1[l], mod0=0)

        wi, bi = w_in[l], b_in[l]
        o_ka, o_va, o_qb, o_kb, o_vb = QA, QA + KVA, QA + 2 * KVA, QA + 2 * KVA + QB, QA + 2 * KVA + 2 * QB
        wq = jnp.concatenate([_pair_heads_a(wi[:, :QA], 1), wi[:, o_qb:o_kb]], axis=1).astype(BF16)
        bq = jnp.concatenate([_pair_heads_a(bi[:QA], 0), bi[o_qb:o_kb]]).reshape(1, -1)
        wk = jnp.concatenate([wi[:, o_ka:o_va], wi[:, o_kb:o_vb]], axis=1).astype(BF16)
        bk = jnp.concatenate([bi[o_ka:o_va], bi[o_kb:o_vb]]).reshape(1, -1)
        wvt = jnp.concatenate([wi[:, o_va:o_qb], wi[:, o_vb:]], axis=1).T.astype(BF16)
        bv = jnp.concatenate([bi[o_va:o_qb], bi[o_vb:]]).reshape(-1, 1)
        q, k, vt = _projection(x, mod, g_pre_mix[l], wq, bq, wk, bk, wvt, bv)

        bias_a, bias_b = _bias_tiles(rel_bias_b[l])
        wo = jnp.concatenate([_pair_heads_a(w_out[l][:QA], 0), w_out[l][QA:]], axis=0).astype(BF16)
        x = _mixer(x, mod, q, k, vt, bias_a, bias_b, sinks_a[l], _pair_heads_a(g_grp_a[l], 0),
                   g_grp_b[l], wo, b_out[l], g_post_mix[l])

        x = _ffn(x, mod, g_pre_ffn2[l], g_post_ffn2[l], w_gate2[l], w_up2[l], w_down2[l], mod0=6)
    return x
```

```python
import functools
import math

import jax
import jax.numpy as jnp
from jax import lax
from jax.experimental import pallas as pl
from jax.experimental.pallas import tpu as pltpu

F32 = jnp.float32
BF16 = jnp.bfloat16

CHUNK = 64
HEAD_DIM = 64
EPS = 1e-6
NEG_INF = -1e30
H_A, KV_A = 8, 2
G_A = H_A // KV_A
BACK_A = 2
H_B = 8
BACK_B = 8
REL_CLIP = 128
N_REL = 2 * REL_CLIP + 1
QA = H_A * HEAD_DIM
KVA = KV_A * HEAD_DIM
QB = H_B * HEAD_DIM
N_MOD = 9
LOG2E = math.log2(math.e)

LANES = 128
TQ = 256
SUB = LANES
NSUB = TQ // SUB
SCORE_AHEAD = 3
PAD = BACK_B * CHUNK
PAD_A = BACK_A * CHUNK
WIN_A = SUB + PAD_A
WIN_B = SUB + PAD
NV_A = PAD_A // SUB + 1
NV_B = PAD // SUB + 1
DEN_ROWS = 16
ROLL_W = 1024
TM_FFN = 1024
FFN_SLAB = 512
W_ROWS = 256
TM_PROJ = 512
KV_LEAD = TM_PROJ
TN_MOD = 1024
FF_CHUNK = 256
VMEM_LIMIT = 56 * 1024 * 1024

_NT = (((1,), (1,)), ((), ()))
_TN = (((0,), (0,)), ((), ()))


def _rms(x, g):
    ms = jnp.mean(x * x, axis=-1, keepdims=True)
    return x * lax.rsqrt(ms + EPS) * g


def _const_spec(shape):
    nd = len(shape)
    return pl.BlockSpec(shape, lambda *_: (0,) * nd, pipeline_mode=pl.Buffered(1))


def _mod_kernel(c_ref, w_ref, b_ref, o_ref):
    c = c_ref[...]
    a = (c * jax.nn.sigmoid(c)).astype(BF16)
    o_ref[...] = jnp.dot(a, w_ref[...].astype(BF16), preferred_element_type=F32) + b_ref[...]


def _modulation(c, w_ada, b_ada):
    bsz, d = c.shape
    n = w_ada.shape[1]
    assert n % TN_MOD == 0
    return pl.pallas_call(
        _mod_kernel,
        grid=(n // TN_MOD,),
        in_specs=[pl.BlockSpec((bsz, d), lambda j: (0, 0)),
                  pl.BlockSpec((d, TN_MOD), lambda j: (0, j)),
                  pl.BlockSpec((1, TN_MOD), lambda j: (0, j))],
        out_specs=pl.BlockSpec((bsz, TN_MOD), lambda j: (0, j)),
        out_shape=jax.ShapeDtypeStruct((bsz, n), F32),
        compiler_params=pltpu.CompilerParams(dimension_semantics=("arbitrary",),
                                             vmem_limit_bytes=VMEM_LIMIT),
        name="adaln_mod",
    )(c, w_ada, b_ada.reshape(1, n))


def _load_bf16(src_hbm, dst_ref, stage, sem):
    ncol = src_hbm.shape[1]
    nblk = src_hbm.shape[0] // W_ROWS

    def copy(n):
        return pltpu.make_async_copy(src_hbm.at[pl.ds(n * W_ROWS, W_ROWS), :],
                                     stage.at[n % 2, :, pl.ds(0, ncol)], sem.at[n % 2])

    copy(0).start()
    for n in range(nblk):
        if n + 1 < nblk:
            copy(n + 1).start()
        copy(n).wait()
        dst_ref[n * W_ROWS:(n + 1) * W_ROWS, :] = stage[n % 2, :, 0:ncol].astype(dst_ref.dtype)


def _ffn_kernel(x_ref, mod_ref, gpre_ref, gpost_ref, wg_hbm, wu_hbm, wd_hbm, o_ref,
                wg_ref, wu_ref, wd_ref, stage, sem, *, mod0):
    @pl.when((pl.program_id(0) == 0) & (pl.program_id(1) == 0))
    def _():
        _load_bf16(wg_hbm, wg_ref, stage, sem)
        _load_bf16(wu_hbm, wu_ref, stage, sem)
        _load_bf16(wd_hbm, wd_ref, stage, sem)

    shift = mod_ref[0, pl.ds(mod0, 1), :]
    scale = mod_ref[0, pl.ds(mod0 + 1, 1), :]
    gate = mod_ref[0, pl.ds(mod0 + 2, 1), :]
    d_ff = wg_ref.shape[1]
    nslab = x_ref.shape[1] // FFN_SLAB
    nchunk = d_ff // FF_CHUNK

    def rows(s):
        return slice(s * FFN_SLAB, (s + 1) * FFN_SLAB)

    def pre(s):
        return (_rms(x_ref[0, rows(s), :], gpre_ref[...]) * (1.0 + scale) + shift).astype(BF16)

    def post(s, acc):
        o_ref[0, rows(s), :] = x_ref[0, rows(s), :] + (0.5 * gate) * _rms(acc, gpost_ref[...])

    h = pre(0)
    done = None
    for s in range(nslab):
        acc = jnp.zeros((FFN_SLAB, x_ref.shape[2]), F32)
        h_next = None
        for c in range(nchunk):
            if c == nchunk // 4 and done is not None:
                post(s - 1, done)
            if c == nchunk // 2 and s + 1 < nslab:
                h_next = pre(s + 1)
            f0 = c * FF_CHUNK
            g = jnp.dot(h, wg_ref[:, f0:f0 + FF_CHUNK], preferred_element_type=F32)
            u = jnp.dot(h, wu_ref[:, f0:f0 + FF_CHUNK], preferred_element_type=F32)
            a = (g * jax.nn.sigmoid(g) * u).astype(BF16)
            acc = acc + jnp.dot(a, wd_ref[f0:f0 + FF_CHUNK, :], preferred_element_type=F32)
        done, h = acc, h_next
    post(nslab - 1, done)


def _ffn(x, mod, g_pre, g_post, wg, wu, wd, *, mod0):
    bsz, seq, d = x.shape
    d_ff = wg.shape[1]
    assert seq % TM_FFN == 0 and TM_FFN % FFN_SLAB == 0 and d_ff % FF_CHUNK == 0
    assert d % W_ROWS == 0 and d_ff % W_ROWS == 0
    blk = lambda b, j: (b, j, 0)
    hbm = pl.BlockSpec(memory_space=pl.ANY)
    return pl.pallas_call(
        functools.partial(_ffn_kernel, mod0=mod0),
        grid=(bsz, seq // TM_FFN),
        in_specs=[pl.BlockSpec((1, TM_FFN, d), blk),
                  pl.BlockSpec((1, N_MOD, d), lambda b, j: (b, 0, 0)),
                  _const_spec((1, d)), _const_spec((1, d)),
                  hbm, hbm, hbm],
        out_specs=pl.BlockSpec((1, TM_FFN, d), blk),
        out_shape=jax.ShapeDtypeStruct((bsz, seq, d), F32),
        scratch_shapes=[pltpu.VMEM((d, d_ff), BF16), pltpu.VMEM((d, d_ff), BF16),
                        pltpu.VMEM((d_ff, d), BF16),
                        pltpu.VMEM((2, W_ROWS, max(d, d_ff)), F32),
                        pltpu.SemaphoreType.DMA((2,))],
        compiler_params=pltpu.CompilerParams(dimension_semantics=("arbitrary", "arbitrary"),
                                             vmem_limit_bytes=VMEM_LIMIT),
        name="ffn",
    )(x, mod, g_pre.reshape(1, d), g_post.reshape(1, d), wg, wu, wd)


def _proj_kernel(x_ref, mod_ref, gpre_ref, wq_ref, bq_ref, wk_ref, bk_ref, wvt_ref, bv_ref,
                 q_ref, k_ref, vt_ref):
    j = pl.program_id(1)

    @pl.when(j == 0)
    def _():
        k_ref[...] = jnp.zeros(k_ref.shape, k_ref.dtype)
        vt_ref[...] = jnp.zeros(vt_ref.shape, vt_ref.dtype)

    @pl.when(j > 0)
    def _():
        shift = mod_ref[0, pl.ds(3, 1), :]
        scale = mod_ref[0, pl.ds(4, 1), :]
        h = (_rms(x_ref[0], gpre_ref[...]) * (1.0 + scale) + shift).astype(BF16)
        q = jnp.dot(h, wq_ref[...], preferred_element_type=F32) + bq_ref[...]
        q_ref[0] = (q * (HEAD_DIM ** -0.5 * LOG2E)).astype(BF16)
        k = jnp.dot(h, wk_ref[...], preferred_element_type=F32) + bk_ref[...]
        k_ref[0] = k.astype(BF16)
        vt = lax.dot_general(wvt_ref[...], h, _NT, preferred_element_type=F32) + bv_ref[...]
        vt_ref[0] = vt.astype(BF16)


def _projection(x, mod, g_pre, wq, bq, wk, bk, wvt, bv):
    bsz, seq, d = x.shape
    nq, nk, nv = wq.shape[1], wk.shape[1], wvt.shape[0]
    assert KV_LEAD == TM_PROJ and seq % TM_PROJ == 0
    nblk = seq // TM_PROJ
    prev = lambda b, j: (b, jnp.maximum(j - 1, 0), 0)
    return pl.pallas_call(
        _proj_kernel,
        grid=(bsz, nblk + 1),
        in_specs=[pl.BlockSpec((1, TM_PROJ, d), prev),
                  pl.BlockSpec((1, N_MOD, d), lambda b, j: (b, 0, 0)),
                  _const_spec((1, d)),
                  _const_spec((d, nq)), _const_spec((1, nq)),
                  _const_spec((d, nk)), _const_spec((1, nk)),
                  _const_spec((nv, d)), _const_spec((nv, 1))],
        out_specs=[pl.BlockSpec((1, TM_PROJ, nq), prev),
                   pl.BlockSpec((1, TM_PROJ, nk), lambda b, j: (b, j, 0)),
                   pl.BlockSpec((1, nv, TM_PROJ), lambda b, j: (b, 0, j))],
        out_shape=[jax.ShapeDtypeStruct((bsz, seq, nq), BF16),
                   jax.ShapeDtypeStruct((bsz, seq + KV_LEAD, nk), BF16),
                   jax.ShapeDtypeStruct((bsz, nv, seq + KV_LEAD), BF16)],
        compiler_params=pltpu.CompilerParams(dimension_semantics=("arbitrary", "arbitrary"),
                                             vmem_limit_bytes=VMEM_LIMIT),
        name="qkv_proj",
    )(x, mod, g_pre.reshape(1, d), wq, bq, wk, bk, wvt, bv)


def _bias_kernel(rel_ref, ba_ref, bb_ref):
    j = lax.broadcasted_iota(jnp.int32, (WIN_A, G_A * SUB), 0)
    r = lax.broadcasted_iota(jnp.int32, (WIN_A, G_A * SUB), 1)
    g = r // SUB
    i = r % SUB
    dist = jnp.abs(PAD_A + i - j).astype(F32)
    in_band = (j // CHUNK >= i // CHUNK) & (j // CHUNK <= i // CHUNK + BACK_A)
    for k in range(KV_A):
        slope = jnp.zeros(r.shape, F32)
        for gg in range(G_A):
            slope = jnp.where(g == gg, 2.0 ** (-8.0 * (k * G_A + gg + 1) / H_A), slope)
        for v in range(NV_A):
            valid = in_band & (j >= PAD_A - v * SUB)
            ba_ref[v, k] = jnp.where(valid, (-slope * dist) * LOG2E, NEG_INF)

    table = rel_ref[...]
    dcol = lax.broadcasted_iota(jnp.int32, (H_B, ROLL_W), 1)
    d = jnp.where(dcol < SUB, dcol, dcol - ROLL_W)
    idx = jnp.clip(PAD + d, -REL_CLIP, REL_CLIP) + REL_CLIP
    row = jnp.zeros((H_B, ROLL_W), F32)
    for n in range(N_REL):
        row = jnp.where(idx == n, table[:, n:n + 1], row)
    jb = lax.broadcasted_iota(jnp.int32, (WIN_B, SUB), 0)
    ib = lax.broadcasted_iota(jnp.int32, (WIN_B, SUB), 1)
    in_band_b = (jb // CHUNK >= ib // CHUNK) & (jb // CHUNK <= ib // CHUNK + BACK_B)
    for h in range(H_B):
        tile = jnp.broadcast_to(row[h:h + 1, :], (WIN_B, ROLL_W))
        tile = pltpu.roll(tile, 0, 1, stride=1, stride_axis=0)[:, :SUB]
        for v in range(NV_B):
            valid = in_band_b & (jb >= PAD - v * SUB)
            bb_ref[v, h // 2, :, pl.ds((h % 2) * SUB, SUB)] = jnp.where(valid, tile * LOG2E, NEG_INF)


def _bias_tiles(rel_bias):
    return pl.pallas_call(
        _bias_kernel,
        out_shape=[jax.ShapeDtypeStruct((NV_A, KV_A, WIN_A, G_A * SUB), F32),
                   jax.ShapeDtypeStruct((NV_B, H_B // 2, WIN_B, 2 * SUB), F32)],
        compiler_params=pltpu.CompilerParams(vmem_limit_bytes=VMEM_LIMIT),
        name="score_bias",
    )(rel_bias)


def _rms_rows(xt, g_col):
    ms = jnp.mean(xt * xt, axis=0, keepdims=True)
    return xt * lax.rsqrt(ms + EPS) * g_col


def _mixer_kernel(sink_ref, q_ref, k_ref, vt_ref, ba_ref, bb_ref, gga_ref, ggb_ref, wo_ref, bo_ref,
                  gpost_ref, mod_ref, x_ref, o_ref):
    i = pl.program_id(1)
    ws_b = [pl.multiple_of(i * TQ + u * SUB + KV_LEAD - PAD, SUB) for u in range(NSUB)]
    ws_a = [pl.multiple_of(i * TQ + u * SUB + KV_LEAD - PAD_A, LANES) for u in range(NSUB)]
    var_b = [jnp.minimum(i * NSUB + u, NV_B - 1) for u in range(NSUB)]
    var_a = [jnp.minimum(i * NSUB + u, NV_A - 1) for u in range(NSUB)]

    low = lax.broadcasted_iota(jnp.int32, (1, LANES), 1) < HEAD_DIM
    halves = (low, jnp.logical_not(low))
    col_g = lax.broadcasted_iota(jnp.int32, (1, G_A * SUB), 1) // SUB
    ones_a = jnp.ones((DEN_ROWS, WIN_A), BF16)
    ones_b = jnp.ones((DEN_ROWS, WIN_B), BF16)

    def scores_a(u, k):
        k_a = k_ref[0, pl.ds(ws_a[u], WIN_A), 0:LANES]
        qs = jnp.concatenate(
            [jnp.where(halves[k], q_ref[0, u * SUB:(u + 1) * SUB, g * LANES:(g + 1) * LANES], 0)
             for g in range(G_A)], axis=0)
        s = lax.dot_general(k_a, qs, _NT, preferred_element_type=F32)
        return s + ba_ref[var_a[u], k]

    def attend_a(u, k, s):
        sink = jnp.zeros((1, G_A * SUB), F32)
        for g in range(G_A):
            sink = jnp.where(col_g == g, sink_ref[k * G_A + g] * LOG2E, sink)
        m = jnp.maximum(jnp.max(s, axis=0, keepdims=True), sink)
        e = jnp.exp2(s - m).astype(BF16)
        vt_a = jnp.concatenate([vt_ref[0, 0:LANES, pl.ds(ws_a[u], WIN_A)], ones_a], axis=0)
        o = jnp.dot(vt_a, e, preferred_element_type=F32)
        den = o[LANES:LANES + 1] + jnp.exp2(sink - m)
        o = o[k * HEAD_DIM:(k + 1) * HEAD_DIM] * (1.0 / den)
        return [o[:, g * SUB:(g + 1) * SUB] for g in range(G_A)]

    def scores_b(u, t):
        q_t = q_ref[0, u * SUB:(u + 1) * SUB, QA + t * LANES:QA + (t + 1) * LANES]
        k_t = k_ref[0, pl.ds(ws_b[u], WIN_B), (1 + t) * LANES:(2 + t) * LANES]
        q2 = jnp.concatenate([jnp.where(halves[0], q_t, 0), jnp.where(halves[1], q_t, 0)], axis=0)
        s = lax.dot_general(k_t, q2, _NT, preferred_element_type=F32)
        return s + bb_ref[var_b[u], t]

    def attend_b(u, t, s):
        m = jnp.max(s, axis=0, keepdims=True)
        e = jnp.exp2(s - m).astype(BF16)
        vt_t = jnp.concatenate([vt_ref[0, (1 + t) * LANES:(2 + t) * LANES, pl.ds(ws_b[u], WIN_B)],
                                ones_b], axis=0)
        o = jnp.dot(vt_t, e, preferred_element_type=F32)
        rden = 1.0 / o[LANES:LANES + 1]
        return [o[:HEAD_DIM, :SUB] * rden[:, :SUB], o[HEAD_DIM:LANES, SUB:] * rden[:, SUB:]]

    groups = []
    for u in range(NSUB):
        groups += [(scores_a, attend_a, u, k) for k in range(KV_A)]
        groups += [(scores_b, attend_b, u, t) for t in range(H_B // 2)]
    outs = []
    pending = [grp[0](*grp[2:]) for grp in groups[:SCORE_AHEAD]]
    for n, (_, attend, u, idx) in enumerate(groups):
        if n + SCORE_AHEAD < len(groups):
            nxt = groups[n + SCORE_AHEAD]
            pending.append(nxt[0](*nxt[2:]))
        outs.append(attend(u, idx, pending.pop(0)))
    per_sub = len(groups) // NSUB
    oa_t, ob_t = [], []
    for u in range(NSUB):
        sub = outs[u * per_sub:(u + 1) * per_sub]
        oa_t.append(jnp.concatenate([sub[k][g] for g in range(G_A) for k in range(KV_A)], axis=0))
        ob_t.append(jnp.concatenate([part for pair in sub[KV_A:] for part in pair], axis=0))
    oa_t = jnp.concatenate(oa_t, axis=1)
    ob_t = jnp.concatenate(ob_t, axis=1)

    y_t = jnp.concatenate([_rms_rows(oa_t, gga_ref[...]), _rms_rows(ob_t, ggb_ref[...])], axis=0)
    z = lax.dot_general(y_t.astype(BF16), wo_ref[...], _TN, preferred_element_type=F32) + bo_ref[...]
    gate = mod_ref[0, pl.ds(5, 1), :]
    o_ref[0] = x_ref[0] + gate * _rms(z, gpost_ref[...])


def _mixer(x, mod, q, k, vt, bias_a, bias_b, sinks, gga, ggb, wo, bo, g_post):
    bsz, seq, d = x.shape
    assert seq % TQ == 0 and TQ % SUB == 0
    tile = lambda b, i: (b, i, 0)
    whole = lambda b, i: (b, 0, 0)
    return pl.pallas_call(
        _mixer_kernel,
        grid=(bsz, seq // TQ),
        in_specs=[pl.BlockSpec(memory_space=pltpu.SMEM),
                  pl.BlockSpec((1, TQ, q.shape[2]), tile),
                  pl.BlockSpec((1,) + k.shape[1:], whole),
                  pl.BlockSpec((1,) + vt.shape[1:], whole),
                  _const_spec(bias_a.shape), _const_spec(bias_b.shape),
                  _const_spec((QA, 1)), _const_spec((QB, 1)),
                  _const_spec(wo.shape), _const_spec((1, d)), _const_spec((1, d)),
                  pl.BlockSpec((1, N_MOD, d), whole),
                  pl.BlockSpec((1, TQ, d), tile)],
        out_specs=pl.BlockSpec((1, TQ, d), tile),
        out_shape=jax.ShapeDtypeStruct((bsz, seq, d), F32),
        compiler_params=pltpu.CompilerParams(dimension_semantics=("arbitrary", "arbitrary"),
                                             vmem_limit_bytes=VMEM_LIMIT),
        name="mixer",
    )(sinks, q, k, vt, bias_a, bias_b, gga.reshape(QA, 1), ggb.reshape(QB, 1), wo,
      bo.reshape(1, d), g_post.reshape(1, d), mod, x)


def _pair_heads_a(w, axis):
    shape = w.shape
    split = shape[:axis] + (KV_A, G_A, HEAD_DIM) + shape[axis + 1:]
    return jnp.swapaxes(w.reshape(split), axis, axis + 1).reshape(shape)


def kernel(x, c, w_ada, b_ada, g_pre_ffn1, w_gate1, w_up1, w_down1, g_post_ffn1, g_pre_mix, w_in, b_in, sinks_a, rel_bias_b, g_grp_a, g_grp_b, w_out, b_out, g_post_mix, g_pre_ffn2, w_gate2, w_up2, w_down2, g_post_ffn2):
    bsz, seq, d = x.shape
    depth = w_ada.shape[0]
    for l in range(depth):
        mod = _modulation(c, w_ada[l], b_ada[l]).reshape(bsz, N_MOD, d)

        x = _ffn(x, mod, g_pre_ffn1[l], g_post_ffn1[l], w_gate1[l], w_up1[l], w_down1[l], mod0=0)

        wi, bi = w_in[l], b_in[l]
        o_ka, o_va, o_qb, o_kb, o_vb = QA, QA + KVA, QA + 2 * KVA, QA + 2 * KVA + QB, QA + 2 * KVA + 2 * QB
        wq = jnp.concatenate([_pair_heads_a(wi[:, :QA], 1), wi[:, o_qb:o_kb]], axis=1).astype(BF16)
        bq = jnp.concatenate([_pair_heads_a(bi[:QA], 0), bi[o_qb:o_kb]]).reshape(1, -1)
        wk = jnp.concatenate([wi[:, o_ka:o_va], wi[:, o_kb:o_vb]], axis=1).astype(BF16)
        bk = jnp.concatenate([bi[o_ka:o_va], bi[o_kb:o_vb]]).reshape(1, -1)
        wvt = jnp.concatenate([wi[:, o_va:o_qb], wi[:, o_vb:]], axis=1).T.astype(BF16)
        bv = jnp.concatenate([bi[o_va:o_qb], bi[o_vb:]]).reshape(-1, 1)
        q, k, vt = _projection(x, mod, g_pre_mix[l], wq, bq, wk, bk, wvt, bv)

        bias_a, bias_b = _bias_tiles(rel_bias_b[l])
        wo = jnp.concatenate([_pair_heads_a(w_out[l][:QA], 0), w_out[l][QA:]], axis=0).astype(BF16)
        x = _mixer(x, mod, q, k, vt, bias_a, bias_b, sinks_a[l], _pair_heads_a(g_grp_a[l], 0),
                   g_grp_b[l], wo, b_out[l], g_post_mix[l])

        x = _ffn(x, mod, g_pre_ffn2[l], g_post_ffn2[l], w_gate2[l], w_up2[l], w_down2[l], mod0=6)
    return x
```

```python
import functools
import math

import jax
import jax.numpy as jnp
from jax import lax
from jax.experimental import pallas as pl
from jax.experimental.pallas import tpu as pltpu

F32 = jnp.float32
BF16 = jnp.bfloat16

CHUNK = 64
HEAD_DIM = 64
EPS = 1e-6
NEG_INF = -1e30
H_A, KV_A = 8, 2
G_A = H_A // KV_A
BACK_A = 2
H_B = 8
BACK_B = 8
REL_CLIP = 128
N_REL = 2 * REL_CLIP + 1
QA = H_A * HEAD_DIM
KVA = KV_A * HEAD_DIM
QB = H_B * HEAD_DIM
N_MOD = 9
LOG2E = math.log2(math.e)

LANES = 128
TQ = 256
SUB = LANES
NSUB = TQ // SUB
SCORE_AHEAD = 3
PAD = BACK_B * CHUNK
PAD_A = BACK_A * CHUNK
WIN_A = SUB + PAD_A
WIN_B = SUB + PAD
NV_A = PAD_A // SUB + 1
NV_B = PAD // SUB + 1
DEN_ROWS = 16
ROLL_W = 1024
TM_FFN = 1024
FFN_SLAB = 512
W_ROWS = 256
TM_PROJ = 1024
PROJ_SLAB = 512
KV_LEAD = TM_PROJ
TN_MOD = 1024
FF_CHUNK = 256
VMEM_LIMIT = 56 * 1024 * 1024

_NT = (((1,), (1,)), ((), ()))
_TN = (((0,), (0,)), ((), ()))


def _rms(x, g):
    ms = jnp.mean(x * x, axis=-1, keepdims=True)
    return x * lax.rsqrt(ms + EPS) * g


def _const_spec(shape):
    nd = len(shape)
    return pl.BlockSpec(shape, lambda *_: (0,) * nd, pipeline_mode=pl.Buffered(1))


def _mod_kernel(c_ref, w_ref, b_ref, o_ref):
    c = c_ref[...]
    a = (c * jax.nn.sigmoid(c)).astype(BF16)
    o_ref[...] = jnp.dot(a, w_ref[...].astype(BF16), preferred_element_type=F32) + b_ref[...]


def _modulation(c, w_ada, b_ada):
    bsz, d = c.shape
    n = w_ada.shape[1]
    assert n % TN_MOD == 0
    return pl.pallas_call(
        _mod_kernel,
        grid=(n // TN_MOD,),
        in_specs=[pl.BlockSpec((bsz, d), lambda j: (0, 0)),
                  pl.BlockSpec((d, TN_MOD), lambda j: (0, j)),
                  pl.BlockSpec((1, TN_MOD), lambda j: (0, j))],
        out_specs=pl.BlockSpec((bsz, TN_MOD), lambda j: (0, j)),
        out_shape=jax.ShapeDtypeStruct((bsz, n), F32),
        compiler_params=pltpu.CompilerParams(dimension_semantics=("arbitrary",),
                                             vmem_limit_bytes=VMEM_LIMIT),
        name="adaln_mod",
    )(c, w_ada, b_ada.reshape(1, n))


def _load_bf16(src_hbm, dst_ref, stage, sem):
    ncol = src_hbm.shape[1]
    nblk = src_hbm.shape[0] // W_ROWS

    def copy(n):
        return pltpu.make_async_copy(src_hbm.at[pl.ds(n * W_ROWS, W_ROWS), :],
                                     stage.at[n % 2, :, pl.ds(0, ncol)], sem.at[n % 2])

    copy(0).start()
    for n in range(nblk):
        if n + 1 < nblk:
            copy(n + 1).start()
        copy(n).wait()
        dst_ref[n * W_ROWS:(n + 1) * W_ROWS, :] = stage[n % 2, :, 0:ncol].astype(dst_ref.dtype)


def _ffn_kernel(x_ref, mod_ref, gpre_ref, gpost_ref, wg_hbm, wu_hbm, wd_hbm, o_ref,
                wg_ref, wu_ref, wd_ref, stage, sem, *, mod0):
    @pl.when((pl.program_id(0) == 0) & (pl.program_id(1) == 0))
    def _():
        _load_bf16(wg_hbm, wg_ref, stage, sem)
        _load_bf16(wu_hbm, wu_ref, stage, sem)
        _load_bf16(wd_hbm, wd_ref, stage, sem)

    shift = mod_ref[0, pl.ds(mod0, 1), :]
    scale = mod_ref[0, pl.ds(mod0 + 1, 1), :]
    gate = mod_ref[0, pl.ds(mod0 + 2, 1), :]
    d_ff = wg_ref.shape[1]
    nslab = x_ref.shape[1] // FFN_SLAB
    nchunk = d_ff // FF_CHUNK

    def rows(s):
        return slice(s * FFN_SLAB, (s + 1) * FFN_SLAB)

    def pre(s):
        return (_rms(x_ref[0, rows(s), :], gpre_ref[...]) * (1.0 + scale) + shift).astype(BF16)

    def post(s, acc):
        o_ref[0, rows(s), :] = x_ref[0, rows(s), :] + (0.5 * gate) * _rms(acc, gpost_ref[...])

    h = pre(0)
    done = None
    for s in range(nslab):
        acc = jnp.zeros((FFN_SLAB, x_ref.shape[2]), F32)
        h_next = None
        for c in range(nchunk):
            if c == nchunk // 4 and done is not None:
                post(s - 1, done)
            if c == nchunk // 2 and s + 1 < nslab:
                h_next = pre(s + 1)
            f0 = c * FF_CHUNK
            g = jnp.dot(h, wg_ref[:, f0:f0 + FF_CHUNK], preferred_element_type=F32)
            u = jnp.dot(h, wu_ref[:, f0:f0 + FF_CHUNK], preferred_element_type=F32)
            a = (g * jax.nn.sigmoid(g) * u).astype(BF16)
            acc = acc + jnp.dot(a, wd_ref[f0:f0 + FF_CHUNK, :], preferred_element_type=F32)
        done, h = acc, h_next
    post(nslab - 1, done)


def _ffn(x, mod, g_pre, g_post, wg, wu, wd, *, mod0):
    bsz, seq, d = x.shape
    d_ff = wg.shape[1]
    assert seq % TM_FFN == 0 and TM_FFN % FFN_SLAB == 0 and d_ff % FF_CHUNK == 0
    assert d % W_ROWS == 0 and d_ff % W_ROWS == 0
    blk = lambda b, j: (b, j, 0)
    hbm = pl.BlockSpec(memory_space=pl.ANY)
    return pl.pallas_call(
        functools.partial(_ffn_kernel, mod0=mod0),
        grid=(bsz, seq // TM_FFN),
        in_specs=[pl.BlockSpec((1, TM_FFN, d), blk),
                  pl.BlockSpec((1, N_MOD, d), lambda b, j: (b, 0, 0)),
                  _const_spec((1, d)), _const_spec((1, d)),
                  hbm, hbm, hbm],
        out_specs=pl.BlockSpec((1, TM_FFN, d), blk),
        out_shape=jax.ShapeDtypeStruct((bsz, seq, d), F32),
        scratch_shapes=[pltpu.VMEM((d, d_ff), BF16), pltpu.VMEM((d, d_ff), BF16),
                        pltpu.VMEM((d_ff, d), BF16),
                        pltpu.VMEM((2, W_ROWS, max(d, d_ff)), F32),
                        pltpu.SemaphoreType.DMA((2,))],
        compiler_params=pltpu.CompilerParams(dimension_semantics=("arbitrary", "arbitrary"),
                                             vmem_limit_bytes=VMEM_LIMIT),
        name="ffn",
    )(x, mod, g_pre.reshape(1, d), g_post.reshape(1, d), wg, wu, wd)


def _proj_kernel(x_ref, mod_ref, gpre_ref, w_ref, b_ref, q_ref, k_ref, vt_ref):
    j = pl.program_id(1)
    nq, nk = q_ref.shape[2], k_ref.shape[2]

    @pl.when(j == 0)
    def _():
        k_ref[...] = jnp.zeros(k_ref.shape, k_ref.dtype)
        vt_ref[...] = jnp.zeros(vt_ref.shape, vt_ref.dtype)

    @pl.when(j > 0)
    def _():
        shift = mod_ref[0, pl.ds(3, 1), :]
        scale = mod_ref[0, pl.ds(4, 1), :]
        nslab = x_ref.shape[1] // PROJ_SLAB

        def rows(s):
            return slice(s * PROJ_SLAB, (s + 1) * PROJ_SLAB)

        def pre(s):
            return (_rms(x_ref[0, rows(s), :], gpre_ref[...]) * (1.0 + scale) + shift).astype(BF16)

        h = pre(0)
        for s in range(nslab):
            q = jnp.dot(h, w_ref[:, 0:nq], preferred_element_type=F32) + b_ref[:, 0:nq]
            q_ref[0, rows(s), :] = (q * (HEAD_DIM ** -0.5 * LOG2E)).astype(BF16)
            h_next = pre(s + 1) if s + 1 < nslab else None
            kv = jnp.dot(h, w_ref[:, nq:], preferred_element_type=F32) + b_ref[:, nq:]
            k_ref[0, rows(s), :] = kv[:, :nk].astype(BF16)
            vt_ref[0, :, rows(s)] = kv[:, nk:].T.astype(BF16)
            h = h_next


def _projection(x, mod, g_pre, w_all, b_all):
    bsz, seq, d = x.shape
    nq, nk, nv = QA + QB, KVA + QB, KVA + QB
    assert w_all.shape == (d, nq + nk + nv)
    assert KV_LEAD == TM_PROJ and seq % TM_PROJ == 0 and TM_PROJ % PROJ_SLAB == 0
    nblk = seq // TM_PROJ
    prev = lambda b, j: (b, jnp.maximum(j - 1, 0), 0)
    return pl.pallas_call(
        _proj_kernel,
        grid=(bsz, nblk + 1),
        in_specs=[pl.BlockSpec((1, TM_PROJ, d), prev),
                  pl.BlockSpec((1, N_MOD, d), lambda b, j: (b, 0, 0)),
                  _const_spec((1, d)),
                  _const_spec(w_all.shape), _const_spec(b_all.shape)],
        out_specs=[pl.BlockSpec((1, TM_PROJ, nq), prev),
                   pl.BlockSpec((1, TM_PROJ, nk), lambda b, j: (b, j, 0)),
                   pl.BlockSpec((1, nv, TM_PROJ), lambda b, j: (b, 0, j))],
        out_shape=[jax.ShapeDtypeStruct((bsz, seq, nq), BF16),
                   jax.ShapeDtypeStruct((bsz, seq + KV_LEAD, nk), BF16),
                   jax.ShapeDtypeStruct((bsz, nv, seq + KV_LEAD), BF16)],
        compiler_params=pltpu.CompilerParams(dimension_semantics=("arbitrary", "arbitrary"),
                                             vmem_limit_bytes=VMEM_LIMIT),
        name="qkv_proj",
    )(x, mod, g_pre.reshape(1, d), w_all, b_all)


def _bias_kernel(rel_ref, ba_ref, bb_ref):
    j = lax.broadcasted_iota(jnp.int32, (WIN_A, G_A * SUB), 0)
    r = lax.broadcasted_iota(jnp.int32, (WIN_A, G_A * SUB), 1)
    g = r // SUB
    i = r % SUB
    dist = jnp.abs(PAD_A + i - j).astype(F32)
    in_band = (j // CHUNK >= i // CHUNK) & (j // CHUNK <= i // CHUNK + BACK_A)
    for k in range(KV_A):
        slope = jnp.zeros(r.shape, F32)
        for gg in range(G_A):
            slope = jnp.where(g == gg, 2.0 ** (-8.0 * (k * G_A + gg + 1) / H_A), slope)
        for v in range(NV_A):
            valid = in_band & (j >= PAD_A - v * SUB)
            ba_ref[v, k] = jnp.where(valid, (-slope * dist) * LOG2E, NEG_INF)

    table = rel_ref[...]
    dcol = lax.broadcasted_iota(jnp.int32, (H_B, ROLL_W), 1)
    d = jnp.where(dcol < SUB, dcol, dcol - ROLL_W)
    idx = jnp.clip(PAD + d, -REL_CLIP, REL_CLIP) + REL_CLIP
    row = jnp.zeros((H_B, ROLL_W), F32)
    for n in range(N_REL):
        row = jnp.where(idx == n, table[:, n:n + 1], row)
    jb = lax.broadcasted_iota(jnp.int32, (WIN_B, SUB), 0)
    ib = lax.broadcasted_iota(jnp.int32, (WIN_B, SUB), 1)
    in_band_b = (jb // CHUNK >= ib // CHUNK) & (jb // CHUNK <= ib // CHUNK + BACK_B)
    for h in range(H_B):
        tile = jnp.broadcast_to(row[h:h + 1, :], (WIN_B, ROLL_W))
        tile = pltpu.roll(tile, 0, 1, stride=1, stride_axis=0)[:, :SUB]
        for v in range(NV_B):
            valid = in_band_b & (jb >= PAD - v * SUB)
            bb_ref[v, h // 2, :, pl.ds((h % 2) * SUB, SUB)] = jnp.where(valid, tile * LOG2E, NEG_INF)


def _bias_tiles(rel_bias):
    return pl.pallas_call(
        _bias_kernel,
        out_shape=[jax.ShapeDtypeStruct((NV_A, KV_A, WIN_A, G_A * SUB), F32),
                   jax.ShapeDtypeStruct((NV_B, H_B // 2, WIN_B, 2 * SUB), F32)],
        compiler_params=pltpu.CompilerParams(vmem_limit_bytes=VMEM_LIMIT),
        name="score_bias",
    )(rel_bias)


def _rms_rows(xt, g_col):
    ms = jnp.mean(xt * xt, axis=0, keepdims=True)
    return xt * lax.rsqrt(ms + EPS) * g_col


def _mixer_kernel(sink_ref, q_ref, k_ref, vt_ref, ba_ref, bb_ref, gga_ref, ggb_ref, wo_ref, bo_ref,
                  gpost_ref, mod_ref, x_ref, o_ref):
    i = pl.program_id(1)
    ws_b = [pl.multiple_of(i * TQ + u * SUB + KV_LEAD - PAD, SUB) for u in range(NSUB)]
    ws_a = [pl.multiple_of(i * TQ + u * SUB + KV_LEAD - PAD_A, LANES) for u in range(NSUB)]
    var_b = [jnp.minimum(i * NSUB + u, NV_B - 1) for u in range(NSUB)]
    var_a = [jnp.minimum(i * NSUB + u, NV_A - 1) for u in range(NSUB)]

    low = lax.broadcasted_iota(jnp.int32, (1, LANES), 1) < HEAD_DIM
    halves = (low, jnp.logical_not(low))
    col_g = lax.broadcasted_iota(jnp.int32, (1, G_A * SUB), 1) // SUB
    ones_a = jnp.ones((DEN_ROWS, WIN_A), BF16)
    ones_b = jnp.ones((DEN_ROWS, WIN_B), BF16)

    def scores_a(u, k):
        k_a = k_ref[0, pl.ds(ws_a[u], WIN_A), 0:LANES]
        qs = jnp.concatenate(
            [jnp.where(halves[k], q_ref[0, u * SUB:(u + 1) * SUB, g * LANES:(g + 1) * LANES], 0)
             for g in range(G_A)], axis=0)
        s = lax.dot_general(k_a, qs, _NT, preferred_element_type=F32)
        return s + ba_ref[var_a[u], k]

    def attend_a(u, k, s):
        sink = jnp.zeros((1, G_A * SUB), F32)
        for g in range(G_A):
            sink = jnp.where(col_g == g, sink_ref[k * G_A + g] * LOG2E, sink)
        m = jnp.maximum(jnp.max(s, axis=0, keepdims=True), sink)
        e = jnp.exp2(s - m).astype(BF16)
        vt_a = jnp.concatenate([vt_ref[0, 0:LANES, pl.ds(ws_a[u], WIN_A)], ones_a], axis=0)
        o = jnp.dot(vt_a, e, preferred_element_type=F32)
        den = o[LANES:LANES + 1] + jnp.exp2(sink - m)
        o = o[k * HEAD_DIM:(k + 1) * HEAD_DIM] * (1.0 / den)
        return [o[:, g * SUB:(g + 1) * SUB] for g in range(G_A)]

    def scores_b(u, t):
        q_t = q_ref[0, u * SUB:(u + 1) * SUB, QA + t * LANES:QA + (t + 1) * LANES]
        k_t = k_ref[0, pl.ds(ws_b[u], WIN_B), (1 + t) * LANES:(2 + t) * LANES]
        q2 = jnp.concatenate([jnp.where(halves[0], q_t, 0), jnp.where(halves[1], q_t, 0)], axis=0)
        s = lax.dot_general(k_t, q2, _NT, preferred_element_type=F32)
        return s + bb_ref[var_b[u], t]

    def attend_b(u, t, s):
        m = jnp.max(s, axis=0, keepdims=True)
        e = jnp.exp2(s - m).astype(BF16)
        vt_t = jnp.concatenate([vt_ref[0, (1 + t) * LANES:(2 + t) * LANES, pl.ds(ws_b[u], WIN_B)],
                                ones_b], axis=0)
        o = jnp.dot(vt_t, e, preferred_element_type=F32)
        rden = 1.0 / o[LANES:LANES + 1]
        return [o[:HEAD_DIM, :SUB] * rden[:, :SUB], o[HEAD_DIM:LANES, SUB:] * rden[:, SUB:]]

    groups = []
    for u in range(NSUB):
        groups += [(scores_a, attend_a, u, k) for k in range(KV_A)]
        groups += [(scores_b, attend_b, u, t) for t in range(H_B // 2)]
    outs = []
    pending = [grp[0](*grp[2:]) for grp in groups[:SCORE_AHEAD]]
    for n, (_, attend, u, idx) in enumerate(groups):
        if n + SCORE_AHEAD < len(groups):
            nxt = groups[n + SCORE_AHEAD]
            pending.append(nxt[0](*nxt[2:]))
        outs.append(attend(u, idx, pending.pop(0)))
    per_sub = len(groups) // NSUB
    oa_t, ob_t = [], []
    for u in range(NSUB):
        sub = outs[u * per_sub:(u + 1) * per_sub]
        oa_t.append(jnp.concatenate([sub[k][g] for g in range(G_A) for k in range(KV_A)], axis=0))
        ob_t.append(jnp.concatenate([part for pair in sub[KV_A:] for part in pair], axis=0))
    oa_t = jnp.concatenate(oa_t, axis=1)
    ob_t = jnp.concatenate(ob_t, axis=1)

    y_t = jnp.concatenate([_rms_rows(oa_t, gga_ref[...]), _rms_rows(ob_t, ggb_ref[...])], axis=0)
    z = lax.dot_general(y_t.astype(BF16), wo_ref[...], _TN, preferred_element_type=F32) + bo_ref[...]
    gate = mod_ref[0, pl.ds(5, 1), :]
    o_ref[0] = x_ref[0] + gate * _rms(z, gpost_ref[...])


def _mixer(x, mod, q, k, vt, bias_a, bias_b, sinks, gga, ggb, wo, bo, g_post):
    bsz, seq, d = x.shape
    assert seq % TQ == 0 and TQ % SUB == 0
    tile = lambda b, i: (b, i, 0)
    whole = lambda b, i: (b, 0, 0)
    return pl.pallas_call(
        _mixer_kernel,
        grid=(bsz, seq // TQ),
        in_specs=[pl.BlockSpec(memory_space=pltpu.SMEM),
                  pl.BlockSpec((1, TQ, q.shape[2]), tile),
                  pl.BlockSpec((1,) + k.shape[1:], whole),
                  pl.BlockSpec((1,) + vt.shape[1:], whole),
                  _const_spec(bias_a.shape), _const_spec(bias_b.shape),
                  _const_spec((QA, 1)), _const_spec((QB, 1)),
                  _const_spec(wo.shape), _const_spec((1, d)), _const_spec((1, d)),
                  pl.BlockSpec((1, N_MOD, d), whole),
                  pl.BlockSpec((1, TQ, d), tile)],
        out_specs=pl.BlockSpec((1, TQ, d), tile),
        out_shape=jax.ShapeDtypeStruct((bsz, seq, d), F32),
        compiler_params=pltpu.CompilerParams(dimension_semantics=("arbitrary", "arbitrary"),
                                             vmem_limit_bytes=VMEM_LIMIT),
        name="mixer",
    )(sinks, q, k, vt, bias_a, bias_b, gga.reshape(QA, 1), ggb.reshape(QB, 1), wo,
      bo.reshape(1, d), g_post.reshape(1, d), mod, x)


def _pair_heads_a(w, axis):
    shape = w.shape
    split = shape[:axis] + (KV_A, G_A, HEAD_DIM) + shape[axis + 1:]
    return jnp.swapaxes(w.reshape(split), axis, axis + 1).reshape(shape)


def kernel(x, c, w_ada, b_ada, g_pre_ffn1, w_gate1, w_up1, w_down1, g_post_ffn1, g_pre_mix, w_in, b_in, sinks_a, rel_bias_b, g_grp_a, g_grp_b, w_out, b_out, g_post_mix, g_pre_ffn2, w_gate2, w_up2, w_down2, g_post_ffn2):
    bsz, seq, d = x.shape
    depth = w_ada.shape[0]
    for l in range(depth):
        mod = _modulation(c, w_ada[l], b_ada[l]).reshape(bsz, N_MOD, d)

        x = _ffn(x, mod, g_pre_ffn1[l], g_post_ffn1[l], w_gate1[l], w_up1[l], w_down1[l], mod0=0)

        wi, bi = w_in[l], b_in[l]
        o_ka, o_va, o_qb, o_kb, o_vb = QA, QA + KVA, QA + 2 * KVA, QA + 2 * KVA + QB, QA + 2 * KVA + 2 * QB
        cols = lambda a: [_pair_heads_a(a[..., :QA], a.ndim - 1), a[..., o_qb:o_kb],
                          a[..., o_ka:o_va], a[..., o_kb:o_vb], a[..., o_va:o_qb], a[..., o_vb:]]
        w_all = jnp.concatenate(cols(wi), axis=1).astype(BF16)
        b_all = jnp.concatenate(cols(bi)).reshape(1, -1)
        q, k, vt = _projection(x, mod, g_pre_mix[l], w_all, b_all)

        bias_a, bias_b = _bias_tiles(rel_bias_b[l])
        wo = jnp.concatenate([_pair_heads_a(w_out[l][:QA], 0), w_out[l][QA:]], axis=0).astype(BF16)
        x = _mixer(x, mod, q, k, vt, bias_a, bias_b, sinks_a[l], _pair_heads_a(g_grp_a[l], 0),
                   g_grp_b[l], wo, b_out[l], g_post_mix[l])

        x = _ffn(x, mod, g_pre_ffn2[l], g_post_ffn2[l], w_gate2[l], w_up2[l], w_down2[l], mod0=6)
    return x
```

```python
import functools
import math

import jax
import jax.numpy as jnp
from jax import lax
from jax.experimental import pallas as pl
from jax.experimental.pallas import tpu as pltpu

F32 = jnp.float32
BF16 = jnp.bfloat16

CHUNK = 64
HEAD_DIM = 64
EPS = 1e-6
NEG_INF = -1e30
H_A, KV_A = 8, 2
G_A = H_A // KV_A
BACK_A = 2
H_B = 8
BACK_B = 8
REL_CLIP = 128
N_REL = 2 * REL_CLIP + 1
QA = H_A * HEAD_DIM
KVA = KV_A * HEAD_DIM
QB = H_B * HEAD_DIM
N_MOD = 9
LOG2E = math.log2(math.e)

LANES = 128
TQ = 512
SUB = LANES
NSUB = TQ // SUB
SCORE_AHEAD = 3
FIN_SUBS = 2
FIN_LAG = 3
PAD = BACK_B * CHUNK
PAD_A = BACK_A * CHUNK
WIN_A = SUB + PAD_A
WIN_B = SUB + PAD
NV_A = PAD_A // SUB + 1
NV_B = PAD // SUB + 1
DEN_ROWS = 16
ROLL_W = 1024
TM_FFN = 1024
FFN_SLAB = 512
W_ROWS = 256
TM_PROJ = 1024
PROJ_SLAB = 512
KV_LEAD = TM_PROJ
TN_MOD = 1024
FF_CHUNK = 256
VMEM_LIMIT = 56 * 1024 * 1024

_NT = (((1,), (1,)), ((), ()))
_TN = (((0,), (0,)), ((), ()))


def _rms(x, g):
    ms = jnp.mean(x * x, axis=-1, keepdims=True)
    return x * lax.rsqrt(ms + EPS) * g


def _const_spec(shape):
    nd = len(shape)
    return pl.BlockSpec(shape, lambda *_: (0,) * nd, pipeline_mode=pl.Buffered(1))


def _mod_kernel(c_ref, w_ref, b_ref, o_ref):
    c = c_ref[...]
    a = (c * jax.nn.sigmoid(c)).astype(BF16)
    o_ref[...] = jnp.dot(a, w_ref[...].astype(BF16), preferred_element_type=F32) + b_ref[...]


def _modulation(c, w_ada, b_ada):
    bsz, d = c.shape
    n = w_ada.shape[1]
    assert n % TN_MOD == 0
    return pl.pallas_call(
        _mod_kernel,
        grid=(n // TN_MOD,),
        in_specs=[pl.BlockSpec((bsz, d), lambda j: (0, 0)),
                  pl.BlockSpec((d, TN_MOD), lambda j: (0, j)),
                  pl.BlockSpec((1, TN_MOD), lambda j: (0, j))],
        out_specs=pl.BlockSpec((bsz, TN_MOD), lambda j: (0, j)),
        out_shape=jax.ShapeDtypeStruct((bsz, n), F32),
        compiler_params=pltpu.CompilerParams(dimension_semantics=("arbitrary",),
                                             vmem_limit_bytes=VMEM_LIMIT),
        name="adaln_mod",
    )(c, w_ada, b_ada.reshape(1, n))


def _load_bf16(src_hbm, dst_ref, stage, sem):
    ncol = src_hbm.shape[1]
    nblk = src_hbm.shape[0] // W_ROWS

    def copy(n):
        return pltpu.make_async_copy(src_hbm.at[pl.ds(n * W_ROWS, W_ROWS), :],
                                     stage.at[n % 2, :, pl.ds(0, ncol)], sem.at[n % 2])

    copy(0).start()
    for n in range(nblk):
        if n + 1 < nblk:
            copy(n + 1).start()
        copy(n).wait()
        dst_ref[n * W_ROWS:(n + 1) * W_ROWS, :] = stage[n % 2, :, 0:ncol].astype(dst_ref.dtype)


def _ffn_kernel(x_ref, mod_ref, gpre_ref, gpost_ref, wg_hbm, wu_hbm, wd_hbm, o_ref,
                wg_ref, wu_ref, wd_ref, stage, sem, *, mod0):
    @pl.when((pl.program_id(0) == 0) & (pl.program_id(1) == 0))
    def _():
        _load_bf16(wg_hbm, wg_ref, stage, sem)
        _load_bf16(wu_hbm, wu_ref, stage, sem)
        _load_bf16(wd_hbm, wd_ref, stage, sem)

    shift = mod_ref[0, pl.ds(mod0, 1), :]
    scale = mod_ref[0, pl.ds(mod0 + 1, 1), :]
    gate = mod_ref[0, pl.ds(mod0 + 2, 1), :]
    d_ff = wg_ref.shape[1]
    nslab = x_ref.shape[1] // FFN_SLAB
    nchunk = d_ff // FF_CHUNK

    def rows(s):
        return slice(s * FFN_SLAB, (s + 1) * FFN_SLAB)

    def pre(s):
        return (_rms(x_ref[0, rows(s), :], gpre_ref[...]) * (1.0 + scale) + shift).astype(BF16)

    def post(s, acc):
        o_ref[0, rows(s), :] = x_ref[0, rows(s), :] + (0.5 * gate) * _rms(acc, gpost_ref[...])

    h = pre(0)
    done = None
    for s in range(nslab):
        acc = jnp.zeros((FFN_SLAB, x_ref.shape[2]), F32)
        h_next = None
        for c in range(nchunk):
            if c == nchunk // 4 and done is not None:
                post(s - 1, done)
            if c == nchunk // 2 and s + 1 < nslab:
                h_next = pre(s + 1)
            f0 = c * FF_CHUNK
            g = jnp.dot(h, wg_ref[:, f0:f0 + FF_CHUNK], preferred_element_type=F32)
            u = jnp.dot(h, wu_ref[:, f0:f0 + FF_CHUNK], preferred_element_type=F32)
            a = (g * jax.nn.sigmoid(g) * u).astype(BF16)
            acc = acc + jnp.dot(a, wd_ref[f0:f0 + FF_CHUNK, :], preferred_element_type=F32)
        done, h = acc, h_next
    post(nslab - 1, done)


def _ffn(x, mod, g_pre, g_post, wg, wu, wd, *, mod0):
    bsz, seq, d = x.shape
    d_ff = wg.shape[1]
    assert seq % TM_FFN == 0 and TM_FFN % FFN_SLAB == 0 and d_ff % FF_CHUNK == 0
    assert d % W_ROWS == 0 and d_ff % W_ROWS == 0
    blk = lambda b, j: (b, j, 0)
    hbm = pl.BlockSpec(memory_space=pl.ANY)
    return pl.pallas_call(
        functools.partial(_ffn_kernel, mod0=mod0),
        grid=(bsz, seq // TM_FFN),
        in_specs=[pl.BlockSpec((1, TM_FFN, d), blk),
                  pl.BlockSpec((1, N_MOD, d), lambda b, j: (b, 0, 0)),
                  _const_spec((1, d)), _const_spec((1, d)),
                  hbm, hbm, hbm],
        out_specs=pl.BlockSpec((1, TM_FFN, d), blk),
        out_shape=jax.ShapeDtypeStruct((bsz, seq, d), F32),
        scratch_shapes=[pltpu.VMEM((d, d_ff), BF16), pltpu.VMEM((d, d_ff), BF16),
                        pltpu.VMEM((d_ff, d), BF16),
                        pltpu.VMEM((2, W_ROWS, max(d, d_ff)), F32),
                        pltpu.SemaphoreType.DMA((2,))],
        compiler_params=pltpu.CompilerParams(dimension_semantics=("arbitrary", "arbitrary"),
                                             vmem_limit_bytes=VMEM_LIMIT),
        name="ffn",
    )(x, mod, g_pre.reshape(1, d), g_post.reshape(1, d), wg, wu, wd)


def _proj_kernel(x_ref, mod_ref, gpre_ref, w_ref, b_ref, q_ref, k_ref, vt_ref):
    j = pl.program_id(1)
    nq, nk = q_ref.shape[2], k_ref.shape[2]

    @pl.when(j == 0)
    def _():
        k_ref[...] = jnp.zeros(k_ref.shape, k_ref.dtype)
        vt_ref[...] = jnp.zeros(vt_ref.shape, vt_ref.dtype)

    @pl.when(j > 0)
    def _():
        shift = mod_ref[0, pl.ds(3, 1), :]
        scale = mod_ref[0, pl.ds(4, 1), :]
        nslab = x_ref.shape[1] // PROJ_SLAB

        def rows(s):
            return slice(s * PROJ_SLAB, (s + 1) * PROJ_SLAB)

        def pre(s):
            return (_rms(x_ref[0, rows(s), :], gpre_ref[...]) * (1.0 + scale) + shift).astype(BF16)

        h = pre(0)
        for s in range(nslab):
            q = jnp.dot(h, w_ref[:, 0:nq], preferred_element_type=F32) + b_ref[:, 0:nq]
            q_ref[0, rows(s), :] = (q * (HEAD_DIM ** -0.5 * LOG2E)).astype(BF16)
            h_next = pre(s + 1) if s + 1 < nslab else None
            kv = jnp.dot(h, w_ref[:, nq:], preferred_element_type=F32) + b_ref[:, nq:]
            k_ref[0, rows(s), :] = kv[:, :nk].astype(BF16)
            vt_ref[0, :, rows(s)] = kv[:, nk:].T.astype(BF16)
            h = h_next


def _projection(x, mod, g_pre, w_all, b_all):
    bsz, seq, d = x.shape
    nq, nk, nv = QA + QB, KVA + QB, KVA + QB
    assert w_all.shape == (d, nq + nk + nv)
    assert KV_LEAD == TM_PROJ and seq % TM_PROJ == 0 and TM_PROJ % PROJ_SLAB == 0
    nblk = seq // TM_PROJ
    prev = lambda b, j: (b, jnp.maximum(j - 1, 0), 0)
    return pl.pallas_call(
        _proj_kernel,
        grid=(bsz, nblk + 1),
        in_specs=[pl.BlockSpec((1, TM_PROJ, d), prev),
                  pl.BlockSpec((1, N_MOD, d), lambda b, j: (b, 0, 0)),
                  _const_spec((1, d)),
                  _const_spec(w_all.shape), _const_spec(b_all.shape)],
        out_specs=[pl.BlockSpec((1, TM_PROJ, nq), prev),
                   pl.BlockSpec((1, TM_PROJ, nk), lambda b, j: (b, j, 0)),
                   pl.BlockSpec((1, nv, TM_PROJ), lambda b, j: (b, 0, j))],
        out_shape=[jax.ShapeDtypeStruct((bsz, seq, nq), BF16),
                   jax.ShapeDtypeStruct((bsz, seq + KV_LEAD, nk), BF16),
                   jax.ShapeDtypeStruct((bsz, nv, seq + KV_LEAD), BF16)],
        compiler_params=pltpu.CompilerParams(dimension_semantics=("arbitrary", "arbitrary"),
                                             vmem_limit_bytes=VMEM_LIMIT),
        name="qkv_proj",
    )(x, mod, g_pre.reshape(1, d), w_all, b_all)


def _bias_kernel(rel_ref, ba_ref, bb_ref):
    j = lax.broadcasted_iota(jnp.int32, (WIN_A, G_A * SUB), 0)
    r = lax.broadcasted_iota(jnp.int32, (WIN_A, G_A * SUB), 1)
    g = r // SUB
    i = r % SUB
    dist = jnp.abs(PAD_A + i - j).astype(F32)
    in_band = (j // CHUNK >= i // CHUNK) & (j // CHUNK <= i // CHUNK + BACK_A)
    for k in range(KV_A):
        slope = jnp.zeros(r.shape, F32)
        for gg in range(G_A):
            slope = jnp.where(g == gg, 2.0 ** (-8.0 * (k * G_A + gg + 1) / H_A), slope)
        for v in range(NV_A):
            valid = in_band & (j >= PAD_A - v * SUB)
            ba_ref[v, k] = jnp.where(valid, (-slope * dist) * LOG2E, NEG_INF)

    table = rel_ref[...]
    dcol = lax.broadcasted_iota(jnp.int32, (H_B, ROLL_W), 1)
    d = jnp.where(dcol < SUB, dcol, dcol - ROLL_W)
    idx = jnp.clip(PAD + d, -REL_CLIP, REL_CLIP) + REL_CLIP
    row = jnp.zeros((H_B, ROLL_W), F32)
    for n in range(N_REL):
        row = jnp.where(idx == n, table[:, n:n + 1], row)
    jb = lax.broadcasted_iota(jnp.int32, (WIN_B, SUB), 0)
    ib = lax.broadcasted_iota(jnp.int32, (WIN_B, SUB), 1)
    in_band_b = (jb // CHUNK >= ib // CHUNK) & (jb // CHUNK <= ib // CHUNK + BACK_B)
    for h in range(H_B):
        tile = jnp.broadcast_to(row[h:h + 1, :], (WIN_B, ROLL_W))
        tile = pltpu.roll(tile, 0, 1, stride=1, stride_axis=0)[:, :SUB]
        for v in range(NV_B):
            valid = in_band_b & (jb >= PAD - v * SUB)
            bb_ref[v, h // 2, :, pl.ds((h % 2) * SUB, SUB)] = jnp.where(valid, tile * LOG2E, NEG_INF)


def _bias_tiles(rel_bias):
    return pl.pallas_call(
        _bias_kernel,
        out_shape=[jax.ShapeDtypeStruct((NV_A, KV_A, WIN_A, G_A * SUB), F32),
                   jax.ShapeDtypeStruct((NV_B, H_B // 2, WIN_B, 2 * SUB), F32)],
        compiler_params=pltpu.CompilerParams(vmem_limit_bytes=VMEM_LIMIT),
        name="score_bias",
    )(rel_bias)


def _rms_rows(xt, g_col):
    ms = jnp.mean(xt * xt, axis=0, keepdims=True)
    return xt * lax.rsqrt(ms + EPS) * g_col


def _mixer_kernel(sink_ref, q_ref, k_ref, vt_ref, ba_ref, bb_ref, gga_ref, ggb_ref, wo_ref, bo_ref,
                  gpost_ref, mod_ref, x_ref, o_ref):
    i = pl.program_id(1)
    ws_b = [pl.multiple_of(i * TQ + u * SUB + KV_LEAD - PAD, SUB) for u in range(NSUB)]
    ws_a = [pl.multiple_of(i * TQ + u * SUB + KV_LEAD - PAD_A, LANES) for u in range(NSUB)]
    var_b = [jnp.minimum(i * NSUB + u, NV_B - 1) for u in range(NSUB)]
    var_a = [jnp.minimum(i * NSUB + u, NV_A - 1) for u in range(NSUB)]

    low = lax.broadcasted_iota(jnp.int32, (1, LANES), 1) < HEAD_DIM
    halves = (low, jnp.logical_not(low))
    col_g = lax.broadcasted_iota(jnp.int32, (1, G_A * SUB), 1) // SUB
    ones_a = jnp.ones((DEN_ROWS, WIN_A), BF16)
    ones_b = jnp.ones((DEN_ROWS, WIN_B), BF16)

    def scores_a(u, k):
        k_a = k_ref[0, pl.ds(ws_a[u], WIN_A), 0:LANES]
        qs = jnp.concatenate(
            [jnp.where(halves[k], q_ref[0, u * SUB:(u + 1) * SUB, g * LANES:(g + 1) * LANES], 0)
             for g in range(G_A)], axis=0)
        s = lax.dot_general(k_a, qs, _NT, preferred_element_type=F32)
        return s + ba_ref[var_a[u], k]

    def attend_a(u, k, s):
        sink = jnp.zeros((1, G_A * SUB), F32)
        for g in range(G_A):
            sink = jnp.where(col_g == g, sink_ref[k * G_A + g] * LOG2E, sink)
        m = jnp.maximum(jnp.max(s, axis=0, keepdims=True), sink)
        e = jnp.exp2(s - m).astype(BF16)
        vt_a = jnp.concatenate([vt_ref[0, 0:LANES, pl.ds(ws_a[u], WIN_A)], ones_a], axis=0)
        o = jnp.dot(vt_a, e, preferred_element_type=F32)
        den = o[LANES:LANES + 1] + jnp.exp2(sink - m)
        o = o[k * HEAD_DIM:(k + 1) * HEAD_DIM] * (1.0 / den)
        return [o[:, g * SUB:(g + 1) * SUB] for g in range(G_A)]

    def scores_b(u, t):
        q_t = q_ref[0, u * SUB:(u + 1) * SUB, QA + t * LANES:QA + (t + 1) * LANES]
        k_t = k_ref[0, pl.ds(ws_b[u], WIN_B), (1 + t) * LANES:(2 + t) * LANES]
        q2 = jnp.concatenate([jnp.where(halves[0], q_t, 0), jnp.where(halves[1], q_t, 0)], axis=0)
        s = lax.dot_general(k_t, q2, _NT, preferred_element_type=F32)
        return s + bb_ref[var_b[u], t]

    def attend_b(u, t, s):
        m = jnp.max(s, axis=0, keepdims=True)
        e = jnp.exp2(s - m).astype(BF16)
        vt_t = jnp.concatenate([vt_ref[0, (1 + t) * LANES:(2 + t) * LANES, pl.ds(ws_b[u], WIN_B)],
                                ones_b], axis=0)
        o = jnp.dot(vt_t, e, preferred_element_type=F32)
        rden = 1.0 / o[LANES:LANES + 1]
        return [o[:HEAD_DIM, :SUB] * rden[:, :SUB], o[HEAD_DIM:LANES, SUB:] * rden[:, SUB:]]

    groups = []
    for u in range(NSUB):
        groups += [(scores_a, attend_a, u, k) for k in range(KV_A)]
        groups += [(scores_b, attend_b, u, t) for t in range(H_B // 2)]
    per_sub = len(groups) // NSUB
    outs = []

    def finish(first_sub):
        oa_t, ob_t = [], []
        for u in range(first_sub, first_sub + FIN_SUBS):
            sub = outs[u * per_sub:(u + 1) * per_sub]
            oa_t.append(jnp.concatenate([sub[k][g] for g in range(G_A) for k in range(KV_A)], axis=0))
            ob_t.append(jnp.concatenate([part for pair in sub[KV_A:] for part in pair], axis=0))
        oa_t = jnp.concatenate(oa_t, axis=1)
        ob_t = jnp.concatenate(ob_t, axis=1)
        y_t = jnp.concatenate([_rms_rows(oa_t, gga_ref[...]), _rms_rows(ob_t, ggb_ref[...])], axis=0)
        z = lax.dot_general(y_t.astype(BF16), wo_ref[...], _TN, preferred_element_type=F32) + bo_ref[...]
        gate = mod_ref[0, pl.ds(5, 1), :]
        r = slice(first_sub * SUB, (first_sub + FIN_SUBS) * SUB)
        o_ref[0, r, :] = x_ref[0, r, :] + gate * _rms(z, gpost_ref[...])

    pending = [grp[0](*grp[2:]) for grp in groups[:SCORE_AHEAD]]
    for n, (_, attend, u, idx) in enumerate(groups):
        if n + SCORE_AHEAD < len(groups):
            nxt = groups[n + SCORE_AHEAD]
            pending.append(nxt[0](*nxt[2:]))
        outs.append(attend(u, idx, pending.pop(0)))
        done_groups = n + 1 - FIN_LAG
        if 0 < done_groups < len(groups) and done_groups % (FIN_SUBS * per_sub) == 0:
            finish(done_groups // per_sub - FIN_SUBS)
    finish(NSUB - FIN_SUBS)


def _mixer(x, mod, q, k, vt, bias_a, bias_b, sinks, gga, ggb, wo, bo, g_post):
    bsz, seq, d = x.shape
    assert seq % TQ == 0 and TQ % SUB == 0
    tile = lambda b, i: (b, i, 0)
    whole = lambda b, i: (b, 0, 0)
    return pl.pallas_call(
        _mixer_kernel,
        grid=(bsz, seq // TQ),
        in_specs=[pl.BlockSpec(memory_space=pltpu.SMEM),
                  pl.BlockSpec((1, TQ, q.shape[2]), tile),
                  pl.BlockSpec((1,) + k.shape[1:], whole),
                  pl.BlockSpec((1,) + vt.shape[1:], whole),
                  _const_spec(bias_a.shape), _const_spec(bias_b.shape),
                  _const_spec((QA, 1)), _const_spec((QB, 1)),
                  _const_spec(wo.shape), _const_spec((1, d)), _const_spec((1, d)),
                  pl.BlockSpec((1, N_MOD, d), whole),
                  pl.BlockSpec((1, TQ, d), tile)],
        out_specs=pl.BlockSpec((1, TQ, d), tile),
        out_shape=jax.ShapeDtypeStruct((bsz, seq, d), F32),
        compiler_params=pltpu.CompilerParams(dimension_semantics=("arbitrary", "arbitrary"),
                                             vmem_limit_bytes=VMEM_LIMIT),
        name="mixer",
    )(sinks, q, k, vt, bias_a, bias_b, gga.reshape(QA, 1), ggb.reshape(QB, 1), wo,
      bo.reshape(1, d), g_post.reshape(1, d), mod, x)


def _pair_heads_a(w, axis):
    shape = w.shape
    split = shape[:axis] + (KV_A, G_A, HEAD_DIM) + shape[axis + 1:]
    return jnp.swapaxes(w.reshape(split), axis, axis + 1).reshape(shape)


def kernel(x, c, w_ada, b_ada, g_pre_ffn1, w_gate1, w_up1, w_down1, g_post_ffn1, g_pre_mix, w_in, b_in, sinks_a, rel_bias_b, g_grp_a, g_grp_b, w_out, b_out, g_post_mix, g_pre_ffn2, w_gate2, w_up2, w_down2, g_post_ffn2):
    bsz, seq, d = x.shape
    depth = w_ada.shape[0]
    for l in range(depth):
        mod = _modulation(c, w_ada[l], b_ada[l]).reshape(bsz, N_MOD, d)

        x = _ffn(x, mod, g_pre_ffn1[l], g_post_ffn1[l], w_gate1[l], w_up1[l], w_down1[l], mod0=0)

        wi, bi = w_in[l], b_in[l]
        o_ka, o_va, o_qb, o_kb, o_vb = QA, QA + KVA, QA + 2 * KVA, QA + 2 * KVA + QB, QA + 2 * KVA + 2 * QB
        cols = lambda a: [_pair_heads_a(a[..., :QA], a.ndim - 1), a[..., o_qb:o_kb],
                          a[..., o_ka:o_va], a[..., o_kb:o_vb], a[..., o_va:o_qb], a[..., o_vb:]]
        w_all = jnp.concatenate(cols(wi), axis=1).astype(BF16)
        b_all = jnp.concatenate(cols(bi)).reshape(1, -1)
        q, k, vt = _projection(x, mod, g_pre_mix[l], w_all, b_all)

        bias_a, bias_b = _bias_tiles(rel_bias_b[l])
        wo = jnp.concatenate([_pair_heads_a(w_out[l][:QA], 0), w_out[l][QA:]], axis=0).astype(BF16)
        x = _mixer(x, mod, q, k, vt, bias_a, bias_b, sinks_a[l], _pair_heads_a(g_grp_a[l], 0),
                   g_grp_b[l], wo, b_out[l], g_post_mix[l])

        x = _ffn(x, mod, g_pre_ffn2[l], g_post_ffn2[l], w_gate2[l], w_up2[l], w_down2[l], mod0=6)
    return x
```

```python
import functools
import math

import jax
import jax.numpy as jnp
from jax import lax
from jax.experimental import pallas as pl
from jax.experimental.pallas import tpu as pltpu

F32 = jnp.float32
BF16 = jnp.bfloat16

CHUNK = 64
HEAD_DIM = 64
EPS = 1e-6
NEG_INF = -1e30
H_A, KV_A = 8, 2
G_A = H_A // KV_A
BACK_A = 2
H_B = 8
BACK_B = 8
REL_CLIP = 128
N_REL = 2 * REL_CLIP + 1
QA = H_A * HEAD_DIM
KVA = KV_A * HEAD_DIM
QB = H_B * HEAD_DIM
N_MOD = 9
LOG2E = math.log2(math.e)

LANES = 128
TQ = 512
SUB = LANES
NSUB = TQ // SUB
SCORE_AHEAD = 3
FIN_SUBS = 2
FIN_LAG = 3
PAD = BACK_B * CHUNK
PAD_A = BACK_A * CHUNK
WIN_A = SUB + PAD_A
WIN_B = SUB + PAD
NV_A = PAD_A // SUB + 1
NV_B = PAD // SUB + 1
DEN_ROWS = 16
ROLL_W = 1024
TM_FFN = 1024
FFN_SLAB = 512
W_ROWS = 256
TM_PROJ = 1024
PROJ_SLAB = 512
KV_LEAD = TM_PROJ
TN_MOD = 1024
FF_CHUNK = 256
VMEM_LIMIT = 56 * 1024 * 1024

_NT = (((1,), (1,)), ((), ()))
_TN = (((0,), (0,)), ((), ()))


def _rms(x, g):
    ms = jnp.mean(x * x, axis=-1, keepdims=True)
    return x * lax.rsqrt(ms + EPS) * g


def _const_spec(shape):
    nd = len(shape)
    return pl.BlockSpec(shape, lambda *_: (0,) * nd, pipeline_mode=pl.Buffered(1))


def _mod_kernel(c_ref, w_ref, b_ref, o_ref):
    c = c_ref[...]
    a = (c * jax.nn.sigmoid(c)).astype(BF16)
    o_ref[...] = jnp.dot(a, w_ref[...].astype(BF16), preferred_element_type=F32) + b_ref[...]


def _modulation(c, w_ada, b_ada):
    bsz, d = c.shape
    n = w_ada.shape[1]
    assert n % TN_MOD == 0
    return pl.pallas_call(
        _mod_kernel,
        grid=(n // TN_MOD,),
        in_specs=[pl.BlockSpec((bsz, d), lambda j: (0, 0)),
                  pl.BlockSpec((d, TN_MOD), lambda j: (0, j)),
                  pl.BlockSpec((1, TN_MOD), lambda j: (0, j))],
        out_specs=pl.BlockSpec((bsz, TN_MOD), lambda j: (0, j)),
        out_shape=jax.ShapeDtypeStruct((bsz, n), F32),
        compiler_params=pltpu.CompilerParams(dimension_semantics=("arbitrary",),
                                             vmem_limit_bytes=VMEM_LIMIT),
        name="adaln_mod",
    )(c, w_ada, b_ada.reshape(1, n))


def _load_bf16(src_hbm, dst_ref, stage, sem):
    ncol = src_hbm.shape[1]
    nblk = src_hbm.shape[0] // W_ROWS

    def copy(n):
        return pltpu.make_async_copy(src_hbm.at[pl.ds(n * W_ROWS, W_ROWS), :],
                                     stage.at[n % 2, :, pl.ds(0, ncol)], sem.at[n % 2])

    copy(0).start()
    for n in range(nblk):
        if n + 1 < nblk:
            copy(n + 1).start()
        copy(n).wait()
        dst_ref[n * W_ROWS:(n + 1) * W_ROWS, :] = stage[n % 2, :, 0:ncol].astype(dst_ref.dtype)


def _ffn_kernel(x_ref, mod_ref, gpre_ref, gpost_ref, wg_hbm, wu_hbm, wd_hbm, o_ref,
                wg_ref, wu_ref, wd_ref, stage, sem, *, mod0):
    @pl.when((pl.program_id(0) == 0) & (pl.program_id(1) == 0))
    def _():
        _load_bf16(wg_hbm, wg_ref, stage, sem)
        _load_bf16(wu_hbm, wu_ref, stage, sem)
        _load_bf16(wd_hbm, wd_ref, stage, sem)

    shift = mod_ref[0, pl.ds(mod0, 1), :]
    scale = mod_ref[0, pl.ds(mod0 + 1, 1), :]
    gate = mod_ref[0, pl.ds(mod0 + 2, 1), :]
    d_ff = wg_ref.shape[1]
    nslab = x_ref.shape[1] // FFN_SLAB
    nchunk = d_ff // FF_CHUNK

    def rows(s):
        return slice(s * FFN_SLAB, (s + 1) * FFN_SLAB)

    def pre(s):
        return (_rms(x_ref[0, rows(s), :], gpre_ref[...]) * (1.0 + scale) + shift).astype(BF16)

    def post(s, acc):
        o_ref[0, rows(s), :] = x_ref[0, rows(s), :] + (0.5 * gate) * _rms(acc, gpost_ref[...])

    h = pre(0)
    done = None
    for s in range(nslab):
        acc = jnp.zeros((FFN_SLAB, x_ref.shape[2]), F32)
        h_next = None
        for c in range(nchunk):
            if c == nchunk // 4 and done is not None:
                post(s - 1, done)
            if c == nchunk // 2 and s + 1 < nslab:
                h_next = pre(s + 1)
            f0 = c * FF_CHUNK
            g = jnp.dot(h, wg_ref[:, f0:f0 + FF_CHUNK], preferred_element_type=F32)
            u = jnp.dot(h, wu_ref[:, f0:f0 + FF_CHUNK], preferred_element_type=F32)
            a = (g * jax.nn.sigmoid(g) * u).astype(BF16)
            acc = acc + jnp.dot(a, wd_ref[f0:f0 + FF_CHUNK, :], preferred_element_type=F32)
        done, h = acc, h_next
    post(nslab - 1, done)


def _ffn(x, mod, g_pre, g_post, wg, wu, wd, *, mod0):
    bsz, seq, d = x.shape
    d_ff = wg.shape[1]
    assert seq % TM_FFN == 0 and TM_FFN % FFN_SLAB == 0 and d_ff % FF_CHUNK == 0
    assert d % W_ROWS == 0 and d_ff % W_ROWS == 0
    blk = lambda b, j: (b, j, 0)
    hbm = pl.BlockSpec(memory_space=pl.ANY)
    return pl.pallas_call(
        functools.partial(_ffn_kernel, mod0=mod0),
        grid=(bsz, seq // TM_FFN),
        in_specs=[pl.BlockSpec((1, TM_FFN, d), blk),
                  pl.BlockSpec((1, N_MOD, d), lambda b, j: (b, 0, 0)),
                  _const_spec((1, d)), _const_spec((1, d)),
                  hbm, hbm, hbm],
        out_specs=pl.BlockSpec((1, TM_FFN, d), blk),
        out_shape=jax.ShapeDtypeStruct((bsz, seq, d), F32),
        scratch_shapes=[pltpu.VMEM((d, d_ff), BF16), pltpu.VMEM((d, d_ff), BF16),
                        pltpu.VMEM((d_ff, d), BF16),
                        pltpu.VMEM((2, W_ROWS, max(d, d_ff)), F32),
                        pltpu.SemaphoreType.DMA((2,))],
        compiler_params=pltpu.CompilerParams(dimension_semantics=("arbitrary", "arbitrary"),
                                             vmem_limit_bytes=VMEM_LIMIT),
        name="ffn",
    )(x, mod, g_pre.reshape(1, d), g_post.reshape(1, d), wg, wu, wd)


def _proj_kernel(x_ref, mod_ref, gpre_ref, w_ref, b_ref, q_ref, k_ref, vt_ref):
    j = pl.program_id(1)

    @pl.when(j == 0)
    def _():
        k_ref[...] = jnp.zeros(k_ref.shape, k_ref.dtype)
        vt_ref[...] = jnp.zeros(vt_ref.shape, vt_ref.dtype)

    @pl.when(j > 0)
    def _():
        shift = mod_ref[0, pl.ds(3, 1), :]
        scale = mod_ref[0, pl.ds(4, 1), :]
        nslab = x_ref.shape[1] // PROJ_SLAB

        def rows(s):
            return slice(s * PROJ_SLAB, (s + 1) * PROJ_SLAB)

        def pre(s):
            return (_rms(x_ref[0, rows(s), :], gpre_ref[...]) * (1.0 + scale) + shift).astype(BF16)

        low = lax.broadcasted_iota(jnp.int32, (1, LANES), 1) < HEAD_DIM
        o_ka, o_va, o_qb = QA, QA + KVA, QA + 2 * KVA
        o_kb, o_vb = o_qb + QB, o_qb + 2 * QB
        h = pre(0)
        for s in range(nslab):
            p = jnp.dot(h, w_ref[...], preferred_element_type=F32) + b_ref[...]
            h_next = pre(s + 1) if s + 1 < nslab else None
            nat = [p[:, t * LANES:(t + 1) * LANES] for t in range(QA // LANES)]
            swp = [pltpu.roll(t, HEAD_DIM, 1) for t in nat]
            qa = []
            for g in range(G_A):
                t0, t1 = g // 2, G_A // 2 + g // 2
                qa.append(jnp.where(low, nat[t0], swp[t1]) if g % 2 == 0 else jnp.where(low, swp[t0], nat[t1]))
            q = jnp.concatenate(qa + [p[:, o_qb:o_kb]], axis=1)
            q_ref[0, rows(s), :] = (q * (HEAD_DIM ** -0.5 * LOG2E)).astype(BF16)
            k_ref[0, rows(s), :] = jnp.concatenate([p[:, o_ka:o_va], p[:, o_kb:o_vb]], axis=1).astype(BF16)
            v = jnp.concatenate([p[:, o_va:o_qb], p[:, o_vb:]], axis=1)
            vt_ref[0, :, rows(s)] = v.T.astype(BF16)
            h = h_next


def _projection(x, mod, g_pre, w_all, b_all):
    bsz, seq, d = x.shape
    nq, nk, nv = QA + QB, KVA + QB, KVA + QB
    assert w_all.shape == (d, nq + nk + nv)
    assert KV_LEAD == TM_PROJ and seq % TM_PROJ == 0 and TM_PROJ % PROJ_SLAB == 0
    nblk = seq // TM_PROJ
    prev = lambda b, j: (b, jnp.maximum(j - 1, 0), 0)
    return pl.pallas_call(
        _proj_kernel,
        grid=(bsz, nblk + 1),
        in_specs=[pl.BlockSpec((1, TM_PROJ, d), prev),
                  pl.BlockSpec((1, N_MOD, d), lambda b, j: (b, 0, 0)),
                  _const_spec((1, d)),
                  _const_spec(w_all.shape), _const_spec(b_all.shape)],
        out_specs=[pl.BlockSpec((1, TM_PROJ, nq), prev),
                   pl.BlockSpec((1, TM_PROJ, nk), lambda b, j: (b, j, 0)),
                   pl.BlockSpec((1, nv, TM_PROJ), lambda b, j: (b, 0, j))],
        out_shape=[jax.ShapeDtypeStruct((bsz, seq, nq), BF16),
                   jax.ShapeDtypeStruct((bsz, seq + KV_LEAD, nk), BF16),
                   jax.ShapeDtypeStruct((bsz, nv, seq + KV_LEAD), BF16)],
        compiler_params=pltpu.CompilerParams(dimension_semantics=("arbitrary", "arbitrary"),
                                             vmem_limit_bytes=VMEM_LIMIT),
        name="qkv_proj",
    )(x, mod, g_pre.reshape(1, d), w_all, b_all)


def _bias_kernel(rel_ref, ba_ref, bb_ref):
    j = lax.broadcasted_iota(jnp.int32, (WIN_A, G_A * SUB), 0)
    r = lax.broadcasted_iota(jnp.int32, (WIN_A, G_A * SUB), 1)
    g = r // SUB
    i = r % SUB
    dist = jnp.abs(PAD_A + i - j).astype(F32)
    in_band = (j // CHUNK >= i // CHUNK) & (j // CHUNK <= i // CHUNK + BACK_A)
    for k in range(KV_A):
        slope = jnp.zeros(r.shape, F32)
        for gg in range(G_A):
            slope = jnp.where(g == gg, 2.0 ** (-8.0 * (k * G_A + gg + 1) / H_A), slope)
        for v in range(NV_A):
            valid = in_band & (j >= PAD_A - v * SUB)
            ba_ref[v, k] = jnp.where(valid, (-slope * dist) * LOG2E, NEG_INF)

    table = rel_ref[...]
    dcol = lax.broadcasted_iota(jnp.int32, (H_B, ROLL_W), 1)
    d = jnp.where(dcol < SUB, dcol, dcol - ROLL_W)
    idx = jnp.clip(PAD + d, -REL_CLIP, REL_CLIP) + REL_CLIP
    row = jnp.zeros((H_B, ROLL_W), F32)
    for n in range(N_REL):
        row = jnp.where(idx == n, table[:, n:n + 1], row)
    jb = lax.broadcasted_iota(jnp.int32, (WIN_B, SUB), 0)
    ib = lax.broadcasted_iota(jnp.int32, (WIN_B, SUB), 1)
    in_band_b = (jb // CHUNK >= ib // CHUNK) & (jb // CHUNK <= ib // CHUNK + BACK_B)
    for h in range(H_B):
        tile = jnp.broadcast_to(row[h:h + 1, :], (WIN_B, ROLL_W))
        tile = pltpu.roll(tile, 0, 1, stride=1, stride_axis=0)[:, :SUB]
        for v in range(NV_B):
            valid = in_band_b & (jb >= PAD - v * SUB)
            bb_ref[v, h // 2, :, pl.ds((h % 2) * SUB, SUB)] = jnp.where(valid, tile * LOG2E, NEG_INF)


def _bias_tiles(rel_bias):
    return pl.pallas_call(
        _bias_kernel,
        out_shape=[jax.ShapeDtypeStruct((NV_A, KV_A, WIN_A, G_A * SUB), F32),
                   jax.ShapeDtypeStruct((NV_B, H_B // 2, WIN_B, 2 * SUB), F32)],
        compiler_params=pltpu.CompilerParams(vmem_limit_bytes=VMEM_LIMIT),
        name="score_bias",
    )(rel_bias)


def _rms_rows(xt, g_col):
    ms = jnp.mean(xt * xt, axis=0, keepdims=True)
    return xt * lax.rsqrt(ms + EPS) * g_col


def _mixer_kernel(sink_ref, q_ref, k_ref, vt_ref, ba_ref, bb_ref, gga_ref, ggb_ref, wo_ref, bo_ref,
                  gpost_ref, mod_ref, x_ref, o_ref):
    i = pl.program_id(1)
    ws_b = [pl.multiple_of(i * TQ + u * SUB + KV_LEAD - PAD, SUB) for u in range(NSUB)]
    ws_a = [pl.multiple_of(i * TQ + u * SUB + KV_LEAD - PAD_A, LANES) for u in range(NSUB)]
    var_b = [jnp.minimum(i * NSUB + u, NV_B - 1) for u in range(NSUB)]
    var_a = [jnp.minimum(i * NSUB + u, NV_A - 1) for u in range(NSUB)]

    low = lax.broadcasted_iota(jnp.int32, (1, LANES), 1) < HEAD_DIM
    halves = (low, jnp.logical_not(low))
    col_g = lax.broadcasted_iota(jnp.int32, (1, G_A * SUB), 1) // SUB
    ones_a = jnp.ones((DEN_ROWS, WIN_A), BF16)
    ones_b = jnp.ones((DEN_ROWS, WIN_B), BF16)

    def scores_a(u, k):
        k_a = k_ref[0, pl.ds(ws_a[u], WIN_A), 0:LANES]
        qs = jnp.concatenate(
            [jnp.where(halves[k], q_ref[0, u * SUB:(u + 1) * SUB, g * LANES:(g + 1) * LANES], 0)
             for g in range(G_A)], axis=0)
        s = lax.dot_general(k_a, qs, _NT, preferred_element_type=F32)
        return s + ba_ref[var_a[u], k]

    def attend_a(u, k, s):
        sink = jnp.zeros((1, G_A * SUB), F32)
        for g in range(G_A):
            sink = jnp.where(col_g == g, sink_ref[k * G_A + g] * LOG2E, sink)
        m = jnp.maximum(jnp.max(s, axis=0, keepdims=True), sink)
        e = jnp.exp2(s - m).astype(BF16)
        vt_a = jnp.concatenate([vt_ref[0, 0:LANES, pl.ds(ws_a[u], WIN_A)], ones_a], axis=0)
        o = jnp.dot(vt_a, e, preferred_element_type=F32)
        den = o[LANES:LANES + 1] + jnp.exp2(sink - m)
        o = o[k * HEAD_DIM:(k + 1) * HEAD_DIM] * (1.0 / den)
        return [o[:, g * SUB:(g + 1) * SUB] for g in range(G_A)]

    def scores_b(u, t):
        q_t = q_ref[0, u * SUB:(u + 1) * SUB, QA + t * LANES:QA + (t + 1) * LANES]
        k_t = k_ref[0, pl.ds(ws_b[u], WIN_B), (1 + t) * LANES:(2 + t) * LANES]
        q2 = jnp.concatenate([jnp.where(halves[0], q_t, 0), jnp.where(halves[1], q_t, 0)], axis=0)
        s = lax.dot_general(k_t, q2, _NT, preferred_element_type=F32)
        return s + bb_ref[var_b[u], t]

    def attend_b(u, t, s):
        m = jnp.max(s, axis=0, keepdims=True)
        e = jnp.exp2(s - m).astype(BF16)
        vt_t = jnp.concatenate([vt_ref[0, (1 + t) * LANES:(2 + t) * LANES, pl.ds(ws_b[u], WIN_B)],
                                ones_b], axis=0)
        o = jnp.dot(vt_t, e, preferred_element_type=F32)
        rden = 1.0 / o[LANES:LANES + 1]
        return [o[:HEAD_DIM, :SUB] * rden[:, :SUB], o[HEAD_DIM:LANES, SUB:] * rden[:, SUB:]]

    groups = []
    for u in range(NSUB):
        groups += [(scores_a, attend_a, u, k) for k in range(KV_A)]
        groups += [(scores_b, attend_b, u, t) for t in range(H_B // 2)]
    per_sub = len(groups) // NSUB
    outs = []

    def finish(first_sub):
        oa_t, ob_t = [], []
        for u in range(first_sub, first_sub + FIN_SUBS):
            sub = outs[u * per_sub:(u + 1) * per_sub]
            oa_t.append(jnp.concatenate([sub[k][g] for k in range(KV_A) for g in range(G_A)], axis=0))
            ob_t.append(jnp.concatenate([part for pair in sub[KV_A:] for part in pair], axis=0))
        oa_t = jnp.concatenate(oa_t, axis=1)
        ob_t = jnp.concatenate(ob_t, axis=1)
        y_t = jnp.concatenate([_rms_rows(oa_t, gga_ref[...]), _rms_rows(ob_t, ggb_ref[...])], axis=0)
        z = lax.dot_general(y_t.astype(BF16), wo_ref[...], _TN, preferred_element_type=F32) + bo_ref[...]
        gate = mod_ref[0, pl.ds(5, 1), :]
        r = slice(first_sub * SUB, (first_sub + FIN_SUBS) * SUB)
        o_ref[0, r, :] = x_ref[0, r, :] + gate * _rms(z, gpost_ref[...])

    pending = [grp[0](*grp[2:]) for grp in groups[:SCORE_AHEAD]]
    for n, (_, attend, u, idx) in enumerate(groups):
        if n + SCORE_AHEAD < len(groups):
            nxt = groups[n + SCORE_AHEAD]
            pending.append(nxt[0](*nxt[2:]))
        outs.append(attend(u, idx, pending.pop(0)))
        done_groups = n + 1 - FIN_LAG
        if 0 < done_groups < len(groups) and done_groups % (FIN_SUBS * per_sub) == 0:
            finish(done_groups // per_sub - FIN_SUBS)
    finish(NSUB - FIN_SUBS)


def _mixer(x, mod, q, k, vt, bias_a, bias_b, sinks, gga, ggb, wo, bo, g_post):
    bsz, seq, d = x.shape
    assert seq % TQ == 0 and TQ % SUB == 0
    tile = lambda b, i: (b, i, 0)
    whole = lambda b, i: (b, 0, 0)
    return pl.pallas_call(
        _mixer_kernel,
        grid=(bsz, seq // TQ),
        in_specs=[pl.BlockSpec(memory_space=pltpu.SMEM),
                  pl.BlockSpec((1, TQ, q.shape[2]), tile),
                  pl.BlockSpec((1,) + k.shape[1:], whole),
                  pl.BlockSpec((1,) + vt.shape[1:], whole),
                  _const_spec(bias_a.shape), _const_spec(bias_b.shape),
                  _const_spec((QA, 1)), _const_spec((QB, 1)),
                  _const_spec(wo.shape), _const_spec((1, d)), _const_spec((1, d)),
                  pl.BlockSpec((1, N_MOD, d), whole),
                  pl.BlockSpec((1, TQ, d), tile)],
        out_specs=pl.BlockSpec((1, TQ, d), tile),
        out_shape=jax.ShapeDtypeStruct((bsz, seq, d), F32),
        compiler_params=pltpu.CompilerParams(dimension_semantics=("arbitrary", "arbitrary"),
                                             vmem_limit_bytes=VMEM_LIMIT),
        name="mixer",
    )(sinks, q, k, vt, bias_a, bias_b, gga.reshape(QA, 1), ggb.reshape(QB, 1), wo,
      bo.reshape(1, d), g_post.reshape(1, d), mod, x)


def kernel(x, c, w_ada, b_ada, g_pre_ffn1, w_gate1, w_up1, w_down1, g_post_ffn1, g_pre_mix, w_in, b_in, sinks_a, rel_bias_b, g_grp_a, g_grp_b, w_out, b_out, g_post_mix, g_pre_ffn2, w_gate2, w_up2, w_down2, g_post_ffn2):
    bsz, seq, d = x.shape
    depth = w_ada.shape[0]
    for l in range(depth):
        mod = _modulation(c, w_ada[l], b_ada[l]).reshape(bsz, N_MOD, d)

        x = _ffn(x, mod, g_pre_ffn1[l], g_post_ffn1[l], w_gate1[l], w_up1[l], w_down1[l], mod0=0)

        q, k, vt = _projection(x, mod, g_pre_mix[l], w_in[l].astype(BF16), b_in[l].reshape(1, -1))

        bias_a, bias_b = _bias_tiles(rel_bias_b[l])
        x = _mixer(x, mod, q, k, vt, bias_a, bias_b, sinks_a[l], g_grp_a[l], g_grp_b[l],
                   w_out[l].astype(BF16), b_out[l], g_post_mix[l])

        x = _ffn(x, mod, g_pre_ffn2[l], g_post_ffn2[l], w_gate2[l], w_up2[l], w_down2[l], mod0=6)
    return x
```

```python
import functools
import math

import jax
import jax.numpy as jnp
from jax import lax
from jax.experimental import pallas as pl
from jax.experimental.pallas import tpu as pltpu

F32 = jnp.float32
BF16 = jnp.bfloat16

CHUNK = 64
HEAD_DIM = 64
EPS = 1e-6
NEG_INF = -1e30
H_A, KV_A = 8, 2
G_A = H_A // KV_A
BACK_A = 2
H_B = 8
BACK_B = 8
REL_CLIP = 128
N_REL = 2 * REL_CLIP + 1
QA = H_A * HEAD_DIM
KVA = KV_A * HEAD_DIM
QB = H_B * HEAD_DIM
N_MOD = 9
LOG2E = math.log2(math.e)

LANES = 128
TQ = 512
SUB = LANES
NSUB = TQ // SUB
SCORE_AHEAD = 3
FIN_SUBS = 2
FIN_LAG = 3
PAD = BACK_B * CHUNK
PAD_A = BACK_A * CHUNK
WIN_A = SUB + PAD_A
WIN_B = SUB + PAD
NV_A = PAD_A // SUB + 1
NV_B = PAD // SUB + 1
EXT_B = PAD + SUB + CHUNK
DEN_ROWS = 16
ROLL_W = 1024
TM_FFN = 1024
FFN_SLAB = 512
W_ROWS = 256
TM_PROJ = 1024
PROJ_SLAB = 512
TN_MOD = 1024
FF_CHUNK = 256
VMEM_LIMIT = 56 * 1024 * 1024

_NT = (((1,), (1,)), ((), ()))
_TN = (((0,), (0,)), ((), ()))


def _rms(x, g):
    ms = jnp.mean(x * x, axis=-1, keepdims=True)
    return x * lax.rsqrt(ms + EPS) * g


def _const_spec(shape):
    nd = len(shape)
    return pl.BlockSpec(shape, lambda *_: (0,) * nd, pipeline_mode=pl.Buffered(1))


def _mod_kernel(c_ref, w_ref, b_ref, o_ref):
    c = c_ref[...]
    a = (c * jax.nn.sigmoid(c)).astype(BF16)
    o_ref[...] = jnp.dot(a, w_ref[...].astype(BF16), preferred_element_type=F32) + b_ref[...]


def _modulation(c, w_ada, b_ada):
    bsz, d = c.shape
    n = w_ada.shape[1]
    assert n % TN_MOD == 0
    return pl.pallas_call(
        _mod_kernel,
        grid=(n // TN_MOD,),
        in_specs=[pl.BlockSpec((bsz, d), lambda j: (0, 0)),
                  pl.BlockSpec((d, TN_MOD), lambda j: (0, j)),
                  pl.BlockSpec((1, TN_MOD), lambda j: (0, j))],
        out_specs=pl.BlockSpec((bsz, TN_MOD), lambda j: (0, j)),
        out_shape=jax.ShapeDtypeStruct((bsz, n), F32),
        compiler_params=pltpu.CompilerParams(dimension_semantics=("arbitrary",),
                                             vmem_limit_bytes=VMEM_LIMIT),
        name="adaln_mod",
    )(c, w_ada, b_ada.reshape(1, n))


def _load_bf16(src_hbm, dst_ref, stage, sem):
    ncol = src_hbm.shape[1]
    nblk = src_hbm.shape[0] // W_ROWS

    def copy(n):
        return pltpu.make_async_copy(src_hbm.at[pl.ds(n * W_ROWS, W_ROWS), :],
                                     stage.at[n % 2, :, pl.ds(0, ncol)], sem.at[n % 2])

    copy(0).start()
    for n in range(nblk):
        if n + 1 < nblk:
            copy(n + 1).start()
        copy(n).wait()
        dst_ref[n * W_ROWS:(n + 1) * W_ROWS, :] = stage[n % 2, :, 0:ncol].astype(dst_ref.dtype)


def _ffn_kernel(x_ref, mod_ref, gpre_ref, gpost_ref, wg_hbm, wu_hbm, wd_hbm, o_ref,
                wg_ref, wu_ref, wd_ref, stage, sem, *, mod0):
    @pl.when((pl.program_id(0) == 0) & (pl.program_id(1) == 0))
    def _():
        _load_bf16(wg_hbm, wg_ref, stage, sem)
        _load_bf16(wu_hbm, wu_ref, stage, sem)
        _load_bf16(wd_hbm, wd_ref, stage, sem)

    shift = mod_ref[0, pl.ds(mod0, 1), :]
    scale = mod_ref[0, pl.ds(mod0 + 1, 1), :]
    gate = mod_ref[0, pl.ds(mod0 + 2, 1), :]
    d_ff = wg_ref.shape[1]
    nslab = x_ref.shape[1] // FFN_SLAB
    nchunk = d_ff // FF_CHUNK

    def rows(s):
        return slice(s * FFN_SLAB, (s + 1) * FFN_SLAB)

    def pre(s):
        return (_rms(x_ref[0, rows(s), :], gpre_ref[...]) * (1.0 + scale) + shift).astype(BF16)

    def post(s, acc):
        o_ref[0, rows(s), :] = x_ref[0, rows(s), :] + (0.5 * gate) * _rms(acc, gpost_ref[...])

    h = pre(0)
    done = None
    for s in range(nslab):
        acc = jnp.zeros((FFN_SLAB, x_ref.shape[2]), F32)
        h_next = None
        for c in range(nchunk):
            if c == nchunk // 4 and done is not None:
                post(s - 1, done)
            if c == nchunk // 2 and s + 1 < nslab:
                h_next = pre(s + 1)
            f0 = c * FF_CHUNK
            g = jnp.dot(h, wg_ref[:, f0:f0 + FF_CHUNK], preferred_element_type=F32)
            u = jnp.dot(h, wu_ref[:, f0:f0 + FF_CHUNK], preferred_element_type=F32)
            a = (g * jax.nn.sigmoid(g) * u).astype(BF16)
            acc = acc + jnp.dot(a, wd_ref[f0:f0 + FF_CHUNK, :], preferred_element_type=F32)
        done, h = acc, h_next
    post(nslab - 1, done)


def _ffn(x, mod, g_pre, g_post, wg, wu, wd, *, mod0):
    bsz, seq, d = x.shape
    d_ff = wg.shape[1]
    assert seq % TM_FFN == 0 and TM_FFN % FFN_SLAB == 0 and d_ff % FF_CHUNK == 0
    assert d % W_ROWS == 0 and d_ff % W_ROWS == 0
    blk = lambda b, j: (b, j, 0)
    hbm = pl.BlockSpec(memory_space=pl.ANY)
    return pl.pallas_call(
        functools.partial(_ffn_kernel, mod0=mod0),
        grid=(bsz, seq // TM_FFN),
        in_specs=[pl.BlockSpec((1, TM_FFN, d), blk),
                  pl.BlockSpec((1, N_MOD, d), lambda b, j: (b, 0, 0)),
                  _const_spec((1, d)), _const_spec((1, d)),
                  hbm, hbm, hbm],
        out_specs=pl.BlockSpec((1, TM_FFN, d), blk),
        out_shape=jax.ShapeDtypeStruct((bsz, seq, d), F32),
        scratch_shapes=[pltpu.VMEM((d, d_ff), BF16), pltpu.VMEM((d, d_ff), BF16),
                        pltpu.VMEM((d_ff, d), BF16),
                        pltpu.VMEM((2, W_ROWS, max(d, d_ff)), F32),
                        pltpu.SemaphoreType.DMA((2,))],
        compiler_params=pltpu.CompilerParams(dimension_semantics=("arbitrary", "arbitrary"),
                                             vmem_limit_bytes=VMEM_LIMIT),
        name="ffn",
    )(x, mod, g_pre.reshape(1, d), g_post.reshape(1, d), wg, wu, wd)


def _proj_kernel(x_ref, mod_ref, gpre_ref, w_ref, b_ref, q_ref, k_ref, vt_ref):
    shift = mod_ref[0, pl.ds(3, 1), :]
    scale = mod_ref[0, pl.ds(4, 1), :]
    nslab = x_ref.shape[1] // PROJ_SLAB

    def rows(s):
        return slice(s * PROJ_SLAB, (s + 1) * PROJ_SLAB)

    def pre(s):
        return (_rms(x_ref[0, rows(s), :], gpre_ref[...]) * (1.0 + scale) + shift).astype(BF16)

    low = lax.broadcasted_iota(jnp.int32, (1, LANES), 1) < HEAD_DIM
    o_ka, o_va, o_qb = QA, QA + KVA, QA + 2 * KVA
    o_kb, o_vb = o_qb + QB, o_qb + 2 * QB
    h = pre(0)
    for s in range(nslab):
        p = jnp.dot(h, w_ref[...], preferred_element_type=F32) + b_ref[...]
        h_next = pre(s + 1) if s + 1 < nslab else None
        nat = [p[:, t * LANES:(t + 1) * LANES] for t in range(QA // LANES)]
        swp = [pltpu.roll(t, HEAD_DIM, 1) for t in nat]
        qa = []
        for g in range(G_A):
            t0, t1 = g // 2, G_A // 2 + g // 2
            qa.append(jnp.where(low, nat[t0], swp[t1]) if g % 2 == 0 else jnp.where(low, swp[t0], nat[t1]))
        q = jnp.concatenate(qa + [p[:, o_qb:o_kb]], axis=1)
        q_ref[0, rows(s), :] = (q * (HEAD_DIM ** -0.5 * LOG2E)).astype(BF16)
        k_ref[0, rows(s), :] = jnp.concatenate([p[:, o_ka:o_va], p[:, o_kb:o_vb]], axis=1).astype(BF16)
        v = jnp.concatenate([p[:, o_va:o_qb], p[:, o_vb:]], axis=1)
        vt_ref[0, :, rows(s)] = v.T.astype(BF16)
        h = h_next


def _projection(x, mod, g_pre, w_all, b_all):
    bsz, seq, d = x.shape
    nq, nk, nv = QA + QB, KVA + QB, KVA + QB
    assert w_all.shape == (d, nq + nk + nv)
    assert seq % TM_PROJ == 0 and TM_PROJ % PROJ_SLAB == 0
    blk = lambda b, j: (b, j, 0)
    return pl.pallas_call(
        _proj_kernel,
        grid=(bsz, seq // TM_PROJ),
        in_specs=[pl.BlockSpec((1, TM_PROJ, d), blk),
                  pl.BlockSpec((1, N_MOD, d), lambda b, j: (b, 0, 0)),
                  _const_spec((1, d)),
                  _const_spec(w_all.shape), _const_spec(b_all.shape)],
        out_specs=[pl.BlockSpec((1, TM_PROJ, nq), blk),
                   pl.BlockSpec((1, TM_PROJ, nk), blk),
                   pl.BlockSpec((1, nv, TM_PROJ), lambda b, j: (b, 0, j))],
        out_shape=[jax.ShapeDtypeStruct((bsz, seq, nq), BF16),
                   jax.ShapeDtypeStruct((bsz, seq, nk), BF16),
                   jax.ShapeDtypeStruct((bsz, nv, seq), BF16)],
        compiler_params=pltpu.CompilerParams(dimension_semantics=("arbitrary", "arbitrary"),
                                             vmem_limit_bytes=VMEM_LIMIT),
        name="qkv_proj",
    )(x, mod, g_pre.reshape(1, d), w_all, b_all)


def _bias_kernel(rel_ref, ba_ref, bb_ref):
    j = lax.broadcasted_iota(jnp.int32, (WIN_A, G_A * SUB), 0)
    r = lax.broadcasted_iota(jnp.int32, (WIN_A, G_A * SUB), 1)
    g = r // SUB
    i = r % SUB
    for k in range(KV_A):
        slope = jnp.zeros(r.shape, F32)
        for gg in range(G_A):
            slope = jnp.where(g == gg, 2.0 ** (-8.0 * (k * G_A + gg + 1) / H_A), slope)
        for v in range(NV_A):
            off = min(v * SUB, PAD_A)
            back = off // CHUNK + i // CHUNK - j // CHUNK
            bias = -slope * jnp.abs(off + i - j).astype(F32)
            ba_ref[v, k] = jnp.where((back >= 0) & (back <= BACK_A), bias * LOG2E, NEG_INF)

    table = rel_ref[...]
    dcol = lax.broadcasted_iota(jnp.int32, (H_B, ROLL_W), 1)
    d = jnp.where(dcol < SUB, dcol, dcol - ROLL_W)
    idx = jnp.clip(PAD + d, -REL_CLIP, REL_CLIP) + REL_CLIP
    row = jnp.zeros((H_B, ROLL_W), F32)
    for n in range(N_REL):
        row = jnp.where(idx == n, table[:, n:n + 1], row)
    jb = lax.broadcasted_iota(jnp.int32, (WIN_B, SUB), 0)
    ib = lax.broadcasted_iota(jnp.int32, (WIN_B, SUB), 1)
    for h in range(H_B):
        tile = jnp.broadcast_to(row[h:h + 1, :], (EXT_B, ROLL_W))
        tile = pltpu.roll(tile, 0, 1, stride=1, stride_axis=0)[:, :SUB]
        tile = jnp.concatenate([tile, jnp.zeros((PAD + WIN_B - EXT_B, SUB), F32)], axis=0)
        for v in range(NV_B):
            off = min(v * SUB, PAD)
            back = off // CHUNK + ib // CHUNK - jb // CHUNK
            window = tile[PAD - off:PAD - off + WIN_B]
            bb_ref[v, h // 2, :, pl.ds((h % 2) * SUB, SUB)] = jnp.where(
                (back >= 0) & (back <= BACK_B), window * LOG2E, NEG_INF)


def _bias_tiles(rel_bias):
    return pl.pallas_call(
        _bias_kernel,
        out_shape=[jax.ShapeDtypeStruct((NV_A, KV_A, WIN_A, G_A * SUB), F32),
                   jax.ShapeDtypeStruct((NV_B, H_B // 2, WIN_B, 2 * SUB), F32)],
        compiler_params=pltpu.CompilerParams(vmem_limit_bytes=VMEM_LIMIT),
        name="score_bias",
    )(rel_bias)


def _rms_rows(xt, g_col):
    ms = jnp.mean(xt * xt, axis=0, keepdims=True)
    return xt * lax.rsqrt(ms + EPS) * g_col


def _mixer_kernel(sink_ref, q_ref, k_ref, vt_ref, ba_ref, bb_ref, gga_ref, ggb_ref, wo_ref, bo_ref,
                  gpost_ref, mod_ref, x_ref, o_ref):
    i = pl.program_id(1)
    ws_b = [pl.multiple_of(jnp.maximum(i * TQ + u * SUB - PAD, 0), SUB) for u in range(NSUB)]
    ws_a = [pl.multiple_of(jnp.maximum(i * TQ + u * SUB - PAD_A, 0), SUB) for u in range(NSUB)]
    var_b = [jnp.minimum(i * NSUB + u, NV_B - 1) for u in range(NSUB)]
    var_a = [jnp.minimum(i * NSUB + u, NV_A - 1) for u in range(NSUB)]

    low = lax.broadcasted_iota(jnp.int32, (1, LANES), 1) < HEAD_DIM
    halves = (low, jnp.logical_not(low))
    col_g = lax.broadcasted_iota(jnp.int32, (1, G_A * SUB), 1) // SUB
    ones_a = jnp.ones((DEN_ROWS, WIN_A), BF16)
    ones_b = jnp.ones((DEN_ROWS, WIN_B), BF16)

    def scores_a(u, k):
        k_a = k_ref[0, pl.ds(ws_a[u], WIN_A), 0:LANES]
        qs = jnp.concatenate(
            [jnp.where(halves[k], q_ref[0, u * SUB:(u + 1) * SUB, g * LANES:(g + 1) * LANES], 0)
             for g in range(G_A)], axis=0)
        s = lax.dot_general(k_a, qs, _NT, preferred_element_type=F32)
        return s + ba_ref[var_a[u], k]

    def attend_a(u, k, s):
        sink = jnp.zeros((1, G_A * SUB), F32)
        for g in range(G_A):
            sink = jnp.where(col_g == g, sink_ref[k * G_A + g] * LOG2E, sink)
        m = jnp.maximum(jnp.max(s, axis=0, keepdims=True), sink)
        e = jnp.exp2(s - m).astype(BF16)
        vt_a = jnp.concatenate([vt_ref[0, 0:LANES, pl.ds(ws_a[u], WIN_A)], ones_a], axis=0)
        o = jnp.dot(vt_a, e, preferred_element_type=F32)
        den = o[LANES:LANES + 1] + jnp.exp2(sink - m)
        o = o[k * HEAD_DIM:(k + 1) * HEAD_DIM] * (1.0 / den)
        return [o[:, g * SUB:(g + 1) * SUB] for g in range(G_A)]

    def scores_b(u, t):
        q_t = q_ref[0, u * SUB:(u + 1) * SUB, QA + t * LANES:QA + (t + 1) * LANES]
        k_t = k_ref[0, pl.ds(ws_b[u], WIN_B), (1 + t) * LANES:(2 + t) * LANES]
        q2 = jnp.concatenate([jnp.where(halves[0], q_t, 0), jnp.where(halves[1], q_t, 0)], axis=0)
        s = lax.dot_general(k_t, q2, _NT, preferred_element_type=F32)
        return s + bb_ref[var_b[u], t]

    def attend_b(u, t, s):
        m = jnp.max(s, axis=0, keepdims=True)
        e = jnp.exp2(s - m).astype(BF16)
        vt_t = jnp.concatenate([vt_ref[0, (1 + t) * LANES:(2 + t) * LANES, pl.ds(ws_b[u], WIN_B)],
                                ones_b], axis=0)
        o = jnp.dot(vt_t, e, preferred_element_type=F32)
        rden = 1.0 / o[LANES:LANES + 1]
        return [o[:HEAD_DIM, :SUB] * rden[:, :SUB], o[HEAD_DIM:LANES, SUB:] * rden[:, SUB:]]

    groups = []
    for u in range(NSUB):
        groups += [(scores_a, attend_a, u, k) for k in range(KV_A)]
        groups += [(scores_b, attend_b, u, t) for t in range(H_B // 2)]
    per_sub = len(groups) // NSUB
    outs = []

    def finish(first_sub):
        oa_t, ob_t = [], []
        for u in range(first_sub, first_sub + FIN_SUBS):
            sub = outs[u * per_sub:(u + 1) * per_sub]
            oa_t.append(jnp.concatenate([sub[k][g] for k in range(KV_A) for g in range(G_A)], axis=0))
            ob_t.append(jnp.concatenate([part for pair in sub[KV_A:] for part in pair], axis=0))
        oa_t = jnp.concatenate(oa_t, axis=1)
        ob_t = jnp.concatenate(ob_t, axis=1)
        y_t = jnp.concatenate([_rms_rows(oa_t, gga_ref[...]), _rms_rows(ob_t, ggb_ref[...])], axis=0)
        z = lax.dot_general(y_t.astype(BF16), wo_ref[...], _TN, preferred_element_type=F32) + bo_ref[...]
        gate = mod_ref[0, pl.ds(5, 1), :]
        r = slice(first_sub * SUB, (first_sub + FIN_SUBS) * SUB)
        o_ref[0, r, :] = x_ref[0, r, :] + gate * _rms(z, gpost_ref[...])

    pending = [grp[0](*grp[2:]) for grp in groups[:SCORE_AHEAD]]
    for n, (_, attend, u, idx) in enumerate(groups):
        if n + SCORE_AHEAD < len(groups):
            nxt = groups[n + SCORE_AHEAD]
            pending.append(nxt[0](*nxt[2:]))
        outs.append(attend(u, idx, pending.pop(0)))
        done_groups = n + 1 - FIN_LAG
        if 0 < done_groups < len(groups) and done_groups % (FIN_SUBS * per_sub) == 0:
            finish(done_groups // per_sub - FIN_SUBS)
    finish(NSUB - FIN_SUBS)


def _mixer(x, mod, q, k, vt, bias_a, bias_b, sinks, gga, ggb, wo, bo, g_post):
    bsz, seq, d = x.shape
    assert seq % TQ == 0 and TQ % SUB == 0
    tile = lambda b, i: (b, i, 0)
    whole = lambda b, i: (b, 0, 0)
    return pl.pallas_call(
        _mixer_kernel,
        grid=(bsz, seq // TQ),
        in_specs=[pl.BlockSpec(memory_space=pltpu.SMEM),
                  pl.BlockSpec((1, TQ, q.shape[2]), tile),
                  pl.BlockSpec((1,) + k.shape[1:], whole),
                  pl.BlockSpec((1,) + vt.shape[1:], whole),
                  _const_spec(bias_a.shape), _const_spec(bias_b.shape),
                  _const_spec((QA, 1)), _const_spec((QB, 1)),
                  _const_spec(wo.shape), _const_spec((1, d)), _const_spec((1, d)),
                  pl.BlockSpec((1, N_MOD, d), whole),
                  pl.BlockSpec((1, TQ, d), tile)],
        out_specs=pl.BlockSpec((1, TQ, d), tile),
        out_shape=jax.ShapeDtypeStruct((bsz, seq, d), F32),
        compiler_params=pltpu.CompilerParams(dimension_semantics=("arbitrary", "arbitrary"),
                                             vmem_limit_bytes=VMEM_LIMIT),
        name="mixer",
    )(sinks, q, k, vt, bias_a, bias_b, gga.reshape(QA, 1), ggb.reshape(QB, 1), wo,
      bo.reshape(1, d), g_post.reshape(1, d), mod, x)


def kernel(x, c, w_ada, b_ada, g_pre_ffn1, w_gate1, w_up1, w_down1, g_post_ffn1, g_pre_mix, w_in, b_in, sinks_a, rel_bias_b, g_grp_a, g_grp_b, w_out, b_out, g_post_mix, g_pre_ffn2, w_gate2, w_up2, w_down2, g_post_ffn2):
    bsz, seq, d = x.shape
    depth = w_ada.shape[0]
    for l in range(depth):
        mod = _modulation(c, w_ada[l], b_ada[l]).reshape(bsz, N_MOD, d)

        x = _ffn(x, mod, g_pre_ffn1[l], g_post_ffn1[l], w_gate1[l], w_up1[l], w_down1[l], mod0=0)

        q, k, vt = _projection(x, mod, g_pre_mix[l], w_in[l].astype(BF16), b_in[l].reshape(1, -1))

        bias_a, bias_b = _bias_tiles(rel_bias_b[l])
        x = _mixer(x, mod, q, k, vt, bias_a, bias_b, sinks_a[l], g_grp_a[l], g_grp_b[l],
                   w_out[l].astype(BF16), b_out[l], g_post_mix[l])

        x = _ffn(x, mod, g_pre_ffn2[l], g_post_ffn2[l], w_gate2[l], w_up2[l], w_down2[l], mod0=6)
    return x
```

```python
import functools
import math

import jax
import jax.numpy as jnp
from jax import lax
from jax.experimental import pallas as pl
from jax.experimental.pallas import tpu as pltpu

F32 = jnp.float32
BF16 = jnp.bfloat16

CHUNK = 64
HEAD_DIM = 64
EPS = 1e-6
NEG_INF = -1e30
H_A, KV_A = 8, 2
G_A = H_A // KV_A
BACK_A = 2
H_B = 8
BACK_B = 8
REL_CLIP = 128
N_REL = 2 * REL_CLIP + 1
QA = H_A * HEAD_DIM
KVA = KV_A * HEAD_DIM
QB = H_B * HEAD_DIM
N_MOD = 9
LOG2E = math.log2(math.e)

LANES = 128
TQ = 512
SUB = LANES
NSUB = TQ // SUB
SCORE_AHEAD = 3
FIN_SUBS = 2
FIN_LAG = 3
PAD = BACK_B * CHUNK
PAD_A = BACK_A * CHUNK
WIN_A = SUB + PAD_A
WIN_B = SUB + PAD
NV_A = PAD_A // SUB + 1
NV_B = PAD // SUB + 1
EXT_B = PAD + SUB + CHUNK
DEN_ROWS = 16
ROLL_W = 1024
TM_FFN = 1024
FFN_SLAB = 512
W_SLOTS = 3
TM_PROJ = 1024
PROJ_SLAB = 512
TN_MOD = 1024
FF_CHUNK = 256
VMEM_LIMIT = 56 * 1024 * 1024

_NT = (((1,), (1,)), ((), ()))
_TN = (((0,), (0,)), ((), ()))


def _rms(x, g):
    ms = jnp.mean(x * x, axis=-1, keepdims=True)
    return x * lax.rsqrt(ms + EPS) * g


def _const_spec(shape):
    nd = len(shape)
    return pl.BlockSpec(shape, lambda *_: (0,) * nd, pipeline_mode=pl.Buffered(1))


def _mod_kernel(c_ref, w_ref, b_ref, o_ref):
    c = c_ref[...]
    a = (c * jax.nn.sigmoid(c)).astype(BF16)
    o_ref[...] = jnp.dot(a, w_ref[...].astype(BF16), preferred_element_type=F32) + b_ref[...]


def _modulation(c, w_ada, b_ada):
    bsz, d = c.shape
    n = w_ada.shape[1]
    assert n % TN_MOD == 0
    return pl.pallas_call(
        _mod_kernel,
        grid=(n // TN_MOD,),
        in_specs=[pl.BlockSpec((bsz, d), lambda j: (0, 0)),
                  pl.BlockSpec((d, TN_MOD), lambda j: (0, j)),
                  pl.BlockSpec((1, TN_MOD), lambda j: (0, j))],
        out_specs=pl.BlockSpec((bsz, TN_MOD), lambda j: (0, j)),
        out_shape=jax.ShapeDtypeStruct((bsz, n), F32),
        compiler_params=pltpu.CompilerParams(dimension_semantics=("arbitrary",),
                                             vmem_limit_bytes=VMEM_LIMIT),
        name="adaln_mod",
    )(c, w_ada, b_ada.reshape(1, n))


def _ffn_kernel(x_ref, mod_ref, gpre_ref, gpost_ref, wg_hbm, wu_hbm, wd_hbm, o_ref,
                wg_ref, wu_ref, wd_ref, stage_g, stage_u, stage_d, sem, *, mod0):
    shift = mod_ref[0, pl.ds(mod0, 1), :]
    scale = mod_ref[0, pl.ds(mod0 + 1, 1), :]
    gate = mod_ref[0, pl.ds(mod0 + 2, 1), :]
    d_ff = wg_ref.shape[1]
    nslab = x_ref.shape[1] // FFN_SLAB
    nchunk = d_ff // FF_CHUNK

    def rows(s):
        return slice(s * FFN_SLAB, (s + 1) * FFN_SLAB)

    def cols(c):
        return pl.ds(c * FF_CHUNK, FF_CHUNK)

    def pre(s):
        return (_rms(x_ref[0, rows(s), :], gpre_ref[...]) * (1.0 + scale) + shift).astype(BF16)

    def post(s, acc):
        o_ref[0, rows(s), :] = x_ref[0, rows(s), :] + (0.5 * gate) * _rms(acc, gpost_ref[...])

    def weight_copies(c):
        slot = c % W_SLOTS
        return (pltpu.make_async_copy(wg_hbm.at[:, cols(c)], stage_g.at[slot], sem.at[0, slot]),
                pltpu.make_async_copy(wu_hbm.at[:, cols(c)], stage_u.at[slot], sem.at[1, slot]),
                pltpu.make_async_copy(wd_hbm.at[cols(c), :], stage_d.at[slot], sem.at[2, slot]))

    def body(load_weights):
        if load_weights:
            for c in range(W_SLOTS):
                for cp in weight_copies(c):
                    cp.start()
        h = pre(0)
        done = None
        for s in range(nslab):
            acc = jnp.zeros((FFN_SLAB, x_ref.shape[2]), F32)
            h_next = None
            for c in range(nchunk):
                if c == nchunk // 4 and done is not None:
                    post(s - 1, done)
                if c == nchunk // 2 and s + 1 < nslab:
                    h_next = pre(s + 1)
                f0 = c * FF_CHUNK
                if load_weights and s == 0:
                    slot = c % W_SLOTS
                    for cp in weight_copies(c):
                        cp.wait()
                    wg_ref[:, f0:f0 + FF_CHUNK] = stage_g[slot].astype(BF16)
                    wu_ref[:, f0:f0 + FF_CHUNK] = stage_u[slot].astype(BF16)
                    wd_ref[f0:f0 + FF_CHUNK, :] = stage_d[slot].astype(BF16)
                    if c + W_SLOTS < nchunk:
                        for cp in weight_copies(c + W_SLOTS):
                            cp.start()
                g = jnp.dot(h, wg_ref[:, f0:f0 + FF_CHUNK], preferred_element_type=F32)
                u = jnp.dot(h, wu_ref[:, f0:f0 + FF_CHUNK], preferred_element_type=F32)
                a = (g * jax.nn.sigmoid(g) * u).astype(BF16)
                acc = acc + jnp.dot(a, wd_ref[f0:f0 + FF_CHUNK, :], preferred_element_type=F32)
            done, h = acc, h_next
        post(nslab - 1, done)

    first = (pl.program_id(0) == 0) & (pl.program_id(1) == 0)
    pl.when(first)(functools.partial(body, True))
    pl.when(jnp.logical_not(first))(functools.partial(body, False))


def _ffn(x, mod, g_pre, g_post, wg, wu, wd, *, mod0):
    bsz, seq, d = x.shape
    d_ff = wg.shape[1]
    assert seq % TM_FFN == 0 and TM_FFN % FFN_SLAB == 0 and d_ff % FF_CHUNK == 0
    assert d_ff // FF_CHUNK >= W_SLOTS
    blk = lambda b, j: (b, j, 0)
    hbm = pl.BlockSpec(memory_space=pl.ANY)
    return pl.pallas_call(
        functools.partial(_ffn_kernel, mod0=mod0),
        grid=(bsz, seq // TM_FFN),
        in_specs=[pl.BlockSpec((1, TM_FFN, d), blk),
                  pl.BlockSpec((1, N_MOD, d), lambda b, j: (b, 0, 0)),
                  _const_spec((1, d)), _const_spec((1, d)),
                  hbm, hbm, hbm],
        out_specs=pl.BlockSpec((1, TM_FFN, d), blk),
        out_shape=jax.ShapeDtypeStruct((bsz, seq, d), F32),
        scratch_shapes=[pltpu.VMEM((d, d_ff), BF16), pltpu.VMEM((d, d_ff), BF16),
                        pltpu.VMEM((d_ff, d), BF16),
                        pltpu.VMEM((W_SLOTS, d, FF_CHUNK), F32), pltpu.VMEM((W_SLOTS, d, FF_CHUNK), F32),
                        pltpu.VMEM((W_SLOTS, FF_CHUNK, d), F32),
                        pltpu.SemaphoreType.DMA((3, W_SLOTS))],
        compiler_params=pltpu.CompilerParams(dimension_semantics=("arbitrary", "arbitrary"),
                                             vmem_limit_bytes=VMEM_LIMIT),
        name="ffn",
    )(x, mod, g_pre.reshape(1, d), g_post.reshape(1, d), wg, wu, wd)


def _proj_kernel(x_ref, mod_ref, gpre_ref, w_ref, b_ref, q_ref, k_ref, vt_ref):
    shift = mod_ref[0, pl.ds(3, 1), :]
    scale = mod_ref[0, pl.ds(4, 1), :]
    nslab = x_ref.shape[1] // PROJ_SLAB

    def rows(s):
        return slice(s * PROJ_SLAB, (s + 1) * PROJ_SLAB)

    def pre(s):
        return (_rms(x_ref[0, rows(s), :], gpre_ref[...]) * (1.0 + scale) + shift).astype(BF16)

    low = lax.broadcasted_iota(jnp.int32, (1, LANES), 1) < HEAD_DIM
    o_ka, o_va, o_qb = QA, QA + KVA, QA + 2 * KVA
    o_kb, o_vb = o_qb + QB, o_qb + 2 * QB
    h = pre(0)
    for s in range(nslab):
        p = jnp.dot(h, w_ref[...], preferred_element_type=F32) + b_ref[...]
        h_next = pre(s + 1) if s + 1 < nslab else None
        nat = [p[:, t * LANES:(t + 1) * LANES] for t in range(QA // LANES)]
        swp = [pltpu.roll(t, HEAD_DIM, 1) for t in nat]
        qa = []
        for g in range(G_A):
            t0, t1 = g // 2, G_A // 2 + g // 2
            qa.append(jnp.where(low, nat[t0], swp[t1]) if g % 2 == 0 else jnp.where(low, swp[t0], nat[t1]))
        q = jnp.concatenate(qa + [p[:, o_qb:o_kb]], axis=1)
        q_ref[0, rows(s), :] = (q * (HEAD_DIM ** -0.5 * LOG2E)).astype(BF16)
        k_ref[0, rows(s), :] = jnp.concatenate([p[:, o_ka:o_va], p[:, o_kb:o_vb]], axis=1).astype(BF16)
        v = jnp.concatenate([p[:, o_va:o_qb], p[:, o_vb:]], axis=1)
        vt_ref[0, :, rows(s)] = v.T.astype(BF16)
        h = h_next


def _projection(x, mod, g_pre, w_all, b_all):
    bsz, seq, d = x.shape
    nq, nk, nv = QA + QB, KVA + QB, KVA + QB
    assert w_all.shape == (d, nq + nk + nv)
    assert seq % TM_PROJ == 0 and TM_PROJ % PROJ_SLAB == 0
    blk = lambda b, j: (b, j, 0)
    return pl.pallas_call(
        _proj_kernel,
        grid=(bsz, seq // TM_PROJ),
        in_specs=[pl.BlockSpec((1, TM_PROJ, d), blk),
                  pl.BlockSpec((1, N_MOD, d), lambda b, j: (b, 0, 0)),
                  _const_spec((1, d)),
                  _const_spec(w_all.shape), _const_spec(b_all.shape)],
        out_specs=[pl.BlockSpec((1, TM_PROJ, nq), blk),
                   pl.BlockSpec((1, TM_PROJ, nk), blk),
                   pl.BlockSpec((1, nv, TM_PROJ), lambda b, j: (b, 0, j))],
        out_shape=[jax.ShapeDtypeStruct((bsz, seq, nq), BF16),
                   jax.ShapeDtypeStruct((bsz, seq, nk), BF16),
                   jax.ShapeDtypeStruct((bsz, nv, seq), BF16)],
        compiler_params=pltpu.CompilerParams(dimension_semantics=("arbitrary", "arbitrary"),
                                             vmem_limit_bytes=VMEM_LIMIT),
        name="qkv_proj",
    )(x, mod, g_pre.reshape(1, d), w_all, b_all)


def _bias_kernel(rel_ref, ba_ref, bb_ref):
    j = lax.broadcasted_iota(jnp.int32, (WIN_A, G_A * SUB), 0)
    r = lax.broadcasted_iota(jnp.int32, (WIN_A, G_A * SUB), 1)
    g = r // SUB
    i = r % SUB
    for k in range(KV_A):
        slope = jnp.zeros(r.shape, F32)
        for gg in range(G_A):
            slope = jnp.where(g == gg, 2.0 ** (-8.0 * (k * G_A + gg + 1) / H_A), slope)
        for v in range(NV_A):
            off = min(v * SUB, PAD_A)
            back = off // CHUNK + i // CHUNK - j // CHUNK
            bias = -slope * jnp.abs(off + i - j).astype(F32)
            ba_ref[v, k] = jnp.where((back >= 0) & (back <= BACK_A), bias * LOG2E, NEG_INF)

    table = rel_ref[...]
    dcol = lax.broadcasted_iota(jnp.int32, (H_B, ROLL_W), 1)
    d = jnp.where(dcol < SUB, dcol, dcol - ROLL_W)
    idx = jnp.clip(PAD + d, -REL_CLIP, REL_CLIP) + REL_CLIP
    row = jnp.zeros((H_B, ROLL_W), F32)
    for n in range(N_REL):
        row = jnp.where(idx == n, table[:, n:n + 1], row)
    jb = lax.broadcasted_iota(jnp.int32, (WIN_B, SUB), 0)
    ib = lax.broadcasted_iota(jnp.int32, (WIN_B, SUB), 1)
    for h in range(H_B):
        tile = jnp.broadcast_to(row[h:h + 1, :], (EXT_B, ROLL_W))
        tile = pltpu.roll(tile, 0, 1, stride=1, stride_axis=0)[:, :SUB]
        tile = jnp.concatenate([tile, jnp.zeros((PAD + WIN_B - EXT_B, SUB), F32)], axis=0)
        for v in range(NV_B):
            off = min(v * SUB, PAD)
            back = off // CHUNK + ib // CHUNK - jb // CHUNK
            window = tile[PAD - off:PAD - off + WIN_B]
            bb_ref[v, h // 2, :, pl.ds((h % 2) * SUB, SUB)] = jnp.where(
                (back >= 0) & (back <= BACK_B), window * LOG2E, NEG_INF)


def _bias_tiles(rel_bias):
    return pl.pallas_call(
        _bias_kernel,
        out_shape=[jax.ShapeDtypeStruct((NV_A, KV_A, WIN_A, G_A * SUB), F32),
                   jax.ShapeDtypeStruct((NV_B, H_B // 2, WIN_B, 2 * SUB), F32)],
        compiler_params=pltpu.CompilerParams(vmem_limit_bytes=VMEM_LIMIT),
        name="score_bias",
    )(rel_bias)


def _rms_rows(xt, g_col):
    ms = jnp.mean(xt * xt, axis=0, keepdims=True)
    return xt * lax.rsqrt(ms + EPS) * g_col


def _mixer_kernel(sink_ref, q_ref, k_ref, vt_ref, ba_ref, bb_ref, gga_ref, ggb_ref, wo_ref, bo_ref,
                  gpost_ref, mod_ref, x_ref, o_ref):
    i = pl.program_id(1)
    ws_b = [pl.multiple_of(jnp.maximum(i * TQ + u * SUB - PAD, 0), SUB) for u in range(NSUB)]
    ws_a = [pl.multiple_of(jnp.maximum(i * TQ + u * SUB - PAD_A, 0), SUB) for u in range(NSUB)]
    var_b = [jnp.minimum(i * NSUB + u, NV_B - 1) for u in range(NSUB)]
    var_a = [jnp.minimum(i * NSUB + u, NV_A - 1) for u in range(NSUB)]

    low = lax.broadcasted_iota(jnp.int32, (1, LANES), 1) < HEAD_DIM
    halves = (low, jnp.logical_not(low))
    col_g = lax.broadcasted_iota(jnp.int32, (1, G_A * SUB), 1) // SUB
    ones_a = jnp.ones((DEN_ROWS, WIN_A), BF16)
    ones_b = jnp.ones((DEN_ROWS, WIN_B), BF16)

    def scores_a(u, k):
        k_a = k_ref[0, pl.ds(ws_a[u], WIN_A), 0:LANES]
        qs = jnp.concatenate(
            [jnp.where(halves[k], q_ref[0, u * SUB:(u + 1) * SUB, g * LANES:(g + 1) * LANES], 0)
             for g in range(G_A)], axis=0)
        s = lax.dot_general(k_a, qs, _NT, preferred_element_type=F32)
        return s + ba_ref[var_a[u], k]

    def attend_a(u, k, s):
        sink = jnp.zeros((1, G_A * SUB), F32)
        for g in range(G_A):
            sink = jnp.where(col_g == g, sink_ref[k * G_A + g] * LOG2E, sink)
        m = jnp.maximum(jnp.max(s, axis=0, keepdims=True), sink)
        e = jnp.exp2(s - m).astype(BF16)
        vt_a = jnp.concatenate([vt_ref[0, 0:LANES, pl.ds(ws_a[u], WIN_A)], ones_a], axis=0)
        o = jnp.dot(vt_a, e, preferred_element_type=F32)
        den = o[LANES:LANES + 1] + jnp.exp2(sink - m)
        o = o[k * HEAD_DIM:(k + 1) * HEAD_DIM] * (1.0 / den)
        return [o[:, g * SUB:(g + 1) * SUB] for g in range(G_A)]

    def scores_b(u, t):
        q_t = q_ref[0, u * SUB:(u + 1) * SUB, QA + t * LANES:QA + (t + 1) * LANES]
        k_t = k_ref[0, pl.ds(ws_b[u], WIN_B), (1 + t) * LANES:(2 + t) * LANES]
        q2 = jnp.concatenate([jnp.where(halves[0], q_t, 0), jnp.where(halves[1], q_t, 0)], axis=0)
        s = lax.dot_general(k_t, q2, _NT, preferred_element_type=F32)
        return s + bb_ref[var_b[u], t]

    def attend_b(u, t, s):
        m = jnp.max(s, axis=0, keepdims=True)
        e = jnp.exp2(s - m).astype(BF16)
        vt_t = jnp.concatenate([vt_ref[0, (1 + t) * LANES:(2 + t) * LANES, pl.ds(ws_b[u], WIN_B)],
                                ones_b], axis=0)
        o = jnp.dot(vt_t, e, preferred_element_type=F32)
        rden = 1.0 / o[LANES:LANES + 1]
        return [o[:HEAD_DIM, :SUB] * rden[:, :SUB], o[HEAD_DIM:LANES, SUB:] * rden[:, SUB:]]

    groups = []
    for u in range(NSUB):
        groups += [(scores_a, attend_a, u, k) for k in range(KV_A)]
        groups += [(scores_b, attend_b, u, t) for t in range(H_B // 2)]
    per_sub = len(groups) // NSUB
    outs = []

    def finish(first_sub):
        oa_t, ob_t = [], []
        for u in range(first_sub, first_sub + FIN_SUBS):
            sub = outs[u * per_sub:(u + 1) * per_sub]
            oa_t.append(jnp.concatenate([sub[k][g] for k in range(KV_A) for g in range(G_A)], axis=0))
            ob_t.append(jnp.concatenate([part for pair in sub[KV_A:] for part in pair], axis=0))
        oa_t = jnp.concatenate(oa_t, axis=1)
        ob_t = jnp.concatenate(ob_t, axis=1)
        y_t = jnp.concatenate([_rms_rows(oa_t, gga_ref[...]), _rms_rows(ob_t, ggb_ref[...])], axis=0)
        z = lax.dot_general(y_t.astype(BF16), wo_ref[...], _TN, preferred_element_type=F32) + bo_ref[...]
        gate = mod_ref[0, pl.ds(5, 1), :]
        r = slice(first_sub * SUB, (first_sub + FIN_SUBS) * SUB)
        o_ref[0, r, :] = x_ref[0, r, :] + gate * _rms(z, gpost_ref[...])

    pending = [grp[0](*grp[2:]) for grp in groups[:SCORE_AHEAD]]
    for n, (_, attend, u, idx) in enumerate(groups):
        if n + SCORE_AHEAD < len(groups):
            nxt = groups[n + SCORE_AHEAD]
            pending.append(nxt[0](*nxt[2:]))
        outs.append(attend(u, idx, pending.pop(0)))
        done_groups = n + 1 - FIN_LAG
        if 0 < done_groups < len(groups) and done_groups % (FIN_SUBS * per_sub) == 0:
            finish(done_groups // per_sub - FIN_SUBS)
    finish(NSUB - FIN_SUBS)


def _mixer(x, mod, q, k, vt, bias_a, bias_b, sinks, gga, ggb, wo, bo, g_post):
    bsz, seq, d = x.shape
    assert seq % TQ == 0 and TQ % SUB == 0
    tile = lambda b, i: (b, i, 0)
    whole = lambda b, i: (b, 0, 0)
    return pl.pallas_call(
        _mixer_kernel,
        grid=(bsz, seq // TQ),
        in_specs=[pl.BlockSpec(memory_space=pltpu.SMEM),
                  pl.BlockSpec((1, TQ, q.shape[2]), tile),
                  pl.BlockSpec((1,) + k.shape[1:], whole),
                  pl.BlockSpec((1,) + vt.shape[1:], whole),
                  _const_spec(bias_a.shape), _const_spec(bias_b.shape),
                  _const_spec((QA, 1)), _const_spec((QB, 1)),
                  _const_spec(wo.shape), _const_spec((1, d)), _const_spec((1, d)),
                  pl.BlockSpec((1, N_MOD, d), whole),
                  pl.BlockSpec((1, TQ, d), tile)],
        out_specs=pl.BlockSpec((1, TQ, d), tile),
        out_shape=jax.ShapeDtypeStruct((bsz, seq, d), F32),
        compiler_params=pltpu.CompilerParams(dimension_semantics=("arbitrary", "arbitrary"),
                                             vmem_limit_bytes=VMEM_LIMIT),
        name="mixer",
    )(sinks, q, k, vt, bias_a, bias_b, gga.reshape(QA, 1), ggb.reshape(QB, 1), wo,
      bo.reshape(1, d), g_post.reshape(1, d), mod, x)


def kernel(x, c, w_ada, b_ada, g_pre_ffn1, w_gate1, w_up1, w_down1, g_post_ffn1, g_pre_mix, w_in, b_in, sinks_a, rel_bias_b, g_grp_a, g_grp_b, w_out, b_out, g_post_mix, g_pre_ffn2, w_gate2, w_up2, w_down2, g_post_ffn2):
    bsz, seq, d = x.shape
    depth = w_ada.shape[0]
    for l in range(depth):
        mod = _modulation(c, w_ada[l], b_ada[l]).reshape(bsz, N_MOD, d)

        x = _ffn(x, mod, g_pre_ffn1[l], g_post_ffn1[l], w_gate1[l], w_up1[l], w_down1[l], mod0=0)

        q, k, vt = _projection(x, mod, g_pre_mix[l], w_in[l].astype(BF16), b_in[l].reshape(1, -1))

        bias_a, bias_b = _bias_tiles(rel_bias_b[l])
        x = _mixer(x, mod, q, k, vt, bias_a, bias_b, sinks_a[l], g_grp_a[l], g_grp_b[l],
                   w_out[l].astype(BF16), b_out[l], g_post_mix[l])

        x = _ffn(x, mod, g_pre_ffn2[l], g_post_ffn2[l], w_gate2[l], w_up2[l], w_down2[l], mod0=6)
    return x
```

```python
import functools
import math

import jax
import jax.numpy as jnp
from jax import lax
from jax.experimental import pallas as pl
from jax.experimental.pallas import tpu as pltpu

F32 = jnp.float32
BF16 = jnp.bfloat16

CHUNK = 64
HEAD_DIM = 64
EPS = 1e-6
NEG_INF = -1e30
H_A, KV_A = 8, 2
G_A = H_A // KV_A
BACK_A = 2
H_B = 8
BACK_B = 8
REL_CLIP = 128
N_REL = 2 * REL_CLIP + 1
QA = H_A * HEAD_DIM
KVA = KV_A * HEAD_DIM
QB = H_B * HEAD_DIM
N_MOD = 9
LOG2E = math.log2(math.e)

LANES = 128
TQ = 512
SUB = LANES
NSUB = TQ // SUB
SCORE_AHEAD = 3
FIN_SUBS = 2
FIN_LAG = 3
PAD = BACK_B * CHUNK
PAD_A = BACK_A * CHUNK
WIN_A = SUB + PAD_A
WIN_B = SUB + PAD
NV_A = PAD_A // SUB + 1
NV_B = PAD // SUB + 1
EXT_B = PAD + SUB + CHUNK
DEN_ROWS = 16
ROLL_W = 1024
TM_FFN = 1024
FFN_SLAB = 512
W_ROWS = 256
W_SLOTS = 3
TM_PROJ = 1024
PROJ_SLAB = 512
TN_MOD = 1536
FF_CHUNK = 256
VMEM_LIMIT = 56 * 1024 * 1024

_NT = (((1,), (1,)), ((), ()))
_TN = (((0,), (0,)), ((), ()))


def _rms(x, g):
    ms = jnp.mean(x * x, axis=-1, keepdims=True)
    return x * lax.rsqrt(ms + EPS) * g


def _const_spec(shape):
    nd = len(shape)
    return pl.BlockSpec(shape, lambda *_: (0,) * nd, pipeline_mode=pl.Buffered(1))


def _mod_kernel(c_ref, w_ref, b_ref, o_ref):
    c = c_ref[...]
    a = (c * jax.nn.sigmoid(c)).astype(BF16)
    o_ref[...] = jnp.dot(a, w_ref[...].astype(BF16), preferred_element_type=F32) + b_ref[...]


def _modulation(c, w_ada, b_ada):
    bsz, d = c.shape
    n = w_ada.shape[1]
    assert n % TN_MOD == 0
    return pl.pallas_call(
        _mod_kernel,
        grid=(n // TN_MOD,),
        in_specs=[pl.BlockSpec((bsz, d), lambda j: (0, 0)),
                  pl.BlockSpec((d, TN_MOD), lambda j: (0, j)),
                  pl.BlockSpec((1, TN_MOD), lambda j: (0, j))],
        out_specs=pl.BlockSpec((bsz, TN_MOD), lambda j: (0, j)),
        out_shape=jax.ShapeDtypeStruct((bsz, n), F32),
        compiler_params=pltpu.CompilerParams(dimension_semantics=("arbitrary",),
                                             vmem_limit_bytes=VMEM_LIMIT),
        name="adaln_mod",
    )(c, w_ada, b_ada.reshape(1, n))


def _load_bf16(src_hbm, dst_ref, stage, sem):
    ncol = src_hbm.shape[1]
    nblk = src_hbm.shape[0] // W_ROWS
    nslot = stage.shape[0]

    def copy(n):
        return pltpu.make_async_copy(src_hbm.at[pl.ds(n * W_ROWS, W_ROWS), :],
                                     stage.at[n % nslot, :, pl.ds(0, ncol)], sem.at[n % nslot])

    for n in range(min(nslot - 1, nblk)):
        copy(n).start()
    for n in range(nblk):
        if n + nslot - 1 < nblk:
            copy(n + nslot - 1).start()
        copy(n).wait()
        dst_ref[n * W_ROWS:(n + 1) * W_ROWS, :] = stage[n % nslot, :, 0:ncol].astype(dst_ref.dtype)


def _ffn_kernel(x_ref, mod_ref, gpre_ref, gpost_ref, wg_hbm, wu_hbm, wd_hbm, o_ref,
                wg_ref, wu_ref, wd_ref, stage, sem, *, mod0):
    @pl.when((pl.program_id(0) == 0) & (pl.program_id(1) == 0))
    def _():
        _load_bf16(wg_hbm, wg_ref, stage, sem)
        _load_bf16(wu_hbm, wu_ref, stage, sem)
        _load_bf16(wd_hbm, wd_ref, stage, sem)

    shift = mod_ref[0, pl.ds(mod0, 1), :]
    scale = mod_ref[0, pl.ds(mod0 + 1, 1), :]
    gate = mod_ref[0, pl.ds(mod0 + 2, 1), :]
    d_ff = wg_ref.shape[1]
    nslab = x_ref.shape[1] // FFN_SLAB
    nchunk = d_ff // FF_CHUNK

    def rows(s):
        return slice(s * FFN_SLAB, (s + 1) * FFN_SLAB)

    def pre(s):
        return (_rms(x_ref[0, rows(s), :], gpre_ref[...]) * (1.0 + scale) + shift).astype(BF16)

    def post(s, acc):
        o_ref[0, rows(s), :] = x_ref[0, rows(s), :] + (0.5 * gate) * _rms(acc, gpost_ref[...])

    h = pre(0)
    done = None
    for s in range(nslab):
        acc = jnp.zeros((FFN_SLAB, x_ref.shape[2]), F32)
        h_next = None
        for c in range(nchunk):
            if c == nchunk // 4 and done is not None:
                post(s - 1, done)
            if c == nchunk // 2 and s + 1 < nslab:
                h_next = pre(s + 1)
            f0 = c * FF_CHUNK
            g = jnp.dot(h, wg_ref[:, f0:f0 + FF_CHUNK], preferred_element_type=F32)
            u = jnp.dot(h, wu_ref[:, f0:f0 + FF_CHUNK], preferred_element_type=F32)
            a = (g * jax.nn.sigmoid(g) * u).astype(BF16)
            acc = acc + jnp.dot(a, wd_ref[f0:f0 + FF_CHUNK, :], preferred_element_type=F32)
        done, h = acc, h_next
    post(nslab - 1, done)


def _ffn(x, mod, g_pre, g_post, wg, wu, wd, *, mod0):
    bsz, seq, d = x.shape
    d_ff = wg.shape[1]
    assert seq % TM_FFN == 0 and TM_FFN % FFN_SLAB == 0 and d_ff % FF_CHUNK == 0
    assert d % W_ROWS == 0 and d_ff % W_ROWS == 0
    blk = lambda b, j: (b, j, 0)
    hbm = pl.BlockSpec(memory_space=pl.ANY)
    return pl.pallas_call(
        functools.partial(_ffn_kernel, mod0=mod0),
        grid=(bsz, seq // TM_FFN),
        in_specs=[pl.BlockSpec((1, TM_FFN, d), blk),
                  pl.BlockSpec((1, N_MOD, d), lambda b, j: (b, 0, 0)),
                  _const_spec((1, d)), _const_spec((1, d)),
                  hbm, hbm, hbm],
        out_specs=pl.BlockSpec((1, TM_FFN, d), blk),
        out_shape=jax.ShapeDtypeStruct((bsz, seq, d), F32),
        scratch_shapes=[pltpu.VMEM((d, d_ff), BF16), pltpu.VMEM((d, d_ff), BF16),
                        pltpu.VMEM((d_ff, d), BF16),
                        pltpu.VMEM((W_SLOTS, W_ROWS, max(d, d_ff)), F32),
                        pltpu.SemaphoreType.DMA((W_SLOTS,))],
        compiler_params=pltpu.CompilerParams(dimension_semantics=("arbitrary", "arbitrary"),
                                             vmem_limit_bytes=VMEM_LIMIT),
        name="ffn",
    )(x, mod, g_pre.reshape(1, d), g_post.reshape(1, d), wg, wu, wd)


def _proj_kernel(x_ref, mod_ref, gpre_ref, w_ref, b_ref, q_ref, k_ref, vt_ref):
    shift = mod_ref[0, pl.ds(3, 1), :]
    scale = mod_ref[0, pl.ds(4, 1), :]
    nslab = x_ref.shape[1] // PROJ_SLAB

    def rows(s):
        return slice(s * PROJ_SLAB, (s + 1) * PROJ_SLAB)

    def pre(s):
        return (_rms(x_ref[0, rows(s), :], gpre_ref[...]) * (1.0 + scale) + shift).astype(BF16)

    low = lax.broadcasted_iota(jnp.int32, (1, LANES), 1) < HEAD_DIM
    o_ka, o_va, o_qb = QA, QA + KVA, QA + 2 * KVA
    o_kb, o_vb = o_qb + QB, o_qb + 2 * QB
    h = pre(0)
    for s in range(nslab):
        p = jnp.dot(h, w_ref[...], preferred_element_type=F32) + b_ref[...]
        h_next = pre(s + 1) if s + 1 < nslab else None
        nat = [p[:, t * LANES:(t + 1) * LANES] for t in range(QA // LANES)]
        swp = [pltpu.roll(t, HEAD_DIM, 1) for t in nat]
        qa = []
        for g in range(G_A):
            t0, t1 = g // 2, G_A // 2 + g // 2
            qa.append(jnp.where(low, nat[t0], swp[t1]) if g % 2 == 0 else jnp.where(low, swp[t0], nat[t1]))
        q = jnp.concatenate(qa + [p[:, o_qb:o_kb]], axis=1)
        q_ref[0, rows(s), :] = (q * (HEAD_DIM ** -0.5 * LOG2E)).astype(BF16)
        k_ref[0, rows(s), :] = jnp.concatenate([p[:, o_ka:o_va], p[:, o_kb:o_vb]], axis=1).astype(BF16)
        v = jnp.concatenate([p[:, o_va:o_qb], p[:, o_vb:]], axis=1)
        vt_ref[0, :, rows(s)] = v.T.astype(BF16)
        h = h_next


def _projection(x, mod, g_pre, w_all, b_all):
    bsz, seq, d = x.shape
    nq, nk, nv = QA + QB, KVA + QB, KVA + QB
    assert w_all.shape == (d, nq + nk + nv)
    assert seq % TM_PROJ == 0 and TM_PROJ % PROJ_SLAB == 0
    blk = lambda b, j: (b, j, 0)
    return pl.pallas_call(
        _proj_kernel,
        grid=(bsz, seq // TM_PROJ),
        in_specs=[pl.BlockSpec((1, TM_PROJ, d), blk),
                  pl.BlockSpec((1, N_MOD, d), lambda b, j: (b, 0, 0)),
                  _const_spec((1, d)),
                  _const_spec(w_all.shape), _const_spec(b_all.shape)],
        out_specs=[pl.BlockSpec((1, TM_PROJ, nq), blk),
                   pl.BlockSpec((1, TM_PROJ, nk), blk),
                   pl.BlockSpec((1, nv, TM_PROJ), lambda b, j: (b, 0, j))],
        out_shape=[jax.ShapeDtypeStruct((bsz, seq, nq), BF16),
                   jax.ShapeDtypeStruct((bsz, seq, nk), BF16),
                   jax.ShapeDtypeStruct((bsz, nv, seq), BF16)],
        compiler_params=pltpu.CompilerParams(dimension_semantics=("arbitrary", "arbitrary"),
                                             vmem_limit_bytes=VMEM_LIMIT),
        name="qkv_proj",
    )(x, mod, g_pre.reshape(1, d), w_all, b_all)


def _bias_kernel(rel_ref, ba_ref, bb_ref):
    j = lax.broadcasted_iota(jnp.int32, (WIN_A, G_A * SUB), 0)
    r = lax.broadcasted_iota(jnp.int32, (WIN_A, G_A * SUB), 1)
    g = r // SUB
    i = r % SUB
    for k in range(KV_A):
        slope = jnp.zeros(r.shape, F32)
        for gg in range(G_A):
            slope = jnp.where(g == gg, 2.0 ** (-8.0 * (k * G_A + gg + 1) / H_A), slope)
        for v in range(NV_A):
            off = min(v * SUB, PAD_A)
            back = off // CHUNK + i // CHUNK - j // CHUNK
            bias = -slope * jnp.abs(off + i - j).astype(F32)
            ba_ref[v, k] = jnp.where((back >= 0) & (back <= BACK_A), bias * LOG2E, NEG_INF)

    table = rel_ref[...]
    dcol = lax.broadcasted_iota(jnp.int32, (H_B, ROLL_W), 1)
    d = jnp.where(dcol < SUB, dcol, dcol - ROLL_W)
    idx = jnp.clip(PAD + d, -REL_CLIP, REL_CLIP) + REL_CLIP
    row = jnp.zeros((H_B, ROLL_W), F32)
    for n in range(N_REL):
        row = jnp.where(idx == n, table[:, n:n + 1], row)
    jb = lax.broadcasted_iota(jnp.int32, (WIN_B, SUB), 0)
    ib = lax.broadcasted_iota(jnp.int32, (WIN_B, SUB), 1)
    for h in range(H_B):
        tile = jnp.broadcast_to(row[h:h + 1, :], (EXT_B, ROLL_W))
        tile = pltpu.roll(tile, 0, 1, stride=1, stride_axis=0)[:, :SUB]
        tile = jnp.concatenate([tile, jnp.zeros((PAD + WIN_B - EXT_B, SUB), F32)], axis=0)
        for v in range(NV_B):
            off = min(v * SUB, PAD)
            back = off // CHUNK + ib // CHUNK - jb // CHUNK
            window = tile[PAD - off:PAD - off + WIN_B]
            bb_ref[v, h // 2, :, pl.ds((h % 2) * SUB, SUB)] = jnp.where(
                (back >= 0) & (back <= BACK_B), window * LOG2E, NEG_INF)


def _bias_tiles(rel_bias):
    return pl.pallas_call(
        _bias_kernel,
        out_shape=[jax.ShapeDtypeStruct((NV_A, KV_A, WIN_A, G_A * SUB), F32),
                   jax.ShapeDtypeStruct((NV_B, H_B // 2, WIN_B, 2 * SUB), F32)],
        compiler_params=pltpu.CompilerParams(vmem_limit_bytes=VMEM_LIMIT),
        name="score_bias",
    )(rel_bias)


def _rms_rows(xt, g_col):
    ms = jnp.mean(xt * xt, axis=0, keepdims=True)
    return xt * lax.rsqrt(ms + EPS) * g_col


def _mixer_kernel(sink_ref, q_ref, k_ref, vt_ref, ba_ref, bb_ref, gga_ref, ggb_ref, wo_ref, bo_ref,
                  gpost_ref, mod_ref, x_ref, o_ref):
    i = pl.program_id(1)
    ws_b = [pl.multiple_of(jnp.maximum(i * TQ + u * SUB - PAD, 0), SUB) for u in range(NSUB)]
    ws_a = [pl.multiple_of(jnp.maximum(i * TQ + u * SUB - PAD_A, 0), SUB) for u in range(NSUB)]
    var_b = [jnp.minimum(i * NSUB + u, NV_B - 1) for u in range(NSUB)]
    var_a = [jnp.minimum(i * NSUB + u, NV_A - 1) for u in range(NSUB)]

    low = lax.broadcasted_iota(jnp.int32, (1, LANES), 1) < HEAD_DIM
    halves = (low, jnp.logical_not(low))
    col_g = lax.broadcasted_iota(jnp.int32, (1, G_A * SUB), 1) // SUB
    ones_a = jnp.ones((DEN_ROWS, WIN_A), BF16)
    ones_b = jnp.ones((DEN_ROWS, WIN_B), BF16)

    def scores_a(u, k):
        k_a = k_ref[0, pl.ds(ws_a[u], WIN_A), 0:LANES]
        qs = jnp.concatenate(
            [jnp.where(halves[k], q_ref[0, u * SUB:(u + 1) * SUB, g * LANES:(g + 1) * LANES], 0)
             for g in range(G_A)], axis=0)
        s = lax.dot_general(k_a, qs, _NT, preferred_element_type=F32)
        return s + ba_ref[var_a[u], k]

    def attend_a(u, k, s):
        sink = jnp.zeros((1, G_A * SUB), F32)
        for g in range(G_A):
            sink = jnp.where(col_g == g, sink_ref[k * G_A + g] * LOG2E, sink)
        m = jnp.maximum(jnp.max(s, axis=0, keepdims=True), sink)
        e = jnp.exp2(s - m).astype(BF16)
        vt_a = jnp.concatenate([vt_ref[0, 0:LANES, pl.ds(ws_a[u], WIN_A)], ones_a], axis=0)
        o = jnp.dot(vt_a, e, preferred_element_type=F32)
        den = o[LANES:LANES + 1] + jnp.exp2(sink - m)
        o = o[k * HEAD_DIM:(k + 1) * HEAD_DIM] * (1.0 / den)
        return [o[:, g * SUB:(g + 1) * SUB] for g in range(G_A)]

    def scores_b(u, t):
        q_t = q_ref[0, u * SUB:(u + 1) * SUB, QA + t * LANES:QA + (t + 1) * LANES]
        k_t = k_ref[0, pl.ds(ws_b[u], WIN_B), (1 + t) * LANES:(2 + t) * LANES]
        q2 = jnp.concatenate([jnp.where(halves[0], q_t, 0), jnp.where(halves[1], q_t, 0)], axis=0)
        s = lax.dot_general(k_t, q2, _NT, preferred_element_type=F32)
        return s + bb_ref[var_b[u], t]

    def attend_b(u, t, s):
        m = jnp.max(s, axis=0, keepdims=True)
        e = jnp.exp2(s - m).astype(BF16)
        vt_t = jnp.concatenate([vt_ref[0, (1 + t) * LANES:(2 + t) * LANES, pl.ds(ws_b[u], WIN_B)],
                                ones_b], axis=0)
        o = jnp.dot(vt_t, e, preferred_element_type=F32)
        rden = 1.0 / o[LANES:LANES + 1]
        return [o[:HEAD_DIM, :SUB] * rden[:, :SUB], o[HEAD_DIM:LANES, SUB:] * rden[:, SUB:]]

    groups = []
    for u in range(NSUB):
        groups += [(scores_a, attend_a, u, k) for k in range(KV_A)]
        groups += [(scores_b, attend_b, u, t) for t in range(H_B // 2)]
    per_sub = len(groups) // NSUB
    outs = []

    def finish(first_sub):
        oa_t, ob_t = [], []
        for u in range(first_sub, first_sub + FIN_SUBS):
            sub = outs[u * per_sub:(u + 1) * per_sub]
            oa_t.append(jnp.concatenate([sub[k][g] for k in range(KV_A) for g in range(G_A)], axis=0))
            ob_t.append(jnp.concatenate([part for pair in sub[KV_A:] for part in pair], axis=0))
        oa_t = jnp.concatenate(oa_t, axis=1)
        ob_t = jnp.concatenate(ob_t, axis=1)
        y_t = jnp.concatenate([_rms_rows(oa_t, gga_ref[...]), _rms_rows(ob_t, ggb_ref[...])], axis=0)
        z = lax.dot_general(y_t.astype(BF16), wo_ref[...], _TN, preferred_element_type=F32) + bo_ref[...]
        gate = mod_ref[0, pl.ds(5, 1), :]
        r = slice(first_sub * SUB, (first_sub + FIN_SUBS) * SUB)
        o_ref[0, r, :] = x_ref[0, r, :] + gate * _rms(z, gpost_ref[...])

    pending = [grp[0](*grp[2:]) for grp in groups[:SCORE_AHEAD]]
    for n, (_, attend, u, idx) in enumerate(groups):
        if n + SCORE_AHEAD < len(groups):
            nxt = groups[n + SCORE_AHEAD]
            pending.append(nxt[0](*nxt[2:]))
        outs.append(attend(u, idx, pending.pop(0)))
        done_groups = n + 1 - FIN_LAG
        if 0 < done_groups < len(groups) and done_groups % (FIN_SUBS * per_sub) == 0:
            finish(done_groups // per_sub - FIN_SUBS)
    finish(NSUB - FIN_SUBS)


def _mixer(x, mod, q, k, vt, bias_a, bias_b, sinks, gga, ggb, wo, bo, g_post):
    bsz, seq, d = x.shape
    assert seq % TQ == 0 and TQ % SUB == 0
    tile = lambda b, i: (b, i, 0)
    whole = lambda b, i: (b, 0, 0)
    return pl.pallas_call(
        _mixer_kernel,
        grid=(bsz, seq // TQ),
        in_specs=[pl.BlockSpec(memory_space=pltpu.SMEM),
                  pl.BlockSpec((1, TQ, q.shape[2]), tile),
                  pl.BlockSpec((1,) + k.shape[1:], whole),
                  pl.BlockSpec((1,) + vt.shape[1:], whole),
                  _const_spec(bias_a.shape), _const_spec(bias_b.shape),
                  _const_spec((QA, 1)), _const_spec((QB, 1)),
                  _const_spec(wo.shape), _const_spec((1, d)), _const_spec((1, d)),
                  pl.BlockSpec((1, N_MOD, d), whole),
                  pl.BlockSpec((1, TQ, d), tile)],
        out_specs=pl.BlockSpec((1, TQ, d), tile),
        out_shape=jax.ShapeDtypeStruct((bsz, seq, d), F32),
        compiler_params=pltpu.CompilerParams(dimension_semantics=("arbitrary", "arbitrary"),
                                             vmem_limit_bytes=VMEM_LIMIT),
        name="mixer",
    )(sinks, q, k, vt, bias_a, bias_b, gga.reshape(QA, 1), ggb.reshape(QB, 1), wo,
      bo.reshape(1, d), g_post.reshape(1, d), mod, x)


def kernel(x, c, w_ada, b_ada, g_pre_ffn1, w_gate1, w_up1, w_down1, g_post_ffn1, g_pre_mix, w_in, b_in, sinks_a, rel_bias_b, g_grp_a, g_grp_b, w_out, b_out, g_post_mix, g_pre_ffn2, w_gate2, w_up2, w_down2, g_post_ffn2):
    bsz, seq, d = x.shape
    depth = w_ada.shape[0]
    for l in range(depth):
        mod = _modulation(c, w_ada[l], b_ada[l]).reshape(bsz, N_MOD, d)

        x = _ffn(x, mod, g_pre_ffn1[l], g_post_ffn1[l], w_gate1[l], w_up1[l], w_down1[l], mod0=0)

        q, k, vt = _projection(x, mod, g_pre_mix[l], w_in[l].astype(BF16), b_in[l].reshape(1, -1))

        bias_a, bias_b = _bias_tiles(rel_bias_b[l])
        x = _mixer(x, mod, q, k, vt, bias_a, bias_b, sinks_a[l], g_grp_a[l], g_grp_b[l],
                   w_out[l].astype(BF16), b_out[l], g_post_mix[l])

        x = _ffn(x, mod, g_pre_ffn2[l], g_post_ffn2[l], w_gate2[l], w_up2[l], w_down2[l], mod0=6)
    return x
```

```python
import functools
import math

import jax
import jax.numpy as jnp
from jax import lax
from jax.experimental import pallas as pl
from jax.experimental.pallas import tpu as pltpu

F32 = jnp.float32
BF16 = jnp.bfloat16

CHUNK = 64
HEAD_DIM = 64
EPS = 1e-6
NEG_INF = -1e30
H_A, KV_A = 8, 2
G_A = H_A // KV_A
BACK_A = 2
H_B = 8
BACK_B = 8
REL_CLIP = 128
N_REL = 2 * REL_CLIP + 1
QA = H_A * HEAD_DIM
KVA = KV_A * HEAD_DIM
QB = H_B * HEAD_DIM
N_MOD = 9
LOG2E = math.log2(math.e)

LANES = 128
TQ = 512
SUB = LANES
NSUB = TQ // SUB
SCORE_AHEAD = 3
FIN_SUBS = 2
FIN_LAG = 3
PAD = BACK_B * CHUNK
PAD_A = BACK_A * CHUNK
WIN_A = SUB + PAD_A
WIN_B = SUB + PAD
NV_A = PAD_A // SUB + 1
NV_B = PAD // SUB + 1
EXT_B = PAD + SUB + CHUNK
DEN_ROWS = 16
ROLL_W = 1024
TM_FFN = 1024
FFN_SLAB = 512
W_ROWS = 256
W_SLOTS = 3
TM_PROJ = 1024
PROJ_SLAB = 512
TN_MOD = 1536
FF_CHUNK = 256
VMEM_LIMIT = 56 * 1024 * 1024

_NT = (((1,), (1,)), ((), ()))
_TN = (((0,), (0,)), ((), ()))


def _rms(x, g):
    ms = jnp.mean(x * x, axis=-1, keepdims=True)
    return x * lax.rsqrt(ms + EPS) * g


def _const_spec(shape):
    nd = len(shape)
    return pl.BlockSpec(shape, lambda *_: (0,) * nd, pipeline_mode=pl.Buffered(1))


def _mod_kernel(c_ref, w_ref, b_ref, o_ref):
    c = c_ref[...]
    a = (c * jax.nn.sigmoid(c)).astype(BF16)
    o_ref[...] = jnp.dot(a, w_ref[...].astype(BF16), preferred_element_type=F32) + b_ref[...]


def _modulation(c, w_ada, b_ada):
    bsz, d = c.shape
    n = w_ada.shape[1]
    assert n % TN_MOD == 0
    return pl.pallas_call(
        _mod_kernel,
        grid=(n // TN_MOD,),
        in_specs=[pl.BlockSpec((bsz, d), lambda j: (0, 0)),
                  pl.BlockSpec((d, TN_MOD), lambda j: (0, j)),
                  pl.BlockSpec((1, TN_MOD), lambda j: (0, j))],
        out_specs=pl.BlockSpec((bsz, TN_MOD), lambda j: (0, j)),
        out_shape=jax.ShapeDtypeStruct((bsz, n), F32),
        compiler_params=pltpu.CompilerParams(dimension_semantics=("arbitrary",),
                                             vmem_limit_bytes=VMEM_LIMIT),
        name="adaln_mod",
    )(c, w_ada, b_ada.reshape(1, n))


def _load_bf16(src_hbm, dst_ref, stage, sem):
    ncol = src_hbm.shape[1]
    nblk = src_hbm.shape[0] // W_ROWS
    nslot = stage.shape[0]

    def copy(n):
        return pltpu.make_async_copy(src_hbm.at[pl.ds(n * W_ROWS, W_ROWS), :],
                                     stage.at[n % nslot, :, pl.ds(0, ncol)], sem.at[n % nslot])

    for n in range(min(nslot - 1, nblk)):
        copy(n).start()
    for n in range(nblk):
        if n + nslot - 1 < nblk:
            copy(n + nslot - 1).start()
        copy(n).wait()
        dst_ref[n * W_ROWS:(n + 1) * W_ROWS, :] = stage[n % nslot, :, 0:ncol].astype(dst_ref.dtype)


def _ffn_kernel(x_ref, mod_ref, gpre_ref, gpost_ref, wg_hbm, wu_hbm, wd_hbm, o_ref,
                wg_ref, wu_ref, wd_ref, stage, sem, *, mod0):
    @pl.when((pl.program_id(0) == 0) & (pl.program_id(1) == 0))
    def _():
        _load_bf16(wg_hbm, wg_ref, stage, sem)
        _load_bf16(wu_hbm, wu_ref, stage, sem)
        _load_bf16(wd_hbm, wd_ref, stage, sem)

    shift = mod_ref[0, pl.ds(mod0, 1), :]
    gain_in = gpre_ref[...] * (1.0 + mod_ref[0, pl.ds(mod0 + 1, 1), :])
    gain_out = (0.5 * mod_ref[0, pl.ds(mod0 + 2, 1), :]) * gpost_ref[...]
    d_ff = wg_ref.shape[1]
    nslab = x_ref.shape[1] // FFN_SLAB
    nchunk = d_ff // FF_CHUNK

    def rows(s):
        return slice(s * FFN_SLAB, (s + 1) * FFN_SLAB)

    def pre(s):
        return (_rms(x_ref[0, rows(s), :], gain_in) + shift).astype(BF16)

    def post(s, acc):
        o_ref[0, rows(s), :] = x_ref[0, rows(s), :] + _rms(acc, gain_out)

    h = pre(0)
    done = None
    for s in range(nslab):
        acc = jnp.zeros((FFN_SLAB, x_ref.shape[2]), F32)
        h_next = None
        for c in range(nchunk):
            if c == nchunk // 4 and done is not None:
                post(s - 1, done)
            if c == nchunk // 2 and s + 1 < nslab:
                h_next = pre(s + 1)
            f0 = c * FF_CHUNK
            g = jnp.dot(h, wg_ref[:, f0:f0 + FF_CHUNK], preferred_element_type=F32)
            u = jnp.dot(h, wu_ref[:, f0:f0 + FF_CHUNK], preferred_element_type=F32)
            a = (g * jax.nn.sigmoid(g) * u).astype(BF16)
            acc = acc + jnp.dot(a, wd_ref[f0:f0 + FF_CHUNK, :], preferred_element_type=F32)
        done, h = acc, h_next
    post(nslab - 1, done)


def _ffn(x, mod, g_pre, g_post, wg, wu, wd, *, mod0):
    bsz, seq, d = x.shape
    d_ff = wg.shape[1]
    assert seq % TM_FFN == 0 and TM_FFN % FFN_SLAB == 0 and d_ff % FF_CHUNK == 0
    assert d % W_ROWS == 0 and d_ff % W_ROWS == 0
    blk = lambda b, j: (b, j, 0)
    hbm = pl.BlockSpec(memory_space=pl.ANY)
    return pl.pallas_call(
        functools.partial(_ffn_kernel, mod0=mod0),
        grid=(bsz, seq // TM_FFN),
        in_specs=[pl.BlockSpec((1, TM_FFN, d), blk),
                  pl.BlockSpec((1, N_MOD, d), lambda b, j: (b, 0, 0)),
                  _const_spec((1, d)), _const_spec((1, d)),
                  hbm, hbm, hbm],
        out_specs=pl.BlockSpec((1, TM_FFN, d), blk),
        out_shape=jax.ShapeDtypeStruct((bsz, seq, d), F32),
        scratch_shapes=[pltpu.VMEM((d, d_ff), BF16), pltpu.VMEM((d, d_ff), BF16),
                        pltpu.VMEM((d_ff, d), BF16),
                        pltpu.VMEM((W_SLOTS, W_ROWS, max(d, d_ff)), F32),
                        pltpu.SemaphoreType.DMA((W_SLOTS,))],
        compiler_params=pltpu.CompilerParams(dimension_semantics=("arbitrary", "arbitrary"),
                                             vmem_limit_bytes=VMEM_LIMIT),
        name="ffn",
    )(x, mod, g_pre.reshape(1, d), g_post.reshape(1, d), wg, wu, wd)


def _proj_kernel(x_ref, mod_ref, gpre_ref, w_ref, b_ref, q_ref, k_ref, vt_ref):
    shift = mod_ref[0, pl.ds(3, 1), :]
    gain_in = gpre_ref[...] * (1.0 + mod_ref[0, pl.ds(4, 1), :])
    nslab = x_ref.shape[1] // PROJ_SLAB

    def rows(s):
        return slice(s * PROJ_SLAB, (s + 1) * PROJ_SLAB)

    def pre(s):
        return (_rms(x_ref[0, rows(s), :], gain_in) + shift).astype(BF16)

    low = lax.broadcasted_iota(jnp.int32, (1, LANES), 1) < HEAD_DIM
    o_ka, o_va, o_qb = QA, QA + KVA, QA + 2 * KVA
    o_kb, o_vb = o_qb + QB, o_qb + 2 * QB
    h = pre(0)
    for s in range(nslab):
        p = jnp.dot(h, w_ref[...], preferred_element_type=F32) + b_ref[...]
        h_next = pre(s + 1) if s + 1 < nslab else None
        nat = [p[:, t * LANES:(t + 1) * LANES] for t in range(QA // LANES)]
        swp = [pltpu.roll(t, HEAD_DIM, 1) for t in nat]
        qa = []
        for g in range(G_A):
            t0, t1 = g // 2, G_A // 2 + g // 2
            qa.append(jnp.where(low, nat[t0], swp[t1]) if g % 2 == 0 else jnp.where(low, swp[t0], nat[t1]))
        q = jnp.concatenate(qa + [p[:, o_qb:o_kb]], axis=1)
        q_ref[0, rows(s), :] = (q * (HEAD_DIM ** -0.5 * LOG2E)).astype(BF16)
        k_ref[0, rows(s), :] = jnp.concatenate([p[:, o_ka:o_va], p[:, o_kb:o_vb]], axis=1).astype(BF16)
        v = jnp.concatenate([p[:, o_va:o_qb], p[:, o_vb:]], axis=1)
        vt_ref[0, :, rows(s)] = v.T.astype(BF16)
        h = h_next


def _projection(x, mod, g_pre, w_all, b_all):
    bsz, seq, d = x.shape
    nq, nk, nv = QA + QB, KVA + QB, KVA + QB
    assert w_all.shape == (d, nq + nk + nv)
    assert seq % TM_PROJ == 0 and TM_PROJ % PROJ_SLAB == 0
    blk = lambda b, j: (b, j, 0)
    return pl.pallas_call(
        _proj_kernel,
        grid=(bsz, seq // TM_PROJ),
        in_specs=[pl.BlockSpec((1, TM_PROJ, d), blk),
                  pl.BlockSpec((1, N_MOD, d), lambda b, j: (b, 0, 0)),
                  _const_spec((1, d)),
                  _const_spec(w_all.shape), _const_spec(b_all.shape)],
        out_specs=[pl.BlockSpec((1, TM_PROJ, nq), blk),
                   pl.BlockSpec((1, TM_PROJ, nk), blk),
                   pl.BlockSpec((1, nv, TM_PROJ), lambda b, j: (b, 0, j))],
        out_shape=[jax.ShapeDtypeStruct((bsz, seq, nq), BF16),
                   jax.ShapeDtypeStruct((bsz, seq, nk), BF16),
                   jax.ShapeDtypeStruct((bsz, nv, seq), BF16)],
        compiler_params=pltpu.CompilerParams(dimension_semantics=("arbitrary", "arbitrary"),
                                             vmem_limit_bytes=VMEM_LIMIT),
        name="qkv_proj",
    )(x, mod, g_pre.reshape(1, d), w_all, b_all)


def _bias_kernel(rel_ref, ba_ref, bb_ref):
    j = lax.broadcasted_iota(jnp.int32, (WIN_A, G_A * SUB), 0)
    r = lax.broadcasted_iota(jnp.int32, (WIN_A, G_A * SUB), 1)
    g = r // SUB
    i = r % SUB
    for k in range(KV_A):
        slope = jnp.zeros(r.shape, F32)
        for gg in range(G_A):
            slope = jnp.where(g == gg, 2.0 ** (-8.0 * (k * G_A + gg + 1) / H_A), slope)
        for v in range(NV_A):
            off = min(v * SUB, PAD_A)
            back = off // CHUNK + i // CHUNK - j // CHUNK
            bias = -slope * jnp.abs(off + i - j).astype(F32)
            ba_ref[v, k] = jnp.where((back >= 0) & (back <= BACK_A), bias * LOG2E, NEG_INF)

    table = rel_ref[...]
    dcol = lax.broadcasted_iota(jnp.int32, (H_B, ROLL_W), 1)
    d = jnp.where(dcol < SUB, dcol, dcol - ROLL_W)
    idx = jnp.clip(PAD + d, -REL_CLIP, REL_CLIP) + REL_CLIP
    row = jnp.zeros((H_B, ROLL_W), F32)
    for n in range(N_REL):
        row = jnp.where(idx == n, table[:, n:n + 1], row)
    jb = lax.broadcasted_iota(jnp.int32, (WIN_B, SUB), 0)
    ib = lax.broadcasted_iota(jnp.int32, (WIN_B, SUB), 1)
    for h in range(H_B):
        tile = jnp.broadcast_to(row[h:h + 1, :], (EXT_B, ROLL_W))
        tile = pltpu.roll(tile, 0, 1, stride=1, stride_axis=0)[:, :SUB]
        tile = jnp.concatenate([tile, jnp.zeros((PAD + WIN_B - EXT_B, SUB), F32)], axis=0)
        for v in range(NV_B):
            off = min(v * SUB, PAD)
            back = off // CHUNK + ib // CHUNK - jb // CHUNK
            window = tile[PAD - off:PAD - off + WIN_B]
            bb_ref[v, h // 2, :, pl.ds((h % 2) * SUB, SUB)] = jnp.where(
                (back >= 0) & (back <= BACK_B), window * LOG2E, NEG_INF)


def _bias_tiles(rel_bias):
    return pl.pallas_call(
        _bias_kernel,
        out_shape=[jax.ShapeDtypeStruct((NV_A, KV_A, WIN_A, G_A * SUB), F32),
                   jax.ShapeDtypeStruct((NV_B, H_B // 2, WIN_B, 2 * SUB), F32)],
        compiler_params=pltpu.CompilerParams(vmem_limit_bytes=VMEM_LIMIT),
        name="score_bias",
    )(rel_bias)


def _rms_rows(xt, g_col):
    ms = jnp.mean(xt * xt, axis=0, keepdims=True)
    return xt * lax.rsqrt(ms + EPS) * g_col


def _mixer_kernel(sink_ref, q_ref, k_ref, vt_ref, ba_ref, bb_ref, gga_ref, ggb_ref, wo_ref, bo_ref,
                  gpost_ref, mod_ref, x_ref, o_ref):
    i = pl.program_id(1)
    ws_b = [pl.multiple_of(jnp.maximum(i * TQ + u * SUB - PAD, 0), SUB) for u in range(NSUB)]
    ws_a = [pl.multiple_of(jnp.maximum(i * TQ + u * SUB - PAD_A, 0), SUB) for u in range(NSUB)]
    var_b = [jnp.minimum(i * NSUB + u, NV_B - 1) for u in range(NSUB)]
    var_a = [jnp.minimum(i * NSUB + u, NV_A - 1) for u in range(NSUB)]

    low = lax.broadcasted_iota(jnp.int32, (1, LANES), 1) < HEAD_DIM
    halves = (low, jnp.logical_not(low))
    col_g = lax.broadcasted_iota(jnp.int32, (1, G_A * SUB), 1) // SUB
    ones_a = jnp.ones((DEN_ROWS, WIN_A), BF16)
    ones_b = jnp.ones((DEN_ROWS, WIN_B), BF16)

    def scores_a(u, k):
        k_a = k_ref[0, pl.ds(ws_a[u], WIN_A), 0:LANES]
        qs = jnp.concatenate(
            [jnp.where(halves[k], q_ref[0, u * SUB:(u + 1) * SUB, g * LANES:(g + 1) * LANES], 0)
             for g in range(G_A)], axis=0)
        s = lax.dot_general(k_a, qs, _NT, preferred_element_type=F32)
        return s + ba_ref[var_a[u], k]

    def attend_a(u, k, s):
        sink = jnp.zeros((1, G_A * SUB), F32)
        for g in range(G_A):
            sink = jnp.where(col_g == g, sink_ref[k * G_A + g] * LOG2E, sink)
        m = jnp.maximum(jnp.max(s, axis=0, keepdims=True), sink)
        e = jnp.exp2(s - m).astype(BF16)
        vt_a = jnp.concatenate([vt_ref[0, 0:LANES, pl.ds(ws_a[u], WIN_A)], ones_a], axis=0)
        o = jnp.dot(vt_a, e, preferred_element_type=F32)
        den = o[LANES:LANES + 1] + jnp.exp2(sink - m)
        o = o[k * HEAD_DIM:(k + 1) * HEAD_DIM] * (1.0 / den)
        return [o[:, g * SUB:(g + 1) * SUB] for g in range(G_A)]

    def scores_b(u, t):
        q_t = q_ref[0, u * SUB:(u + 1) * SUB, QA + t * LANES:QA + (t + 1) * LANES]
        k_t = k_ref[0, pl.ds(ws_b[u], WIN_B), (1 + t) * LANES:(2 + t) * LANES]
        q2 = jnp.concatenate([jnp.where(halves[0], q_t, 0), jnp.where(halves[1], q_t, 0)], axis=0)
        s = lax.dot_general(k_t, q2, _NT, preferred_element_type=F32)
        return s + bb_ref[var_b[u], t]

    def attend_b(u, t, s):
        m = jnp.max(s, axis=0, keepdims=True)
        e = jnp.exp2(s - m).astype(BF16)
        vt_t = jnp.concatenate([vt_ref[0, (1 + t) * LANES:(2 + t) * LANES, pl.ds(ws_b[u], WIN_B)],
                                ones_b], axis=0)
        o = jnp.dot(vt_t, e, preferred_element_type=F32)
        rden = 1.0 / o[LANES:LANES + 1]
        return [o[:HEAD_DIM, :SUB] * rden[:, :SUB], o[HEAD_DIM:LANES, SUB:] * rden[:, SUB:]]

    groups = []
    for u in range(NSUB):
        groups += [(scores_a, attend_a, u, k) for k in range(KV_A)]
        groups += [(scores_b, attend_b, u, t) for t in range(H_B // 2)]
    per_sub = len(groups) // NSUB
    outs = []

    def finish(first_sub):
        oa_t, ob_t = [], []
        for u in range(first_sub, first_sub + FIN_SUBS):
            sub = outs[u * per_sub:(u + 1) * per_sub]
            oa_t.append(jnp.concatenate([sub[k][g] for k in range(KV_A) for g in range(G_A)], axis=0))
            ob_t.append(jnp.concatenate([part for pair in sub[KV_A:] for part in pair], axis=0))
        oa_t = jnp.concatenate(oa_t, axis=1)
        ob_t = jnp.concatenate(ob_t, axis=1)
        y_t = jnp.concatenate([_rms_rows(oa_t, gga_ref[...]), _rms_rows(ob_t, ggb_ref[...])], axis=0)
        z = lax.dot_general(y_t.astype(BF16), wo_ref[...], _TN, preferred_element_type=F32) + bo_ref[...]
        r = slice(first_sub * SUB, (first_sub + FIN_SUBS) * SUB)
        o_ref[0, r, :] = x_ref[0, r, :] + _rms(z, mod_ref[0, pl.ds(5, 1), :] * gpost_ref[...])

    pending = [grp[0](*grp[2:]) for grp in groups[:SCORE_AHEAD]]
    for n, (_, attend, u, idx) in enumerate(groups):
        if n + SCORE_AHEAD < len(groups):
            nxt = groups[n + SCORE_AHEAD]
            pending.append(nxt[0](*nxt[2:]))
        outs.append(attend(u, idx, pending.pop(0)))
        done_groups = n + 1 - FIN_LAG
        if 0 < done_groups < len(groups) and done_groups % (FIN_SUBS * per_sub) == 0:
            finish(done_groups // per_sub - FIN_SUBS)
    finish(NSUB - FIN_SUBS)


def _mixer(x, mod, q, k, vt, bias_a, bias_b, sinks, gga, ggb, wo, bo, g_post):
    bsz, seq, d = x.shape
    assert seq % TQ == 0 and TQ % SUB == 0
    tile = lambda b, i: (b, i, 0)
    whole = lambda b, i: (b, 0, 0)
    return pl.pallas_call(
        _mixer_kernel,
        grid=(bsz, seq // TQ),
        in_specs=[pl.BlockSpec(memory_space=pltpu.SMEM),
                  pl.BlockSpec((1, TQ, q.shape[2]), tile),
                  pl.BlockSpec((1,) + k.shape[1:], whole),
                  pl.BlockSpec((1,) + vt.shape[1:], whole),
                  _const_spec(bias_a.shape), _const_spec(bias_b.shape),
                  _const_spec((QA, 1)), _const_spec((QB, 1)),
                  _const_spec(wo.shape), _const_spec((1, d)), _const_spec((1, d)),
                  pl.BlockSpec((1, N_MOD, d), whole),
                  pl.BlockSpec((1, TQ, d), tile)],
        out_specs=pl.BlockSpec((1, TQ, d), tile),
        out_shape=jax.ShapeDtypeStruct((bsz, seq, d), F32),
        compiler_params=pltpu.CompilerParams(dimension_semantics=("arbitrary", "arbitrary"),
                                             vmem_limit_bytes=VMEM_LIMIT),
        name="mixer",
    )(sinks, q, k, vt, bias_a, bias_b, gga.reshape(QA, 1), ggb.reshape(QB, 1), wo,
      bo.reshape(1, d), g_post.reshape(1, d), mod, x)


def kernel(x, c, w_ada, b_ada, g_pre_ffn1, w_gate1, w_up1, w_down1, g_post_ffn1, g_pre_mix, w_in, b_in, sinks_a, rel_bias_b, g_grp_a, g_grp_b, w_out, b_out, g_post_mix, g_pre_ffn2, w_gate2, w_up2, w_down2, g_post_ffn2):
    bsz, seq, d = x.shape
    depth = w_ada.shape[0]
    for l in range(depth):
        mod = _modulation(c, w_ada[l], b_ada[l]).reshape(bsz, N_MOD, d)

        x = _ffn(x, mod, g_pre_ffn1[l], g_post_ffn1[l], w_gate1[l], w_up1[l], w_down1[l], mod0=0)

        q, k, vt = _projection(x, mod, g_pre_mix[l], w_in[l].astype(BF16), b_in[l].reshape(1, -1))

        bias_a, bias_b = _bias_tiles(rel_bias_b[l])
        x = _mixer(x, mod, q, k, vt, bias_a, bias_b, sinks_a[l], g_grp_a[l], g_grp_b[l],
                   w_out[l].astype(BF16), b_out[l], g_post_mix[l])

        x = _ffn(x, mod, g_pre_ffn2[l], g_post_ffn2[l], w_gate2[l], w_up2[l], w_down2[l], mod0=6)
    return x
```

```python
import functools
import math

import jax
import jax.numpy as jnp
from jax import lax
from jax.experimental import pallas as pl
from jax.experimental.pallas import tpu as pltpu

F32 = jnp.float32
BF16 = jnp.bfloat16

CHUNK = 64
HEAD_DIM = 64
EPS = 1e-6
NEG_INF = -1e30
H_A, KV_A = 8, 2
G_A = H_A // KV_A
BACK_A = 2
H_B = 8
BACK_B = 8
REL_CLIP = 128
N_REL = 2 * REL_CLIP + 1
QA = H_A * HEAD_DIM
KVA = KV_A * HEAD_DIM
QB = H_B * HEAD_DIM
N_MOD = 9
LOG2E = math.log2(math.e)

LANES = 128
TQ = 512
SUB = LANES
NSUB = TQ // SUB
SCORE_AHEAD = 3
FIN_SUBS = 2
FIN_LAG = 3
PAD = BACK_B * CHUNK
PAD_A = BACK_A * CHUNK
WIN_A = SUB + PAD_A
WIN_B = SUB + PAD
NV_A = PAD_A // SUB + 1
NV_B = PAD // SUB + 1
EXT_B = PAD + SUB + CHUNK
DEN_ROWS = 16
ROLL_W = 1024
TM_FFN = 1024
FFN_SLAB = 512
W_ROWS = 256
W_SLOTS = 3
TM_PROJ = 1024
PROJ_SLAB = 512
TN_MOD = 1536
FF_CHUNK = 256
VMEM_LIMIT = 56 * 1024 * 1024

_NT = (((1,), (1,)), ((), ()))
_TN = (((0,), (0,)), ((), ()))


def _rms(x, g):
    ms = jnp.mean(x * x, axis=-1, keepdims=True)
    return x * lax.rsqrt(ms + EPS) * g


def _mod_row(mod_ref, b, k):
    d = mod_ref.shape[1] // N_MOD
    return mod_ref[pl.ds(b, 1), pl.ds(k * d, d)]


def _const_spec(shape):
    nd = len(shape)
    return pl.BlockSpec(shape, lambda *_: (0,) * nd, pipeline_mode=pl.Buffered(1))


def _mod_kernel(c_ref, w_ref, b_ref, o_ref):
    c = c_ref[...]
    a = (c * jax.nn.sigmoid(c)).astype(BF16)
    o_ref[...] = jnp.dot(a, w_ref[...].astype(BF16), preferred_element_type=F32) + b_ref[...]


def _modulation(c, w_ada, b_ada):
    bsz, d = c.shape
    n = w_ada.shape[1]
    assert n % TN_MOD == 0
    return pl.pallas_call(
        _mod_kernel,
        grid=(n // TN_MOD,),
        in_specs=[pl.BlockSpec((bsz, d), lambda j: (0, 0)),
                  pl.BlockSpec((d, TN_MOD), lambda j: (0, j)),
                  pl.BlockSpec((1, TN_MOD), lambda j: (0, j))],
        out_specs=pl.BlockSpec((bsz, TN_MOD), lambda j: (0, j)),
        out_shape=jax.ShapeDtypeStruct((bsz, n), F32),
        compiler_params=pltpu.CompilerParams(dimension_semantics=("arbitrary",),
                                             vmem_limit_bytes=VMEM_LIMIT),
        name="adaln_mod",
    )(c, w_ada, b_ada.reshape(1, n))


def _load_bf16(src_hbm, dst_ref, stage, sem):
    ncol = src_hbm.shape[1]
    nblk = src_hbm.shape[0] // W_ROWS
    nslot = stage.shape[0]

    def copy(n):
        return pltpu.make_async_copy(src_hbm.at[pl.ds(n * W_ROWS, W_ROWS), :],
                                     stage.at[n % nslot, :, pl.ds(0, ncol)], sem.at[n % nslot])

    for n in range(min(nslot - 1, nblk)):
        copy(n).start()
    for n in range(nblk):
        if n + nslot - 1 < nblk:
            copy(n + nslot - 1).start()
        copy(n).wait()
        dst_ref[n * W_ROWS:(n + 1) * W_ROWS, :] = stage[n % nslot, :, 0:ncol].astype(dst_ref.dtype)


def _ffn_kernel(x_ref, mod_ref, gpre_ref, gpost_ref, wg_hbm, wu_hbm, wd_hbm, o_ref,
                wg_ref, wu_ref, wd_ref, stage, sem, *, mod0):
    @pl.when((pl.program_id(0) == 0) & (pl.program_id(1) == 0))
    def _():
        _load_bf16(wg_hbm, wg_ref, stage, sem)
        _load_bf16(wu_hbm, wu_ref, stage, sem)
        _load_bf16(wd_hbm, wd_ref, stage, sem)

    b = pl.program_id(0)
    shift = _mod_row(mod_ref, b, mod0)
    gain_in = gpre_ref[...] * (1.0 + _mod_row(mod_ref, b, mod0 + 1))
    gain_out = (0.5 * _mod_row(mod_ref, b, mod0 + 2)) * gpost_ref[...]
    d_ff = wg_ref.shape[1]
    nslab = x_ref.shape[1] // FFN_SLAB
    nchunk = d_ff // FF_CHUNK

    def rows(s):
        return slice(s * FFN_SLAB, (s + 1) * FFN_SLAB)

    def pre(s):
        return (_rms(x_ref[0, rows(s), :], gain_in) + shift).astype(BF16)

    def post(s, acc):
        o_ref[0, rows(s), :] = x_ref[0, rows(s), :] + _rms(acc, gain_out)

    h = pre(0)
    done = None
    for s in range(nslab):
        acc = jnp.zeros((FFN_SLAB, x_ref.shape[2]), F32)
        h_next = None
        for c in range(nchunk):
            if c == nchunk // 4 and done is not None:
                post(s - 1, done)
            if c == nchunk // 2 and s + 1 < nslab:
                h_next = pre(s + 1)
            f0 = c * FF_CHUNK
            g = jnp.dot(h, wg_ref[:, f0:f0 + FF_CHUNK], preferred_element_type=F32)
            u = jnp.dot(h, wu_ref[:, f0:f0 + FF_CHUNK], preferred_element_type=F32)
            a = (g * jax.nn.sigmoid(g) * u).astype(BF16)
            acc = acc + jnp.dot(a, wd_ref[f0:f0 + FF_CHUNK, :], preferred_element_type=F32)
        done, h = acc, h_next
    post(nslab - 1, done)


def _ffn(x, mod, g_pre, g_post, wg, wu, wd, *, mod0):
    bsz, seq, d = x.shape
    d_ff = wg.shape[1]
    assert seq % TM_FFN == 0 and TM_FFN % FFN_SLAB == 0 and d_ff % FF_CHUNK == 0
    assert d % W_ROWS == 0 and d_ff % W_ROWS == 0
    blk = lambda b, j: (b, j, 0)
    hbm = pl.BlockSpec(memory_space=pl.ANY)
    return pl.pallas_call(
        functools.partial(_ffn_kernel, mod0=mod0),
        grid=(bsz, seq // TM_FFN),
        in_specs=[pl.BlockSpec((1, TM_FFN, d), blk),
                  _const_spec(mod.shape),
                  _const_spec((1, d)), _const_spec((1, d)),
                  hbm, hbm, hbm],
        out_specs=pl.BlockSpec((1, TM_FFN, d), blk),
        out_shape=jax.ShapeDtypeStruct((bsz, seq, d), F32),
        scratch_shapes=[pltpu.VMEM((d, d_ff), BF16), pltpu.VMEM((d, d_ff), BF16),
                        pltpu.VMEM((d_ff, d), BF16),
                        pltpu.VMEM((W_SLOTS, W_ROWS, max(d, d_ff)), F32),
                        pltpu.SemaphoreType.DMA((W_SLOTS,))],
        compiler_params=pltpu.CompilerParams(dimension_semantics=("arbitrary", "arbitrary"),
                                             vmem_limit_bytes=VMEM_LIMIT),
        name="ffn",
    )(x, mod, g_pre.reshape(1, d), g_post.reshape(1, d), wg, wu, wd)


def _proj_kernel(x_ref, mod_ref, gpre_ref, w_ref, b_ref, q_ref, k_ref, vt_ref):
    b = pl.program_id(0)
    shift = _mod_row(mod_ref, b, 3)
    gain_in = gpre_ref[...] * (1.0 + _mod_row(mod_ref, b, 4))
    nslab = x_ref.shape[1] // PROJ_SLAB

    def rows(s):
        return slice(s * PROJ_SLAB, (s + 1) * PROJ_SLAB)

    def pre(s):
        return (_rms(x_ref[0, rows(s), :], gain_in) + shift).astype(BF16)

    low = lax.broadcasted_iota(jnp.int32, (1, LANES), 1) < HEAD_DIM
    o_ka, o_va, o_qb = QA, QA + KVA, QA + 2 * KVA
    o_kb, o_vb = o_qb + QB, o_qb + 2 * QB
    h = pre(0)
    for s in range(nslab):
        p = jnp.dot(h, w_ref[...], preferred_element_type=F32) + b_ref[...]
        h_next = pre(s + 1) if s + 1 < nslab else None
        nat = [p[:, t * LANES:(t + 1) * LANES] for t in range(QA // LANES)]
        swp = [pltpu.roll(t, HEAD_DIM, 1) for t in nat]
        qa = []
        for g in range(G_A):
            t0, t1 = g // 2, G_A // 2 + g // 2
            qa.append(jnp.where(low, nat[t0], swp[t1]) if g % 2 == 0 else jnp.where(low, swp[t0], nat[t1]))
        q = jnp.concatenate(qa + [p[:, o_qb:o_kb]], axis=1)
        q_ref[0, rows(s), :] = (q * (HEAD_DIM ** -0.5 * LOG2E)).astype(BF16)
        k_ref[0, rows(s), :] = jnp.concatenate([p[:, o_ka:o_va], p[:, o_kb:o_vb]], axis=1).astype(BF16)
        v = jnp.concatenate([p[:, o_va:o_qb], p[:, o_vb:]], axis=1)
        vt_ref[0, :, rows(s)] = v.T.astype(BF16)
        h = h_next


def _projection(x, mod, g_pre, w_all, b_all):
    bsz, seq, d = x.shape
    nq, nk, nv = QA + QB, KVA + QB, KVA + QB
    assert w_all.shape == (d, nq + nk + nv)
    assert seq % TM_PROJ == 0 and TM_PROJ % PROJ_SLAB == 0
    blk = lambda b, j: (b, j, 0)
    return pl.pallas_call(
        _proj_kernel,
        grid=(bsz, seq // TM_PROJ),
        in_specs=[pl.BlockSpec((1, TM_PROJ, d), blk),
                  _const_spec(mod.shape),
                  _const_spec((1, d)),
                  _const_spec(w_all.shape), _const_spec(b_all.shape)],
        out_specs=[pl.BlockSpec((1, TM_PROJ, nq), blk),
                   pl.BlockSpec((1, TM_PROJ, nk), blk),
                   pl.BlockSpec((1, nv, TM_PROJ), lambda b, j: (b, 0, j))],
        out_shape=[jax.ShapeDtypeStruct((bsz, seq, nq), BF16),
                   jax.ShapeDtypeStruct((bsz, seq, nk), BF16),
                   jax.ShapeDtypeStruct((bsz, nv, seq), BF16)],
        compiler_params=pltpu.CompilerParams(dimension_semantics=("arbitrary", "arbitrary"),
                                             vmem_limit_bytes=VMEM_LIMIT),
        name="qkv_proj",
    )(x, mod, g_pre.reshape(1, d), w_all, b_all)


def _bias_kernel(rel_ref, ba_ref, bb_ref):
    j = lax.broadcasted_iota(jnp.int32, (WIN_A, G_A * SUB), 0)
    r = lax.broadcasted_iota(jnp.int32, (WIN_A, G_A * SUB), 1)
    g = r // SUB
    i = r % SUB
    for k in range(KV_A):
        slope = jnp.zeros(r.shape, F32)
        for gg in range(G_A):
            slope = jnp.where(g == gg, 2.0 ** (-8.0 * (k * G_A + gg + 1) / H_A), slope)
        for v in range(NV_A):
            off = min(v * SUB, PAD_A)
            back = off // CHUNK + i // CHUNK - j // CHUNK
            bias = -slope * jnp.abs(off + i - j).astype(F32)
            ba_ref[v, k] = jnp.where((back >= 0) & (back <= BACK_A), bias * LOG2E, NEG_INF)

    table = rel_ref[...]
    dcol = lax.broadcasted_iota(jnp.int32, (H_B, ROLL_W), 1)
    d = jnp.where(dcol < SUB, dcol, dcol - ROLL_W)
    idx = jnp.clip(PAD + d, -REL_CLIP, REL_CLIP) + REL_CLIP
    row = jnp.zeros((H_B, ROLL_W), F32)
    for n in range(N_REL):
        row = jnp.where(idx == n, table[:, n:n + 1], row)
    jb = lax.broadcasted_iota(jnp.int32, (WIN_B, SUB), 0)
    ib = lax.broadcasted_iota(jnp.int32, (WIN_B, SUB), 1)
    for h in range(H_B):
        tile = jnp.broadcast_to(row[h:h + 1, :], (EXT_B, ROLL_W))
        tile = pltpu.roll(tile, 0, 1, stride=1, stride_axis=0)[:, :SUB]
        tile = jnp.concatenate([tile, jnp.zeros((PAD + WIN_B - EXT_B, SUB), F32)], axis=0)
        for v in range(NV_B):
            off = min(v * SUB, PAD)
            back = off // CHUNK + ib // CHUNK - jb // CHUNK
            window = tile[PAD - off:PAD - off + WIN_B]
            bb_ref[v, h // 2, :, pl.ds((h % 2) * SUB, SUB)] = jnp.where(
                (back >= 0) & (back <= BACK_B), window * LOG2E, NEG_INF)


def _bias_tiles(rel_bias):
    return pl.pallas_call(
        _bias_kernel,
        out_shape=[jax.ShapeDtypeStruct((NV_A, KV_A, WIN_A, G_A * SUB), F32),
                   jax.ShapeDtypeStruct((NV_B, H_B // 2, WIN_B, 2 * SUB), F32)],
        compiler_params=pltpu.CompilerParams(vmem_limit_bytes=VMEM_LIMIT),
        name="score_bias",
    )(rel_bias)


def _as_columns(g_row, width):
    t = jnp.transpose(jnp.broadcast_to(g_row, (LANES, g_row.shape[1])))
    return jnp.concatenate([t] * (width // LANES), axis=1)


def _rms_rows(xt, g_col):
    ms = jnp.mean(xt * xt, axis=0, keepdims=True)
    return xt * lax.rsqrt(ms + EPS) * g_col


def _mixer_kernel(sink_ref, q_ref, k_ref, vt_ref, ba_ref, bb_ref, gga_ref, ggb_ref, wo_ref, bo_ref,
                  gpost_ref, mod_ref, x_ref, o_ref):
    i = pl.program_id(1)
    ws_b = [pl.multiple_of(jnp.maximum(i * TQ + u * SUB - PAD, 0), SUB) for u in range(NSUB)]
    ws_a = [pl.multiple_of(jnp.maximum(i * TQ + u * SUB - PAD_A, 0), SUB) for u in range(NSUB)]
    var_b = [jnp.minimum(i * NSUB + u, NV_B - 1) for u in range(NSUB)]
    var_a = [jnp.minimum(i * NSUB + u, NV_A - 1) for u in range(NSUB)]

    low = lax.broadcasted_iota(jnp.int32, (1, LANES), 1) < HEAD_DIM
    halves = (low, jnp.logical_not(low))
    col_g = lax.broadcasted_iota(jnp.int32, (1, G_A * SUB), 1) // SUB
    ones_a = jnp.ones((DEN_ROWS, WIN_A), BF16)
    ones_b = jnp.ones((DEN_ROWS, WIN_B), BF16)

    def scores_a(u, k):
        k_a = k_ref[0, pl.ds(ws_a[u], WIN_A), 0:LANES]
        qs = jnp.concatenate(
            [jnp.where(halves[k], q_ref[0, u * SUB:(u + 1) * SUB, g * LANES:(g + 1) * LANES], 0)
             for g in range(G_A)], axis=0)
        s = lax.dot_general(k_a, qs, _NT, preferred_element_type=F32)
        return s + ba_ref[var_a[u], k]

    def attend_a(u, k, s):
        sink = jnp.zeros((1, G_A * SUB), F32)
        for g in range(G_A):
            sink = jnp.where(col_g == g, sink_ref[k * G_A + g] * LOG2E, sink)
        m = jnp.maximum(jnp.max(s, axis=0, keepdims=True), sink)
        e = jnp.exp2(s - m).astype(BF16)
        vt_a = jnp.concatenate([vt_ref[0, 0:LANES, pl.ds(ws_a[u], WIN_A)], ones_a], axis=0)
        o = jnp.dot(vt_a, e, preferred_element_type=F32)
        den = o[LANES:LANES + 1] + jnp.exp2(sink - m)
        o = o[k * HEAD_DIM:(k + 1) * HEAD_DIM] * (1.0 / den)
        return [o[:, g * SUB:(g + 1) * SUB] for g in range(G_A)]

    def scores_b(u, t):
        q_t = q_ref[0, u * SUB:(u + 1) * SUB, QA + t * LANES:QA + (t + 1) * LANES]
        k_t = k_ref[0, pl.ds(ws_b[u], WIN_B), (1 + t) * LANES:(2 + t) * LANES]
        q2 = jnp.concatenate([jnp.where(halves[0], q_t, 0), jnp.where(halves[1], q_t, 0)], axis=0)
        s = lax.dot_general(k_t, q2, _NT, preferred_element_type=F32)
        return s + bb_ref[var_b[u], t]

    def attend_b(u, t, s):
        m = jnp.max(s, axis=0, keepdims=True)
        e = jnp.exp2(s - m).astype(BF16)
        vt_t = jnp.concatenate([vt_ref[0, (1 + t) * LANES:(2 + t) * LANES, pl.ds(ws_b[u], WIN_B)],
                                ones_b], axis=0)
        o = jnp.dot(vt_t, e, preferred_element_type=F32)
        rden = 1.0 / o[LANES:LANES + 1]
        return [o[:HEAD_DIM, :SUB] * rden[:, :SUB], o[HEAD_DIM:LANES, SUB:] * rden[:, SUB:]]

    groups = []
    for u in range(NSUB):
        groups += [(scores_a, attend_a, u, k) for k in range(KV_A)]
        groups += [(scores_b, attend_b, u, t) for t in range(H_B // 2)]
    per_sub = len(groups) // NSUB
    outs = []
    gain_out = _mod_row(mod_ref, pl.program_id(0), 5) * gpost_ref[...]
    gain_a = _as_columns(gga_ref[...], FIN_SUBS * SUB)
    gain_b = _as_columns(ggb_ref[...], FIN_SUBS * SUB)

    def finish(first_sub):
        oa_t, ob_t = [], []
        for u in range(first_sub, first_sub + FIN_SUBS):
            sub = outs[u * per_sub:(u + 1) * per_sub]
            oa_t.append(jnp.concatenate([sub[k][g] for k in range(KV_A) for g in range(G_A)], axis=0))
            ob_t.append(jnp.concatenate([part for pair in sub[KV_A:] for part in pair], axis=0))
        oa_t = jnp.concatenate(oa_t, axis=1)
        ob_t = jnp.concatenate(ob_t, axis=1)
        y_t = jnp.concatenate([_rms_rows(oa_t, gain_a), _rms_rows(ob_t, gain_b)], axis=0)
        z = lax.dot_general(y_t.astype(BF16), wo_ref[...], _TN, preferred_element_type=F32) + bo_ref[...]
        r = slice(first_sub * SUB, (first_sub + FIN_SUBS) * SUB)
        o_ref[0, r, :] = x_ref[0, r, :] + _rms(z, gain_out)

    pending = [grp[0](*grp[2:]) for grp in groups[:SCORE_AHEAD]]
    for n, (_, attend, u, idx) in enumerate(groups):
        if n + SCORE_AHEAD < len(groups):
            nxt = groups[n + SCORE_AHEAD]
            pending.append(nxt[0](*nxt[2:]))
        outs.append(attend(u, idx, pending.pop(0)))
        done_groups = n + 1 - FIN_LAG
        if 0 < done_groups < len(groups) and done_groups % (FIN_SUBS * per_sub) == 0:
            finish(done_groups // per_sub - FIN_SUBS)
    finish(NSUB - FIN_SUBS)


def _mixer(x, mod, q, k, vt, bias_a, bias_b, sinks, gga, ggb, wo, bo, g_post):
    bsz, seq, d = x.shape
    assert seq % TQ == 0 and TQ % SUB == 0
    tile = lambda b, i: (b, i, 0)
    whole = lambda b, i: (b, 0, 0)
    return pl.pallas_call(
        _mixer_kernel,
        grid=(bsz, seq // TQ),
        in_specs=[pl.BlockSpec(memory_space=pltpu.SMEM),
                  pl.BlockSpec((1, TQ, q.shape[2]), tile),
                  pl.BlockSpec((1,) + k.shape[1:], whole),
                  pl.BlockSpec((1,) + vt.shape[1:], whole),
                  _const_spec(bias_a.shape), _const_spec(bias_b.shape),
                  _const_spec((1, QA)), _const_spec((1, QB)),
                  _const_spec(wo.shape), _const_spec((1, d)), _const_spec((1, d)),
                  _const_spec(mod.shape),
                  pl.BlockSpec((1, TQ, d), tile)],
        out_specs=pl.BlockSpec((1, TQ, d), tile),
        out_shape=jax.ShapeDtypeStruct((bsz, seq, d), F32),
        compiler_params=pltpu.CompilerParams(dimension_semantics=("arbitrary", "arbitrary"),
                                             vmem_limit_bytes=VMEM_LIMIT),
        name="mixer",
    )(sinks, q, k, vt, bias_a, bias_b, gga.reshape(1, QA), ggb.reshape(1, QB), wo,
      bo.reshape(1, d), g_post.reshape(1, d), mod, x)


def kernel(x, c, w_ada, b_ada, g_pre_ffn1, w_gate1, w_up1, w_down1, g_post_ffn1, g_pre_mix, w_in, b_in, sinks_a, rel_bias_b, g_grp_a, g_grp_b, w_out, b_out, g_post_mix, g_pre_ffn2, w_gate2, w_up2, w_down2, g_post_ffn2):
    bsz, seq, d = x.shape
    depth = w_ada.shape[0]
    for l in range(depth):
        mod = _modulation(c, w_ada[l], b_ada[l])

        x = _ffn(x, mod, g_pre_ffn1[l], g_post_ffn1[l], w_gate1[l], w_up1[l], w_down1[l], mod0=0)

        q, k, vt = _projection(x, mod, g_pre_mix[l], w_in[l].astype(BF16), b_in[l].reshape(1, -1))

        bias_a, bias_b = _bias_tiles(rel_bias_b[l])
        x = _mixer(x, mod, q, k, vt, bias_a, bias_b, sinks_a[l], g_grp_a[l], g_grp_b[l],
                   w_out[l].astype(BF16), b_out[l], g_post_mix[l])

        x = _ffn(x, mod, g_pre_ffn2[l], g_post_ffn2[l], w_gate2[l], w_up2[l], w_down2[l], mod0=6)
    return x
```

```python
import functools
import math

import jax
import jax.numpy as jnp
from jax import lax
from jax.experimental import pallas as pl
from jax.experimental.pallas import tpu as pltpu

F32 = jnp.float32
BF16 = jnp.bfloat16

CHUNK = 64
HEAD_DIM = 64
EPS = 1e-6
NEG_INF = -1e30
H_A, KV_A = 8, 2
G_A = H_A // KV_A
BACK_A = 2
H_B = 8
BACK_B = 8
REL_CLIP = 128
N_REL = 2 * REL_CLIP + 1
QA = H_A * HEAD_DIM
KVA = KV_A * HEAD_DIM
QB = H_B * HEAD_DIM
N_MOD = 9
LOG2E = math.log2(math.e)

LANES = 128
TQ = 512
SUB = LANES
NSUB = TQ // SUB
SCORE_AHEAD = 3
FIN_SUBS = 2
FIN_LAG = 3
PAD = BACK_B * CHUNK
PAD_A = BACK_A * CHUNK
WIN_A = SUB + PAD_A
WIN_B = SUB + PAD
NV_A = PAD_A // SUB + 1
NV_B = PAD // SUB + 1
EXT_B = PAD + SUB + CHUNK
DEN_ROWS = 16
ROLL_W = 1024
TM_FFN = 1024
FFN_SLAB = 512
W_ROWS = 256
W_SLOTS = 3
TM_PROJ = 1024
PROJ_SLAB = 512
TN_MOD = 1536
FF_CHUNK = 256
VMEM_LIMIT = 56 * 1024 * 1024

_NT = (((1,), (1,)), ((), ()))
_TN = (((0,), (0,)), ((), ()))


def _rms(x, g):
    ms = jnp.mean(x * x, axis=-1, keepdims=True)
    return x * lax.rsqrt(ms + EPS) * g


def _mod_row(mod_ref, b, k):
    d = mod_ref.shape[1] // N_MOD
    return mod_ref[pl.ds(b, 1), pl.ds(k * d, d)]


def _const_spec(shape):
    nd = len(shape)
    return pl.BlockSpec(shape, lambda *_: (0,) * nd, pipeline_mode=pl.Buffered(1))


def _mod_kernel(c_ref, w_ref, b_ref, o_ref):
    c = c_ref[...]
    a = (c * jax.nn.sigmoid(c)).astype(BF16)
    o_ref[...] = jnp.dot(a, w_ref[...].astype(BF16), preferred_element_type=F32) + b_ref[...]


def _modulation(c, w_ada, b_ada):
    bsz, d = c.shape
    n = w_ada.shape[1]
    assert n % TN_MOD == 0
    return pl.pallas_call(
        _mod_kernel,
        grid=(n // TN_MOD,),
        in_specs=[pl.BlockSpec((bsz, d), lambda j: (0, 0)),
                  pl.BlockSpec((d, TN_MOD), lambda j: (0, j)),
                  pl.BlockSpec((1, TN_MOD), lambda j: (0, j))],
        out_specs=pl.BlockSpec((bsz, TN_MOD), lambda j: (0, j)),
        out_shape=jax.ShapeDtypeStruct((bsz, n), F32),
        compiler_params=pltpu.CompilerParams(dimension_semantics=("arbitrary",),
                                             vmem_limit_bytes=VMEM_LIMIT),
        name="adaln_mod",
    )(c, w_ada, b_ada.reshape(1, n))


def _load_bf16(src_hbm, dst_ref, stage, sem):
    ncol = src_hbm.shape[1]
    nblk = src_hbm.shape[0] // W_ROWS
    nslot = stage.shape[0]

    def copy(n):
        return pltpu.make_async_copy(src_hbm.at[pl.ds(n * W_ROWS, W_ROWS), :],
                                     stage.at[n % nslot, :, pl.ds(0, ncol)], sem.at[n % nslot])

    for n in range(min(nslot - 1, nblk)):
        copy(n).start()
    for n in range(nblk):
        if n + nslot - 1 < nblk:
            copy(n + nslot - 1).start()
        copy(n).wait()
        dst_ref[n * W_ROWS:(n + 1) * W_ROWS, :] = stage[n % nslot, :, 0:ncol].astype(dst_ref.dtype)


def _ffn_kernel(x_ref, mod_ref, gpre_ref, gpost_ref, wg_hbm, wu_hbm, wd_hbm, o_ref,
                wg_ref, wu_ref, wd_ref, stage, sem, *, mod0):
    @pl.when((pl.program_id(0) == 0) & (pl.program_id(1) == 0))
    def _():
        _load_bf16(wg_hbm, wg_ref, stage, sem)
        _load_bf16(wu_hbm, wu_ref, stage, sem)
        _load_bf16(wd_hbm, wd_ref, stage, sem)

    b = pl.program_id(0)
    shift = _mod_row(mod_ref, b, mod0)
    gain_in = gpre_ref[...] * (1.0 + _mod_row(mod_ref, b, mod0 + 1))
    gain_out = (0.5 * _mod_row(mod_ref, b, mod0 + 2)) * gpost_ref[...]
    d_ff = wg_ref.shape[1]
    nslab = x_ref.shape[1] // FFN_SLAB
    nchunk = d_ff // FF_CHUNK

    def rows(s):
        return slice(s * FFN_SLAB, (s + 1) * FFN_SLAB)

    def pre(s):
        return (_rms(x_ref[0, rows(s), :], gain_in) + shift).astype(BF16)

    def post(s, acc):
        o_ref[0, rows(s), :] = x_ref[0, rows(s), :] + _rms(acc, gain_out)

    h = pre(0)
    done = None
    for s in range(nslab):
        acc = jnp.zeros((FFN_SLAB, x_ref.shape[2]), F32)
        h_next = None
        for c in range(nchunk):
            if c == nchunk // 4 and done is not None:
                post(s - 1, done)
            if c == nchunk // 2 and s + 1 < nslab:
                h_next = pre(s + 1)
            f0 = c * FF_CHUNK
            g = jnp.dot(h, wg_ref[:, f0:f0 + FF_CHUNK], preferred_element_type=F32)
            u = jnp.dot(h, wu_ref[:, f0:f0 + FF_CHUNK], preferred_element_type=F32)
            a = (g * jax.nn.sigmoid(g) * u).astype(BF16)
            acc = acc + jnp.dot(a, wd_ref[f0:f0 + FF_CHUNK, :], preferred_element_type=F32)
        done, h = acc, h_next
    post(nslab - 1, done)


def _ffn(x, mod, g_pre, g_post, wg, wu, wd, *, mod0):
    bsz, seq, d = x.shape
    d_ff = wg.shape[1]
    assert seq % TM_FFN == 0 and TM_FFN % FFN_SLAB == 0 and d_ff % FF_CHUNK == 0
    assert d % W_ROWS == 0 and d_ff % W_ROWS == 0
    blk = lambda b, j: (b, j, 0)
    hbm = pl.BlockSpec(memory_space=pl.ANY)
    return pl.pallas_call(
        functools.partial(_ffn_kernel, mod0=mod0),
        grid=(bsz, seq // TM_FFN),
        in_specs=[pl.BlockSpec((1, TM_FFN, d), blk),
                  _const_spec(mod.shape),
                  _const_spec((1, d)), _const_spec((1, d)),
                  hbm, hbm, hbm],
        out_specs=pl.BlockSpec((1, TM_FFN, d), blk),
        out_shape=jax.ShapeDtypeStruct((bsz, seq, d), F32),
        scratch_shapes=[pltpu.VMEM((d, d_ff), BF16), pltpu.VMEM((d, d_ff), BF16),
                        pltpu.VMEM((d_ff, d), BF16),
                        pltpu.VMEM((W_SLOTS, W_ROWS, max(d, d_ff)), F32),
                        pltpu.SemaphoreType.DMA((W_SLOTS,))],
        compiler_params=pltpu.CompilerParams(dimension_semantics=("arbitrary", "arbitrary"),
                                             vmem_limit_bytes=VMEM_LIMIT),
        name="ffn",
    )(x, mod, g_pre.reshape(1, d), g_post.reshape(1, d), wg, wu, wd)


def _proj_kernel(x_ref, mod_ref, gpre_ref, w_ref, b_ref, q_ref, k_ref, vt_ref):
    b = pl.program_id(0)
    shift = _mod_row(mod_ref, b, 3)
    gain_in = gpre_ref[...] * (1.0 + _mod_row(mod_ref, b, 4))
    nslab = x_ref.shape[1] // PROJ_SLAB

    def rows(s):
        return slice(s * PROJ_SLAB, (s + 1) * PROJ_SLAB)

    def pre(s):
        return (_rms(x_ref[0, rows(s), :], gain_in) + shift).astype(BF16)

    low = lax.broadcasted_iota(jnp.int32, (1, LANES), 1) < HEAD_DIM
    o_ka, o_va, o_qb = QA, QA + KVA, QA + 2 * KVA
    o_kb, o_vb = o_qb + QB, o_qb + 2 * QB
    h = pre(0)
    for s in range(nslab):
        p = jnp.dot(h, w_ref[...], preferred_element_type=F32) + b_ref[...]
        h_next = pre(s + 1) if s + 1 < nslab else None
        nat = [p[:, t * LANES:(t + 1) * LANES] for t in range(QA // LANES)]
        swp = [pltpu.roll(t, HEAD_DIM, 1) for t in nat]
        qa = []
        for g in range(G_A):
            t0, t1 = g // 2, G_A // 2 + g // 2
            qa.append(jnp.where(low, nat[t0], swp[t1]) if g % 2 == 0 else jnp.where(low, swp[t0], nat[t1]))
        q = jnp.concatenate(qa + [p[:, o_qb:o_kb]], axis=1)
        q_ref[0, rows(s), :] = (q * (HEAD_DIM ** -0.5 * LOG2E)).astype(BF16)
        k_ref[0, rows(s), :] = jnp.concatenate([p[:, o_ka:o_va], p[:, o_kb:o_vb]], axis=1).astype(BF16)
        v = jnp.concatenate([p[:, o_va:o_qb], p[:, o_vb:]], axis=1)
        vt_ref[0, :, rows(s)] = v.T.astype(BF16)
        h = h_next


def _projection(x, mod, g_pre, w_all, b_all):
    bsz, seq, d = x.shape
    nq, nk, nv = QA + QB, KVA + QB, KVA + QB
    assert w_all.shape == (d, nq + nk + nv)
    assert seq % TM_PROJ == 0 and TM_PROJ % PROJ_SLAB == 0
    blk = lambda b, j: (b, j, 0)
    return pl.pallas_call(
        _proj_kernel,
        grid=(bsz, seq // TM_PROJ),
        in_specs=[pl.BlockSpec((1, TM_PROJ, d), blk),
                  _const_spec(mod.shape),
                  _const_spec((1, d)),
                  _const_spec(w_all.shape), _const_spec(b_all.shape)],
        out_specs=[pl.BlockSpec((1, TM_PROJ, nq), blk),
                   pl.BlockSpec((1, TM_PROJ, nk), blk),
                   pl.BlockSpec((1, nv, TM_PROJ), lambda b, j: (b, 0, j))],
        out_shape=[jax.ShapeDtypeStruct((bsz, seq, nq), BF16),
                   jax.ShapeDtypeStruct((bsz, seq, nk), BF16),
                   jax.ShapeDtypeStruct((bsz, nv, seq), BF16)],
        compiler_params=pltpu.CompilerParams(dimension_semantics=("arbitrary", "arbitrary"),
                                             vmem_limit_bytes=VMEM_LIMIT),
        name="qkv_proj",
    )(x, mod, g_pre.reshape(1, d), w_all, b_all)


def _bias_tiles(rel_ref, ba_ref, bb_ref):
    j = lax.broadcasted_iota(jnp.int32, (WIN_A, G_A * SUB), 0)
    r = lax.broadcasted_iota(jnp.int32, (WIN_A, G_A * SUB), 1)
    g = r // SUB
    i = r % SUB
    for k in range(KV_A):
        slope = jnp.zeros(r.shape, F32)
        for gg in range(G_A):
            slope = jnp.where(g == gg, 2.0 ** (-8.0 * (k * G_A + gg + 1) / H_A), slope)
        for v in range(NV_A):
            off = min(v * SUB, PAD_A)
            back = off // CHUNK + i // CHUNK - j // CHUNK
            bias = -slope * jnp.abs(off + i - j).astype(F32)
            ba_ref[v, k] = jnp.where((back >= 0) & (back <= BACK_A), bias * LOG2E, NEG_INF)

    table = rel_ref[...]
    dcol = lax.broadcasted_iota(jnp.int32, (H_B, ROLL_W), 1)
    d = jnp.where(dcol < SUB, dcol, dcol - ROLL_W)
    idx = jnp.clip(PAD + d, -REL_CLIP, REL_CLIP) + REL_CLIP
    row = jnp.zeros((H_B, ROLL_W), F32)
    for n in range(N_REL):
        row = jnp.where(idx == n, table[:, n:n + 1], row)
    jb = lax.broadcasted_iota(jnp.int32, (WIN_B, SUB), 0)
    ib = lax.broadcasted_iota(jnp.int32, (WIN_B, SUB), 1)
    for h in range(H_B):
        tile = jnp.broadcast_to(row[h:h + 1, :], (EXT_B, ROLL_W))
        tile = pltpu.roll(tile, 0, 1, stride=1, stride_axis=0)[:, :SUB]
        tile = jnp.concatenate([tile, jnp.zeros((PAD + WIN_B - EXT_B, SUB), F32)], axis=0)
        for v in range(NV_B):
            off = min(v * SUB, PAD)
            back = off // CHUNK + ib // CHUNK - jb // CHUNK
            window = tile[PAD - off:PAD - off + WIN_B]
            bb_ref[v, h // 2, :, pl.ds((h % 2) * SUB, SUB)] = jnp.where(
                (back >= 0) & (back <= BACK_B), window * LOG2E, NEG_INF)


def _as_columns(g_row, width):
    t = jnp.transpose(jnp.broadcast_to(g_row, (LANES, g_row.shape[1])))
    return jnp.concatenate([t] * (width // LANES), axis=1)


def _rms_rows(xt, g_col):
    ms = jnp.mean(xt * xt, axis=0, keepdims=True)
    return xt * lax.rsqrt(ms + EPS) * g_col


def _mixer_kernel(sink_ref, rel_ref, q_ref, k_ref, vt_ref, gga_ref, ggb_ref, wo_ref, bo_ref,
                  gpost_ref, mod_ref, x_ref, o_ref, ba_ref, bb_ref):
    i = pl.program_id(1)

    @pl.when((pl.program_id(0) == 0) & (i == 0))
    def _():
        _bias_tiles(rel_ref, ba_ref, bb_ref)

    ws_b = [pl.multiple_of(jnp.maximum(i * TQ + u * SUB - PAD, 0), SUB) for u in range(NSUB)]
    ws_a = [pl.multiple_of(jnp.maximum(i * TQ + u * SUB - PAD_A, 0), SUB) for u in range(NSUB)]
    var_b = [jnp.minimum(i * NSUB + u, NV_B - 1) for u in range(NSUB)]
    var_a = [jnp.minimum(i * NSUB + u, NV_A - 1) for u in range(NSUB)]

    low = lax.broadcasted_iota(jnp.int32, (1, LANES), 1) < HEAD_DIM
    halves = (low, jnp.logical_not(low))
    col_g = lax.broadcasted_iota(jnp.int32, (1, G_A * SUB), 1) // SUB
    ones_a = jnp.ones((DEN_ROWS, WIN_A), BF16)
    ones_b = jnp.ones((DEN_ROWS, WIN_B), BF16)

    def scores_a(u, k):
        k_a = k_ref[0, pl.ds(ws_a[u], WIN_A), 0:LANES]
        qs = jnp.concatenate(
            [jnp.where(halves[k], q_ref[0, u * SUB:(u + 1) * SUB, g * LANES:(g + 1) * LANES], 0)
             for g in range(G_A)], axis=0)
        s = lax.dot_general(k_a, qs, _NT, preferred_element_type=F32)
        return s + ba_ref[var_a[u], k]

    def attend_a(u, k, s):
        sink = jnp.zeros((1, G_A * SUB), F32)
        for g in range(G_A):
            sink = jnp.where(col_g == g, sink_ref[k * G_A + g] * LOG2E, sink)
        m = jnp.maximum(jnp.max(s, axis=0, keepdims=True), sink)
        e = jnp.exp2(s - m).astype(BF16)
        vt_a = jnp.concatenate([vt_ref[0, 0:LANES, pl.ds(ws_a[u], WIN_A)], ones_a], axis=0)
        o = jnp.dot(vt_a, e, preferred_element_type=F32)
        den = o[LANES:LANES + 1] + jnp.exp2(sink - m)
        o = o[k * HEAD_DIM:(k + 1) * HEAD_DIM] * (1.0 / den)
        return [o[:, g * SUB:(g + 1) * SUB] for g in range(G_A)]

    def scores_b(u, t):
        q_t = q_ref[0, u * SUB:(u + 1) * SUB, QA + t * LANES:QA + (t + 1) * LANES]
        k_t = k_ref[0, pl.ds(ws_b[u], WIN_B), (1 + t) * LANES:(2 + t) * LANES]
        q2 = jnp.concatenate([jnp.where(halves[0], q_t, 0), jnp.where(halves[1], q_t, 0)], axis=0)
        s = lax.dot_general(k_t, q2, _NT, preferred_element_type=F32)
        return s + bb_ref[var_b[u], t]

    def attend_b(u, t, s):
        m = jnp.max(s, axis=0, keepdims=True)
        e = jnp.exp2(s - m).astype(BF16)
        vt_t = jnp.concatenate([vt_ref[0, (1 + t) * LANES:(2 + t) * LANES, pl.ds(ws_b[u], WIN_B)],
                                ones_b], axis=0)
        o = jnp.dot(vt_t, e, preferred_element_type=F32)
        rden = 1.0 / o[LANES:LANES + 1]
        return [o[:HEAD_DIM, :SUB] * rden[:, :SUB], o[HEAD_DIM:LANES, SUB:] * rden[:, SUB:]]

    groups = []
    for u in range(NSUB):
        groups += [(scores_a, attend_a, u, k) for k in range(KV_A)]
        groups += [(scores_b, attend_b, u, t) for t in range(H_B // 2)]
    per_sub = len(groups) // NSUB
    outs = []
    gain_out = _mod_row(mod_ref, pl.program_id(0), 5) * gpost_ref[...]
    gain_a = _as_columns(gga_ref[...], FIN_SUBS * SUB)
    gain_b = _as_columns(ggb_ref[...], FIN_SUBS * SUB)

    def finish(first_sub):
        oa_t, ob_t = [], []
        for u in range(first_sub, first_sub + FIN_SUBS):
            sub = outs[u * per_sub:(u + 1) * per_sub]
            oa_t.append(jnp.concatenate([sub[k][g] for k in range(KV_A) for g in range(G_A)], axis=0))
            ob_t.append(jnp.concatenate([part for pair in sub[KV_A:] for part in pair], axis=0))
        oa_t = jnp.concatenate(oa_t, axis=1)
        ob_t = jnp.concatenate(ob_t, axis=1)
        y_t = jnp.concatenate([_rms_rows(oa_t, gain_a), _rms_rows(ob_t, gain_b)], axis=0)
        z = lax.dot_general(y_t.astype(BF16), wo_ref[...], _TN, preferred_element_type=F32) + bo_ref[...]
        r = slice(first_sub * SUB, (first_sub + FIN_SUBS) * SUB)
        o_ref[0, r, :] = x_ref[0, r, :] + _rms(z, gain_out)

    pending = [grp[0](*grp[2:]) for grp in groups[:SCORE_AHEAD]]
    for n, (_, attend, u, idx) in enumerate(groups):
        if n + SCORE_AHEAD < len(groups):
            nxt = groups[n + SCORE_AHEAD]
            pending.append(nxt[0](*nxt[2:]))
        outs.append(attend(u, idx, pending.pop(0)))
        done_groups = n + 1 - FIN_LAG
        if 0 < done_groups < len(groups) and done_groups % (FIN_SUBS * per_sub) == 0:
            finish(done_groups // per_sub - FIN_SUBS)
    finish(NSUB - FIN_SUBS)


def _mixer(x, mod, q, k, vt, rel_bias, sinks, gga, ggb, wo, bo, g_post):
    bsz, seq, d = x.shape
    assert seq % TQ == 0 and TQ % SUB == 0
    tile = lambda b, i: (b, i, 0)
    whole = lambda b, i: (b, 0, 0)
    return pl.pallas_call(
        _mixer_kernel,
        grid=(bsz, seq // TQ),
        in_specs=[pl.BlockSpec(memory_space=pltpu.SMEM),
                  _const_spec(rel_bias.shape),
                  pl.BlockSpec((1, TQ, q.shape[2]), tile),
                  pl.BlockSpec((1,) + k.shape[1:], whole),
                  pl.BlockSpec((1,) + vt.shape[1:], whole),
                  _const_spec((1, QA)), _const_spec((1, QB)),
                  _const_spec(wo.shape), _const_spec((1, d)), _const_spec((1, d)),
                  _const_spec(mod.shape),
                  pl.BlockSpec((1, TQ, d), tile)],
        out_specs=pl.BlockSpec((1, TQ, d), tile),
        out_shape=jax.ShapeDtypeStruct((bsz, seq, d), F32),
        scratch_shapes=[pltpu.VMEM((NV_A, KV_A, WIN_A, G_A * SUB), F32),
                        pltpu.VMEM((NV_B, H_B // 2, WIN_B, 2 * SUB), F32)],
        compiler_params=pltpu.CompilerParams(dimension_semantics=("arbitrary", "arbitrary"),
                                             vmem_limit_bytes=VMEM_LIMIT),
        name="mixer",
    )(sinks, rel_bias, q, k, vt, gga.reshape(1, QA), ggb.reshape(1, QB), wo,
      bo.reshape(1, d), g_post.reshape(1, d), mod, x)


def kernel(x, c, w_ada, b_ada, g_pre_ffn1, w_gate1, w_up1, w_down1, g_post_ffn1, g_pre_mix, w_in, b_in, sinks_a, rel_bias_b, g_grp_a, g_grp_b, w_out, b_out, g_post_mix, g_pre_ffn2, w_gate2, w_up2, w_down2, g_post_ffn2):
    bsz, seq, d = x.shape
    depth = w_ada.shape[0]
    for l in range(depth):
        mod = _modulation(c, w_ada[l], b_ada[l])

        x = _ffn(x, mod, g_pre_ffn1[l], g_post_ffn1[l], w_gate1[l], w_up1[l], w_down1[l], mod0=0)

        q, k, vt = _projection(x, mod, g_pre_mix[l], w_in[l].astype(BF16), b_in[l].reshape(1, -1))

        x = _mixer(x, mod, q, k, vt, rel_bias_b[l], sinks_a[l], g_grp_a[l], g_grp_b[l],
                   w_out[l].astype(BF16), b_out[l], g_post_mix[l])

        x = _ffn(x, mod, g_pre_ffn2[l], g_post_ffn2[l], w_gate2[l], w_up2[l], w_down2[l], mod0=6)
    return x
```

```python
import functools
import math

import jax
import jax.numpy as jnp
from jax import lax
from jax.experimental import pallas as pl
from jax.experimental.pallas import tpu as pltpu

F32 = jnp.float32
BF16 = jnp.bfloat16

CHUNK = 64
HEAD_DIM = 64
EPS = 1e-6
NEG_INF = -1e30
H_A, KV_A = 8, 2
G_A = H_A // KV_A
BACK_A = 2
H_B = 8
BACK_B = 8
REL_CLIP = 128
N_REL = 2 * REL_CLIP + 1
QA = H_A * HEAD_DIM
KVA = KV_A * HEAD_DIM
QB = H_B * HEAD_DIM
N_MOD = 9
LOG2E = math.log2(math.e)

LANES = 128
TQ = 512
SUB = LANES
NSUB = TQ // SUB
SCORE_AHEAD = 3
FIN_SUBS = 2
FIN_LAG = 3
PAD = BACK_B * CHUNK
PAD_A = BACK_A * CHUNK
WIN_A = SUB + PAD_A
WIN_B = SUB + PAD
NV_A = PAD_A // SUB + 1
NV_B = PAD // SUB + 1
EXT_B = PAD + SUB + CHUNK
DEN_ROWS = 16
ROLL_W = 1024
TM_FFN = 1024
FFN_SLAB = 512
W_ROWS = 256
W_SLOTS = 3
TM_PROJ = 1024
PROJ_SLAB = 512
TN_MOD = 1536
FF_CHUNK = 256
VMEM_LIMIT = 56 * 1024 * 1024

_NT = (((1,), (1,)), ((), ()))
_TN = (((0,), (0,)), ((), ()))


def _rms(x, g):
    ms = jnp.mean(x * x, axis=-1, keepdims=True)
    return x * lax.rsqrt(ms + EPS) * g


def _mod_row(mod_ref, b, k):
    d = mod_ref.shape[1] // N_MOD
    return mod_ref[pl.ds(b, 1), pl.ds(k * d, d)]


def _const_spec(shape):
    nd = len(shape)
    return pl.BlockSpec(shape, lambda *_: (0,) * nd, pipeline_mode=pl.Buffered(1))


def _mod_kernel(c_ref, w_ref, b_ref, o_ref):
    c = c_ref[...]
    a = (c * jax.nn.sigmoid(c)).astype(BF16)
    o_ref[...] = jnp.dot(a, w_ref[...].astype(BF16), preferred_element_type=F32) + b_ref[...]


def _modulation(c, w_ada, b_ada):
    bsz, d = c.shape
    n = w_ada.shape[1]
    assert n % TN_MOD == 0
    return pl.pallas_call(
        _mod_kernel,
        grid=(n // TN_MOD,),
        in_specs=[pl.BlockSpec((bsz, d), lambda j: (0, 0)),
                  pl.BlockSpec((d, TN_MOD), lambda j: (0, j)),
                  pl.BlockSpec((1, TN_MOD), lambda j: (0, j))],
        out_specs=pl.BlockSpec((bsz, TN_MOD), lambda j: (0, j)),
        out_shape=jax.ShapeDtypeStruct((bsz, n), F32),
        compiler_params=pltpu.CompilerParams(dimension_semantics=("arbitrary",),
                                             vmem_limit_bytes=VMEM_LIMIT),
        name="adaln_mod",
    )(c, w_ada, b_ada.reshape(1, n))


def _load_bf16(src_hbm, dst_ref, stage, sem):
    ncol = src_hbm.shape[1]
    nblk = src_hbm.shape[0] // W_ROWS
    nslot = stage.shape[0]

    def copy(n):
        return pltpu.make_async_copy(src_hbm.at[pl.ds(n * W_ROWS, W_ROWS), :],
                                     stage.at[n % nslot, :, pl.ds(0, ncol)], sem.at[n % nslot])

    for n in range(min(nslot - 1, nblk)):
        copy(n).start()
    for n in range(nblk):
        if n + nslot - 1 < nblk:
            copy(n + nslot - 1).start()
        copy(n).wait()
        dst_ref[n * W_ROWS:(n + 1) * W_ROWS, :] = stage[n % nslot, :, 0:ncol].astype(dst_ref.dtype)


def _ffn_kernel(x_ref, mod_ref, gpre_ref, gpost_ref, wg_hbm, wu_hbm, wd_hbm, o_ref,
                wg_ref, wu_ref, wd_ref, stage, sem, *, mod0):
    @pl.when((pl.program_id(0) == 0) & (pl.program_id(1) == 0))
    def _():
        _load_bf16(wg_hbm, wg_ref, stage, sem)
        _load_bf16(wu_hbm, wu_ref, stage, sem)
        _load_bf16(wd_hbm, wd_ref, stage, sem)

    b = pl.program_id(0)
    shift = _mod_row(mod_ref, b, mod0)
    gain_in = gpre_ref[...] * (1.0 + _mod_row(mod_ref, b, mod0 + 1))
    gain_out = (0.5 * _mod_row(mod_ref, b, mod0 + 2)) * gpost_ref[...]
    d_ff = wg_ref.shape[1]
    nslab = x_ref.shape[1] // FFN_SLAB
    nchunk = d_ff // FF_CHUNK

    def rows(s):
        return slice(s * FFN_SLAB, (s + 1) * FFN_SLAB)

    def pre(s):
        return (_rms(x_ref[0, rows(s), :], gain_in) + shift).astype(BF16)

    def post(s, acc):
        o_ref[0, rows(s), :] = x_ref[0, rows(s), :] + _rms(acc, gain_out)

    h = pre(0)
    done = None
    for s in range(nslab):
        acc = jnp.zeros((FFN_SLAB, x_ref.shape[2]), F32)
        h_next = None
        for c in range(nchunk):
            if c == nchunk // 4 and done is not None:
                post(s - 1, done)
            if c == nchunk // 2 and s + 1 < nslab:
                h_next = pre(s + 1)
            f0 = c * FF_CHUNK
            g = jnp.dot(h, wg_ref[:, f0:f0 + FF_CHUNK], preferred_element_type=F32)
            u = jnp.dot(h, wu_ref[:, f0:f0 + FF_CHUNK], preferred_element_type=F32)
            a = (g * jax.nn.sigmoid(g) * u).astype(BF16)
            acc = acc + jnp.dot(a, wd_ref[f0:f0 + FF_CHUNK, :], preferred_element_type=F32)
        done, h = acc, h_next
    post(nslab - 1, done)


def _ffn(x, mod, g_pre, g_post, wg, wu, wd, *, mod0):
    bsz, seq, d = x.shape
    d_ff = wg.shape[1]
    assert seq % TM_FFN == 0 and TM_FFN % FFN_SLAB == 0 and d_ff % FF_CHUNK == 0
    assert d % W_ROWS == 0 and d_ff % W_ROWS == 0
    blk = lambda b, j: (b, j, 0)
    hbm = pl.BlockSpec(memory_space=pl.ANY)
    return pl.pallas_call(
        functools.partial(_ffn_kernel, mod0=mod0),
        grid=(bsz, seq // TM_FFN),
        in_specs=[pl.BlockSpec((1, TM_FFN, d), blk),
                  _const_spec(mod.shape),
                  _const_spec((1, d)), _const_spec((1, d)),
                  hbm, hbm, hbm],
        out_specs=pl.BlockSpec((1, TM_FFN, d), blk),
        out_shape=jax.ShapeDtypeStruct((bsz, seq, d), F32),
        scratch_shapes=[pltpu.VMEM((d, d_ff), BF16), pltpu.VMEM((d, d_ff), BF16),
                        pltpu.VMEM((d_ff, d), BF16),
                        pltpu.VMEM((W_SLOTS, W_ROWS, max(d, d_ff)), F32),
                        pltpu.SemaphoreType.DMA((W_SLOTS,))],
        compiler_params=pltpu.CompilerParams(dimension_semantics=("arbitrary", "arbitrary"),
                                             vmem_limit_bytes=VMEM_LIMIT),
        name="ffn",
    )(x, mod, g_pre.reshape(1, d), g_post.reshape(1, d), wg, wu, wd)


def _proj_kernel(x_ref, mod_ref, gpre_ref, w_hbm, b_ref, q_ref, k_ref, vt_ref, w_ref, stage, sem):
    b = pl.program_id(0)

    @pl.when((b == 0) & (pl.program_id(1) == 0))
    def _():
        _load_bf16(w_hbm, w_ref, stage, sem)

    shift = _mod_row(mod_ref, b, 3)
    gain_in = gpre_ref[...] * (1.0 + _mod_row(mod_ref, b, 4))
    nslab = x_ref.shape[1] // PROJ_SLAB

    def rows(s):
        return slice(s * PROJ_SLAB, (s + 1) * PROJ_SLAB)

    def pre(s):
        return (_rms(x_ref[0, rows(s), :], gain_in) + shift).astype(BF16)

    low = lax.broadcasted_iota(jnp.int32, (1, LANES), 1) < HEAD_DIM
    o_ka, o_va, o_qb = QA, QA + KVA, QA + 2 * KVA
    o_kb, o_vb = o_qb + QB, o_qb + 2 * QB
    h = pre(0)
    for s in range(nslab):
        p = jnp.dot(h, w_ref[...], preferred_element_type=F32) + b_ref[...]
        h_next = pre(s + 1) if s + 1 < nslab else None
        nat = [p[:, t * LANES:(t + 1) * LANES] for t in range(QA // LANES)]
        swp = [pltpu.roll(t, HEAD_DIM, 1) for t in nat]
        qa = []
        for g in range(G_A):
            t0, t1 = g // 2, G_A // 2 + g // 2
            qa.append(jnp.where(low, nat[t0], swp[t1]) if g % 2 == 0 else jnp.where(low, swp[t0], nat[t1]))
        q = jnp.concatenate(qa + [p[:, o_qb:o_kb]], axis=1)
        q_ref[0, rows(s), :] = (q * (HEAD_DIM ** -0.5 * LOG2E)).astype(BF16)
        k_ref[0, rows(s), :] = jnp.concatenate([p[:, o_ka:o_va], p[:, o_kb:o_vb]], axis=1).astype(BF16)
        v = jnp.concatenate([p[:, o_va:o_qb], p[:, o_vb:]], axis=1)
        vt_ref[0, :, rows(s)] = v.T.astype(BF16)
        h = h_next


def _projection(x, mod, g_pre, w_all, b_all):
    bsz, seq, d = x.shape
    nq, nk, nv = QA + QB, KVA + QB, KVA + QB
    assert w_all.shape == (d, nq + nk + nv)
    assert seq % TM_PROJ == 0 and TM_PROJ % PROJ_SLAB == 0 and d % W_ROWS == 0
    blk = lambda b, j: (b, j, 0)
    return pl.pallas_call(
        _proj_kernel,
        grid=(bsz, seq // TM_PROJ),
        in_specs=[pl.BlockSpec((1, TM_PROJ, d), blk),
                  _const_spec(mod.shape),
                  _const_spec((1, d)),
                  pl.BlockSpec(memory_space=pl.ANY), _const_spec(b_all.shape)],
        out_specs=[pl.BlockSpec((1, TM_PROJ, nq), blk),
                   pl.BlockSpec((1, TM_PROJ, nk), blk),
                   pl.BlockSpec((1, nv, TM_PROJ), lambda b, j: (b, 0, j))],
        out_shape=[jax.ShapeDtypeStruct((bsz, seq, nq), BF16),
                   jax.ShapeDtypeStruct((bsz, seq, nk), BF16),
                   jax.ShapeDtypeStruct((bsz, nv, seq), BF16)],
        scratch_shapes=[pltpu.VMEM(w_all.shape, BF16),
                        pltpu.VMEM((W_SLOTS, W_ROWS, w_all.shape[1]), F32),
                        pltpu.SemaphoreType.DMA((W_SLOTS,))],
        compiler_params=pltpu.CompilerParams(dimension_semantics=("arbitrary", "arbitrary"),
                                             vmem_limit_bytes=VMEM_LIMIT),
        name="qkv_proj",
    )(x, mod, g_pre.reshape(1, d), w_all, b_all)


def _bias_tiles(rel_ref, ba_ref, bb_ref):
    j = lax.broadcasted_iota(jnp.int32, (WIN_A, G_A * SUB), 0)
    r = lax.broadcasted_iota(jnp.int32, (WIN_A, G_A * SUB), 1)
    g = r // SUB
    i = r % SUB
    for k in range(KV_A):
        slope = jnp.zeros(r.shape, F32)
        for gg in range(G_A):
            slope = jnp.where(g == gg, 2.0 ** (-8.0 * (k * G_A + gg + 1) / H_A), slope)
        for v in range(NV_A):
            off = min(v * SUB, PAD_A)
            back = off // CHUNK + i // CHUNK - j // CHUNK
            bias = -slope * jnp.abs(off + i - j).astype(F32)
            ba_ref[v, k] = jnp.where((back >= 0) & (back <= BACK_A), bias * LOG2E, NEG_INF)

    table = rel_ref[...]
    dcol = lax.broadcasted_iota(jnp.int32, (H_B, ROLL_W), 1)
    d = jnp.where(dcol < SUB, dcol, dcol - ROLL_W)
    idx = jnp.clip(PAD + d, -REL_CLIP, REL_CLIP) + REL_CLIP
    row = jnp.zeros((H_B, ROLL_W), F32)
    for n in range(N_REL):
        row = jnp.where(idx == n, table[:, n:n + 1], row)
    jb = lax.broadcasted_iota(jnp.int32, (WIN_B, SUB), 0)
    ib = lax.broadcasted_iota(jnp.int32, (WIN_B, SUB), 1)
    for h in range(H_B):
        tile = jnp.broadcast_to(row[h:h + 1, :], (EXT_B, ROLL_W))
        tile = pltpu.roll(tile, 0, 1, stride=1, stride_axis=0)[:, :SUB]
        tile = jnp.concatenate([tile, jnp.zeros((PAD + WIN_B - EXT_B, SUB), F32)], axis=0)
        for v in range(NV_B):
            off = min(v * SUB, PAD)
            back = off // CHUNK + ib // CHUNK - jb // CHUNK
            window = tile[PAD - off:PAD - off + WIN_B]
            bb_ref[v, h // 2, :, pl.ds((h % 2) * SUB, SUB)] = jnp.where(
                (back >= 0) & (back <= BACK_B), window * LOG2E, NEG_INF)


def _as_columns(g_row, width):
    t = jnp.transpose(jnp.broadcast_to(g_row, (LANES, g_row.shape[1])))
    return jnp.concatenate([t] * (width // LANES), axis=1)


def _rms_rows(xt, g_col):
    ms = jnp.mean(xt * xt, axis=0, keepdims=True)
    return xt * lax.rsqrt(ms + EPS) * g_col


def _mixer_kernel(sink_ref, rel_ref, q_ref, k_ref, vt_ref, gga_ref, ggb_ref, wo_ref, bo_ref,
                  gpost_ref, mod_ref, x_ref, o_ref, ba_ref, bb_ref):
    i = pl.program_id(1)

    @pl.when((pl.program_id(0) == 0) & (i == 0))
    def _():
        _bias_tiles(rel_ref, ba_ref, bb_ref)

    ws_b = [pl.multiple_of(jnp.maximum(i * TQ + u * SUB - PAD, 0), SUB) for u in range(NSUB)]
    ws_a = [pl.multiple_of(jnp.maximum(i * TQ + u * SUB - PAD_A, 0), SUB) for u in range(NSUB)]
    var_b = [jnp.minimum(i * NSUB + u, NV_B - 1) for u in range(NSUB)]
    var_a = [jnp.minimum(i * NSUB + u, NV_A - 1) for u in range(NSUB)]

    low = lax.broadcasted_iota(jnp.int32, (1, LANES), 1) < HEAD_DIM
    halves = (low, jnp.logical_not(low))
    col_g = lax.broadcasted_iota(jnp.int32, (1, G_A * SUB), 1) // SUB
    ones_a = jnp.ones((DEN_ROWS, WIN_A), BF16)
    ones_b = jnp.ones((DEN_ROWS, WIN_B), BF16)

    def scores_a(u, k):
        k_a = k_ref[0, pl.ds(ws_a[u], WIN_A), 0:LANES]
        qs = jnp.concatenate(
            [jnp.where(halves[k], q_ref[0, u * SUB:(u + 1) * SUB, g * LANES:(g + 1) * LANES], 0)
             for g in range(G_A)], axis=0)
        s = lax.dot_general(k_a, qs, _NT, preferred_element_type=F32)
        return s + ba_ref[var_a[u], k]

    def attend_a(u, k, s):
        sink = jnp.zeros((1, G_A * SUB), F32)
        for g in range(G_A):
            sink = jnp.where(col_g == g, sink_ref[k * G_A + g] * LOG2E, sink)
        m = jnp.maximum(jnp.max(s, axis=0, keepdims=True), sink)
        e = jnp.exp2(s - m).astype(BF16)
        vt_a = jnp.concatenate([vt_ref[0, 0:LANES, pl.ds(ws_a[u], WIN_A)], ones_a], axis=0)
        o = jnp.dot(vt_a, e, preferred_element_type=F32)
        den = o[LANES:LANES + 1] + jnp.exp2(sink - m)
        o = o[k * HEAD_DIM:(k + 1) * HEAD_DIM] * (1.0 / den)
        return [o[:, g * SUB:(g + 1) * SUB] for g in range(G_A)]

    def scores_b(u, t):
        q_t = q_ref[0, u * SUB:(u + 1) * SUB, QA + t * LANES:QA + (t + 1) * LANES]
        k_t = k_ref[0, pl.ds(ws_b[u], WIN_B), (1 + t) * LANES:(2 + t) * LANES]
        q2 = jnp.concatenate([jnp.where(halves[0], q_t, 0), jnp.where(halves[1], q_t, 0)], axis=0)
        s = lax.dot_general(k_t, q2, _NT, preferred_element_type=F32)
        return s + bb_ref[var_b[u], t]

    def attend_b(u, t, s):
        m = jnp.max(s, axis=0, keepdims=True)
        e = jnp.exp2(s - m).astype(BF16)
        vt_t = jnp.concatenate([vt_ref[0, (1 + t) * LANES:(2 + t) * LANES, pl.ds(ws_b[u], WIN_B)],
                                ones_b], axis=0)
        o = jnp.dot(vt_t, e, preferred_element_type=F32)
        rden = 1.0 / o[LANES:LANES + 1]
        return [o[:HEAD_DIM, :SUB] * rden[:, :SUB], o[HEAD_DIM:LANES, SUB:] * rden[:, SUB:]]

    groups = []
    for u in range(NSUB):
        groups += [(scores_a, attend_a, u, k) for k in range(KV_A)]
        groups += [(scores_b, attend_b, u, t) for t in range(H_B // 2)]
    per_sub = len(groups) // NSUB
    outs = []
    gain_out = _mod_row(mod_ref, pl.program_id(0), 5) * gpost_ref[...]
    gain_a = _as_columns(gga_ref[...], FIN_SUBS * SUB)
    gain_b = _as_columns(ggb_ref[...], FIN_SUBS * SUB)

    def finish(first_sub):
        oa_t, ob_t = [], []
        for u in range(first_sub, first_sub + FIN_SUBS):
            sub = outs[u * per_sub:(u + 1) * per_sub]
            oa_t.append(jnp.concatenate([sub[k][g] for k in range(KV_A) for g in range(G_A)], axis=0))
            ob_t.append(jnp.concatenate([part for pair in sub[KV_A:] for part in pair], axis=0))
        oa_t = jnp.concatenate(oa_t, axis=1)
        ob_t = jnp.concatenate(ob_t, axis=1)
        y_t = jnp.concatenate([_rms_rows(oa_t, gain_a), _rms_rows(ob_t, gain_b)], axis=0)
        z = lax.dot_general(y_t.astype(BF16), wo_ref[...], _TN, preferred_element_type=F32) + bo_ref[...]
        r = slice(first_sub * SUB, (first_sub + FIN_SUBS) * SUB)
        o_ref[0, r, :] = x_ref[0, r, :] + _rms(z, gain_out)

    pending = [grp[0](*grp[2:]) for grp in groups[:SCORE_AHEAD]]
    for n, (_, attend, u, idx) in enumerate(groups):
        if n + SCORE_AHEAD < len(groups):
            nxt = groups[n + SCORE_AHEAD]
            pending.append(nxt[0](*nxt[2:]))
        outs.append(attend(u, idx, pending.pop(0)))
        done_groups = n + 1 - FIN_LAG
        if 0 < done_groups < len(groups) and done_groups % (FIN_SUBS * per_sub) == 0:
            finish(done_groups // per_sub - FIN_SUBS)
    finish(NSUB - FIN_SUBS)


def _mixer(x, mod, q, k, vt, rel_bias, sinks, gga, ggb, wo, bo, g_post):
    bsz, seq, d = x.shape
    assert seq % TQ == 0 and TQ % SUB == 0
    tile = lambda b, i: (b, i, 0)
    whole = lambda b, i: (b, 0, 0)
    return pl.pallas_call(
        _mixer_kernel,
        grid=(bsz, seq // TQ),
        in_specs=[pl.BlockSpec(memory_space=pltpu.SMEM),
                  _const_spec(rel_bias.shape),
                  pl.BlockSpec((1, TQ, q.shape[2]), tile),
                  pl.BlockSpec((1,) + k.shape[1:], whole),
                  pl.BlockSpec((1,) + vt.shape[1:], whole),
                  _const_spec((1, QA)), _const_spec((1, QB)),
                  _const_spec(wo.shape), _const_spec((1, d)), _const_spec((1, d)),
                  _const_spec(mod.shape),
                  pl.BlockSpec((1, TQ, d), tile)],
        out_specs=pl.BlockSpec((1, TQ, d), tile),
        out_shape=jax.ShapeDtypeStruct((bsz, seq, d), F32),
        scratch_shapes=[pltpu.VMEM((NV_A, KV_A, WIN_A, G_A * SUB), F32),
                        pltpu.VMEM((NV_B, H_B // 2, WIN_B, 2 * SUB), F32)],
        compiler_params=pltpu.CompilerParams(dimension_semantics=("arbitrary", "arbitrary"),
                                             vmem_limit_bytes=VMEM_LIMIT),
        name="mixer",
    )(sinks, rel_bias, q, k, vt, gga.reshape(1, QA), ggb.reshape(1, QB), wo,
      bo.reshape(1, d), g_post.reshape(1, d), mod, x)


def kernel(x, c, w_ada, b_ada, g_pre_ffn1, w_gate1, w_up1, w_down1, g_post_ffn1, g_pre_mix, w_in, b_in, sinks_a, rel_bias_b, g_grp_a, g_grp_b, w_out, b_out, g_post_mix, g_pre_ffn2, w_gate2, w_up2, w_down2, g_post_ffn2):
    bsz, seq, d = x.shape
    depth = w_ada.shape[0]
    for l in range(depth):
        mod = _modulation(c, w_ada[l], b_ada[l])

        x = _ffn(x, mod, g_pre_ffn1[l], g_post_ffn1[l], w_gate1[l], w_up1[l], w_down1[l], mod0=0)

        q, k, vt = _projection(x, mod, g_pre_mix[l], w_in[l], b_in[l].reshape(1, -1))

        x = _mixer(x, mod, q, k, vt, rel_bias_b[l], sinks_a[l], g_grp_a[l], g_grp_b[l],
                   w_out[l].astype(BF16), b_out[l], g_post_mix[l])

        x = _ffn(x, mod, g_pre_ffn2[l], g_post_ffn2[l], w_gate2[l], w_up2[l], w_down2[l], mod0=6)
    return x
```

```python
import functools
import math

import jax
import jax.numpy as jnp
from jax import lax
from jax.experimental import pallas as pl
from jax.experimental.pallas import tpu as pltpu

F32 = jnp.float32
BF16 = jnp.bfloat16

CHUNK = 64
HEAD_DIM = 64
EPS = 1e-6
NEG_INF = -1e30
H_A, KV_A = 8, 2
G_A = H_A // KV_A
BACK_A = 2
H_B = 8
BACK_B = 8
REL_CLIP = 128
N_REL = 2 * REL_CLIP + 1
QA = H_A * HEAD_DIM
KVA = KV_A * HEAD_DIM
QB = H_B * HEAD_DIM
N_MOD = 9
LOG2E = math.log2(math.e)

LANES = 128
TQ = 1024
SUB = LANES
NSUB = TQ // SUB
SCORE_AHEAD = 3
FIN_SUBS = 2
FIN_LAG = 3
PAD = BACK_B * CHUNK
PAD_A = BACK_A * CHUNK
WIN_A = SUB + PAD_A
WIN_B = SUB + PAD
NV_A = PAD_A // SUB + 1
NV_B = PAD // SUB + 1
EXT_B = PAD + SUB + CHUNK
DEN_ROWS = 16
ROLL_W = 1024
TM_FFN = 1024
FFN_SLAB = 512
W_ROWS = 256
W_SLOTS = 3
TM_PROJ = 2048
PROJ_SLAB = 512
TN_MOD = 1536
FF_CHUNK = 256
VMEM_LIMIT = 56 * 1024 * 1024

_NT = (((1,), (1,)), ((), ()))
_TN = (((0,), (0,)), ((), ()))


def _rms(x, g):
    ms = jnp.mean(x * x, axis=-1, keepdims=True)
    return x * lax.rsqrt(ms + EPS) * g


def _mod_row(mod_ref, b, k):
    d = mod_ref.shape[1] // N_MOD
    return mod_ref[pl.ds(b, 1), pl.ds(k * d, d)]


def _const_spec(shape):
    nd = len(shape)
    return pl.BlockSpec(shape, lambda *_: (0,) * nd, pipeline_mode=pl.Buffered(1))


def _mod_kernel(c_ref, w_ref, b_ref, o_ref):
    c = c_ref[...]
    a = (c * jax.nn.sigmoid(c)).astype(BF16)
    o_ref[...] = jnp.dot(a, w_ref[...].astype(BF16), preferred_element_type=F32) + b_ref[...]


def _modulation(c, w_ada, b_ada):
    bsz, d = c.shape
    n = w_ada.shape[1]
    assert n % TN_MOD == 0
    return pl.pallas_call(
        _mod_kernel,
        grid=(n // TN_MOD,),
        in_specs=[pl.BlockSpec((bsz, d), lambda j: (0, 0)),
                  pl.BlockSpec((d, TN_MOD), lambda j: (0, j)),
                  pl.BlockSpec((1, TN_MOD), lambda j: (0, j))],
        out_specs=pl.BlockSpec((bsz, TN_MOD), lambda j: (0, j)),
        out_shape=jax.ShapeDtypeStruct((bsz, n), F32),
        compiler_params=pltpu.CompilerParams(dimension_semantics=("arbitrary",),
                                             vmem_limit_bytes=VMEM_LIMIT),
        name="adaln_mod",
    )(c, w_ada, b_ada.reshape(1, n))


def _load_bf16(src_hbm, dst_ref, stage, sem):
    ncol = src_hbm.shape[1]
    nblk = src_hbm.shape[0] // W_ROWS
    nslot = stage.shape[0]

    def copy(n):
        return pltpu.make_async_copy(src_hbm.at[pl.ds(n * W_ROWS, W_ROWS), :],
                                     stage.at[n % nslot, :, pl.ds(0, ncol)], sem.at[n % nslot])

    for n in range(min(nslot - 1, nblk)):
        copy(n).start()
    for n in range(nblk):
        if n + nslot - 1 < nblk:
            copy(n + nslot - 1).start()
        copy(n).wait()
        dst_ref[n * W_ROWS:(n + 1) * W_ROWS, :] = stage[n % nslot, :, 0:ncol].astype(dst_ref.dtype)


def _ffn_kernel(x_ref, mod_ref, gpre_ref, gpost_ref, wg_hbm, wu_hbm, wd_hbm, o_ref,
                wg_ref, wu_ref, wd_ref, stage, sem, *, mod0):
    @pl.when((pl.program_id(0) == 0) & (pl.program_id(1) == 0))
    def _():
        _load_bf16(wg_hbm, wg_ref, stage, sem)
        _load_bf16(wu_hbm, wu_ref, stage, sem)
        _load_bf16(wd_hbm, wd_ref, stage, sem)

    b = pl.program_id(0)
    shift = _mod_row(mod_ref, b, mod0)
    gain_in = gpre_ref[...] * (1.0 + _mod_row(mod_ref, b, mod0 + 1))
    gain_out = (0.5 * _mod_row(mod_ref, b, mod0 + 2)) * gpost_ref[...]
    d_ff = wg_ref.shape[1]
    nslab = x_ref.shape[1] // FFN_SLAB
    nchunk = d_ff // FF_CHUNK

    def rows(s):
        return slice(s * FFN_SLAB, (s + 1) * FFN_SLAB)

    def pre(s):
        return (_rms(x_ref[0, rows(s), :], gain_in) + shift).astype(BF16)

    def post(s, acc):
        o_ref[0, rows(s), :] = x_ref[0, rows(s), :] + _rms(acc, gain_out)

    h = pre(0)
    done = None
    for s in range(nslab):
        acc = jnp.zeros((FFN_SLAB, x_ref.shape[2]), F32)
        h_next = None
        for c in range(nchunk):
            if c == nchunk // 4 and done is not None:
                post(s - 1, done)
            if c == nchunk // 2 and s + 1 < nslab:
                h_next = pre(s + 1)
            f0 = c * FF_CHUNK
            g = jnp.dot(h, wg_ref[:, f0:f0 + FF_CHUNK], preferred_element_type=F32)
            u = jnp.dot(h, wu_ref[:, f0:f0 + FF_CHUNK], preferred_element_type=F32)
            a = (g * jax.nn.sigmoid(g) * u).astype(BF16)
            acc = acc + jnp.dot(a, wd_ref[f0:f0 + FF_CHUNK, :], preferred_element_type=F32)
        done, h = acc, h_next
    post(nslab - 1, done)


def _ffn(x, mod, g_pre, g_post, wg, wu, wd, *, mod0):
    bsz, seq, d = x.shape
    d_ff = wg.shape[1]
    assert seq % TM_FFN == 0 and TM_FFN % FFN_SLAB == 0 and d_ff % FF_CHUNK == 0
    assert d % W_ROWS == 0 and d_ff % W_ROWS == 0
    blk = lambda b, j: (b, j, 0)
    hbm = pl.BlockSpec(memory_space=pl.ANY)
    return pl.pallas_call(
        functools.partial(_ffn_kernel, mod0=mod0),
        grid=(bsz, seq // TM_FFN),
        in_specs=[pl.BlockSpec((1, TM_FFN, d), blk),
                  _const_spec(mod.shape),
                  _const_spec((1, d)), _const_spec((1, d)),
                  hbm, hbm, hbm],
        out_specs=pl.BlockSpec((1, TM_FFN, d), blk),
        out_shape=jax.ShapeDtypeStruct((bsz, seq, d), F32),
        scratch_shapes=[pltpu.VMEM((d, d_ff), BF16), pltpu.VMEM((d, d_ff), BF16),
                        pltpu.VMEM((d_ff, d), BF16),
                        pltpu.VMEM((W_SLOTS, W_ROWS, max(d, d_ff)), F32),
                        pltpu.SemaphoreType.DMA((W_SLOTS,))],
        compiler_params=pltpu.CompilerParams(dimension_semantics=("arbitrary", "arbitrary"),
                                             vmem_limit_bytes=VMEM_LIMIT),
        name="ffn",
    )(x, mod, g_pre.reshape(1, d), g_post.reshape(1, d), wg, wu, wd)


def _proj_kernel(x_ref, mod_ref, gpre_ref, w_hbm, b_ref, q_ref, k_ref, vt_ref, w_ref, stage, sem):
    b = pl.program_id(0)

    @pl.when((b == 0) & (pl.program_id(1) == 0))
    def _():
        _load_bf16(w_hbm, w_ref, stage, sem)

    shift = _mod_row(mod_ref, b, 3)
    gain_in = gpre_ref[...] * (1.0 + _mod_row(mod_ref, b, 4))
    nslab = x_ref.shape[1] // PROJ_SLAB

    def rows(s):
        return slice(s * PROJ_SLAB, (s + 1) * PROJ_SLAB)

    def pre(s):
        return (_rms(x_ref[0, rows(s), :], gain_in) + shift).astype(BF16)

    low = lax.broadcasted_iota(jnp.int32, (1, LANES), 1) < HEAD_DIM
    o_ka, o_va, o_qb = QA, QA + KVA, QA + 2 * KVA
    o_kb, o_vb = o_qb + QB, o_qb + 2 * QB
    h = pre(0)
    for s in range(nslab):
        p = jnp.dot(h, w_ref[...], preferred_element_type=F32) + b_ref[...]
        h_next = pre(s + 1) if s + 1 < nslab else None
        nat = [p[:, t * LANES:(t + 1) * LANES] for t in range(QA // LANES)]
        swp = [pltpu.roll(t, HEAD_DIM, 1) for t in nat]
        qa = []
        for g in range(G_A):
            t0, t1 = g // 2, G_A // 2 + g // 2
            qa.append(jnp.where(low, nat[t0], swp[t1]) if g % 2 == 0 else jnp.where(low, swp[t0], nat[t1]))
        q = jnp.concatenate(qa + [p[:, o_qb:o_kb]], axis=1)
        q_ref[0, rows(s), :] = (q * (HEAD_DIM ** -0.5 * LOG2E)).astype(BF16)
        k_ref[0, rows(s), :] = jnp.concatenate([p[:, o_ka:o_va], p[:, o_kb:o_vb]], axis=1).astype(BF16)
        v = jnp.concatenate([p[:, o_va:o_qb], p[:, o_vb:]], axis=1)
        vt_ref[0, :, rows(s)] = v.T.astype(BF16)
        h = h_next


def _projection(x, mod, g_pre, w_all, b_all):
    bsz, seq, d = x.shape
    nq, nk, nv = QA + QB, KVA + QB, KVA + QB
    assert w_all.shape == (d, nq + nk + nv)
    assert seq % TM_PROJ == 0 and TM_PROJ % PROJ_SLAB == 0 and d % W_ROWS == 0
    blk = lambda b, j: (b, j, 0)
    return pl.pallas_call(
        _proj_kernel,
        grid=(bsz, seq // TM_PROJ),
        in_specs=[pl.BlockSpec((1, TM_PROJ, d), blk),
                  _const_spec(mod.shape),
                  _const_spec((1, d)),
                  pl.BlockSpec(memory_space=pl.ANY), _const_spec(b_all.shape)],
        out_specs=[pl.BlockSpec((1, TM_PROJ, nq), blk),
                   pl.BlockSpec((1, TM_PROJ, nk), blk),
                   pl.BlockSpec((1, nv, TM_PROJ), lambda b, j: (b, 0, j))],
        out_shape=[jax.ShapeDtypeStruct((bsz, seq, nq), BF16),
                   jax.ShapeDtypeStruct((bsz, seq, nk), BF16),
                   jax.ShapeDtypeStruct((bsz, nv, seq), BF16)],
        scratch_shapes=[pltpu.VMEM(w_all.shape, BF16),
                        pltpu.VMEM((W_SLOTS, W_ROWS, w_all.shape[1]), F32),
                        pltpu.SemaphoreType.DMA((W_SLOTS,))],
        compiler_params=pltpu.CompilerParams(dimension_semantics=("arbitrary", "arbitrary"),
                                             vmem_limit_bytes=VMEM_LIMIT),
        name="qkv_proj",
    )(x, mod, g_pre.reshape(1, d), w_all, b_all)


def _bias_tiles(rel_ref, ba_ref, bb_ref):
    j = lax.broadcasted_iota(jnp.int32, (WIN_A, G_A * SUB), 0)
    r = lax.broadcasted_iota(jnp.int32, (WIN_A, G_A * SUB), 1)
    g = r // SUB
    i = r % SUB
    for k in range(KV_A):
        slope = jnp.zeros(r.shape, F32)
        for gg in range(G_A):
            slope = jnp.where(g == gg, 2.0 ** (-8.0 * (k * G_A + gg + 1) / H_A), slope)
        for v in range(NV_A):
            off = min(v * SUB, PAD_A)
            back = off // CHUNK + i // CHUNK - j // CHUNK
            bias = -slope * jnp.abs(off + i - j).astype(F32)
            ba_ref[v, k] = jnp.where((back >= 0) & (back <= BACK_A), bias * LOG2E, NEG_INF)

    table = rel_ref[...]
    dcol = lax.broadcasted_iota(jnp.int32, (H_B, ROLL_W), 1)
    d = jnp.where(dcol < SUB, dcol, dcol - ROLL_W)
    idx = jnp.clip(PAD + d, -REL_CLIP, REL_CLIP) + REL_CLIP
    row = jnp.zeros((H_B, ROLL_W), F32)
    for n in range(N_REL):
        row = jnp.where(idx == n, table[:, n:n + 1], row)
    jb = lax.broadcasted_iota(jnp.int32, (WIN_B, SUB), 0)
    ib = lax.broadcasted_iota(jnp.int32, (WIN_B, SUB), 1)
    for h in range(H_B):
        tile = jnp.broadcast_to(row[h:h + 1, :], (EXT_B, ROLL_W))
        tile = pltpu.roll(tile, 0, 1, stride=1, stride_axis=0)[:, :SUB]
        tile = jnp.concatenate([tile, jnp.zeros((PAD + WIN_B - EXT_B, SUB), F32)], axis=0)
        for v in range(NV_B):
            off = min(v * SUB, PAD)
            back = off // CHUNK + ib // CHUNK - jb // CHUNK
            window = tile[PAD - off:PAD - off + WIN_B]
            bb_ref[v, h // 2, :, pl.ds((h % 2) * SUB, SUB)] = jnp.where(
                (back >= 0) & (back <= BACK_B), window * LOG2E, NEG_INF)


def _as_columns(g_row, width):
    t = jnp.transpose(jnp.broadcast_to(g_row, (LANES, g_row.shape[1])))
    return jnp.concatenate([t] * (width // LANES), axis=1)


def _rms_rows(xt, g_col):
    ms = jnp.mean(xt * xt, axis=0, keepdims=True)
    return xt * lax.rsqrt(ms + EPS) * g_col


def _mixer_kernel(sink_ref, rel_ref, q_ref, k_ref, vt_ref, gga_ref, ggb_ref, wo_ref, bo_ref,
                  gpost_ref, mod_ref, x_ref, o_ref, ba_ref, bb_ref):
    i = pl.program_id(1)

    @pl.when((pl.program_id(0) == 0) & (i == 0))
    def _():
        _bias_tiles(rel_ref, ba_ref, bb_ref)

    ws_b = [pl.multiple_of(jnp.maximum(i * TQ + u * SUB - PAD, 0), SUB) for u in range(NSUB)]
    ws_a = [pl.multiple_of(jnp.maximum(i * TQ + u * SUB - PAD_A, 0), SUB) for u in range(NSUB)]
    var_b = [jnp.minimum(i * NSUB + u, NV_B - 1) for u in range(NSUB)]
    var_a = [jnp.minimum(i * NSUB + u, NV_A - 1) for u in range(NSUB)]

    low = lax.broadcasted_iota(jnp.int32, (1, LANES), 1) < HEAD_DIM
    halves = (low, jnp.logical_not(low))
    col_g = lax.broadcasted_iota(jnp.int32, (1, G_A * SUB), 1) // SUB
    ones_a = jnp.ones((DEN_ROWS, WIN_A), BF16)
    ones_b = jnp.ones((DEN_ROWS, WIN_B), BF16)

    def scores_a(u, k):
        k_a = k_ref[0, pl.ds(ws_a[u], WIN_A), 0:LANES]
        qs = jnp.concatenate(
            [jnp.where(halves[k], q_ref[0, u * SUB:(u + 1) * SUB, g * LANES:(g + 1) * LANES], 0)
             for g in range(G_A)], axis=0)
        s = lax.dot_general(k_a, qs, _NT, preferred_element_type=F32)
        return s + ba_ref[var_a[u], k]

    def attend_a(u, k, s):
        sink = jnp.zeros((1, G_A * SUB), F32)
        for g in range(G_A):
            sink = jnp.where(col_g == g, sink_ref[k * G_A + g] * LOG2E, sink)
        m = jnp.maximum(jnp.max(s, axis=0, keepdims=True), sink)
        e = jnp.exp2(s - m).astype(BF16)
        vt_a = jnp.concatenate([vt_ref[0, 0:LANES, pl.ds(ws_a[u], WIN_A)], ones_a], axis=0)
        o = jnp.dot(vt_a, e, preferred_element_type=F32)
        den = o[LANES:LANES + 1] + jnp.exp2(sink - m)
        o = o[k * HEAD_DIM:(k + 1) * HEAD_DIM] * (1.0 / den)
        return [o[:, g * SUB:(g + 1) * SUB] for g in range(G_A)]

    def scores_b(u, t):
        q_t = q_ref[0, u * SUB:(u + 1) * SUB, QA + t * LANES:QA + (t + 1) * LANES]
        k_t = k_ref[0, pl.ds(ws_b[u], WIN_B), (1 + t) * LANES:(2 + t) * LANES]
        q2 = jnp.concatenate([jnp.where(halves[0], q_t, 0), jnp.where(halves[1], q_t, 0)], axis=0)
        s = lax.dot_general(k_t, q2, _NT, preferred_element_type=F32)
        return s + bb_ref[var_b[u], t]

    def attend_b(u, t, s):
        m = jnp.max(s, axis=0, keepdims=True)
        e = jnp.exp2(s - m).astype(BF16)
        vt_t = jnp.concatenate([vt_ref[0, (1 + t) * LANES:(2 + t) * LANES, pl.ds(ws_b[u], WIN_B)],
                                ones_b], axis=0)
        o = jnp.dot(vt_t, e, preferred_element_type=F32)
        rden = 1.0 / o[LANES:LANES + 1]
        return [o[:HEAD_DIM, :SUB] * rden[:, :SUB], o[HEAD_DIM:LANES, SUB:] * rden[:, SUB:]]

    groups = []
    for u in range(NSUB):
        groups += [(scores_a, attend_a, u, k) for k in range(KV_A)]
        groups += [(scores_b, attend_b, u, t) for t in range(H_B // 2)]
    per_sub = len(groups) // NSUB
    outs = []
    gain_out = _mod_row(mod_ref, pl.program_id(0), 5) * gpost_ref[...]
    gain_a = _as_columns(gga_ref[...], FIN_SUBS * SUB)
    gain_b = _as_columns(ggb_ref[...], FIN_SUBS * SUB)

    def finish(first_sub):
        oa_t, ob_t = [], []
        for u in range(first_sub, first_sub + FIN_SUBS):
            sub = outs[u * per_sub:(u + 1) * per_sub]
            oa_t.append(jnp.concatenate([sub[k][g] for k in range(KV_A) for g in range(G_A)], axis=0))
            ob_t.append(jnp.concatenate([part for pair in sub[KV_A:] for part in pair], axis=0))
        oa_t = jnp.concatenate(oa_t, axis=1)
        ob_t = jnp.concatenate(ob_t, axis=1)
        y_t = jnp.concatenate([_rms_rows(oa_t, gain_a), _rms_rows(ob_t, gain_b)], axis=0)
        z = lax.dot_general(y_t.astype(BF16), wo_ref[...], _TN, preferred_element_type=F32) + bo_ref[...]
        r = slice(first_sub * SUB, (first_sub + FIN_SUBS) * SUB)
        o_ref[0, r, :] = x_ref[0, r, :] + _rms(z, gain_out)

    pending = [grp[0](*grp[2:]) for grp in groups[:SCORE_AHEAD]]
    for n, (_, attend, u, idx) in enumerate(groups):
        if n + SCORE_AHEAD < len(groups):
            nxt = groups[n + SCORE_AHEAD]
            pending.append(nxt[0](*nxt[2:]))
        outs.append(attend(u, idx, pending.pop(0)))
        done_groups = n + 1 - FIN_LAG
        if 0 < done_groups < len(groups) and done_groups % (FIN_SUBS * per_sub) == 0:
            finish(done_groups // per_sub - FIN_SUBS)
    finish(NSUB - FIN_SUBS)


def _mixer(x, mod, q, k, vt, rel_bias, sinks, gga, ggb, wo, bo, g_post):
    bsz, seq, d = x.shape
    assert seq % TQ == 0 and TQ % SUB == 0
    tile = lambda b, i: (b, i, 0)
    whole = lambda b, i: (b, 0, 0)
    return pl.pallas_call(
        _mixer_kernel,
        grid=(bsz, seq // TQ),
        in_specs=[pl.BlockSpec(memory_space=pltpu.SMEM),
                  _const_spec(rel_bias.shape),
                  pl.BlockSpec((1, TQ, q.shape[2]), tile),
                  pl.BlockSpec((1,) + k.shape[1:], whole),
                  pl.BlockSpec((1,) + vt.shape[1:], whole),
                  _const_spec((1, QA)), _const_spec((1, QB)),
                  _const_spec(wo.shape), _const_spec((1, d)), _const_spec((1, d)),
                  _const_spec(mod.shape),
                  pl.BlockSpec((1, TQ, d), tile)],
        out_specs=pl.BlockSpec((1, TQ, d), tile),
        out_shape=jax.ShapeDtypeStruct((bsz, seq, d), F32),
        scratch_shapes=[pltpu.VMEM((NV_A, KV_A, WIN_A, G_A * SUB), F32),
                        pltpu.VMEM((NV_B, H_B // 2, WIN_B, 2 * SUB), F32)],
        compiler_params=pltpu.CompilerParams(dimension_semantics=("arbitrary", "arbitrary"),
                                             vmem_limit_bytes=VMEM_LIMIT),
        name="mixer",
    )(sinks, rel_bias, q, k, vt, gga.reshape(1, QA), ggb.reshape(1, QB), wo,
      bo.reshape(1, d), g_post.reshape(1, d), mod, x)


def kernel(x, c, w_ada, b_ada, g_pre_ffn1, w_gate1, w_up1, w_down1, g_post_ffn1, g_pre_mix, w_in, b_in, sinks_a, rel_bias_b, g_grp_a, g_grp_b, w_out, b_out, g_post_mix, g_pre_ffn2, w_gate2, w_up2, w_down2, g_post_ffn2):
    bsz, seq, d = x.shape
    depth = w_ada.shape[0]
    for l in range(depth):
        mod = _modulation(c, w_ada[l], b_ada[l])

        x = _ffn(x, mod, g_pre_ffn1[l], g_post_ffn1[l], w_gate1[l], w_up1[l], w_down1[l], mod0=0)

        q, k, vt = _projection(x, mod, g_pre_mix[l], w_in[l], b_in[l].reshape(1, -1))

        x = _mixer(x, mod, q, k, vt, rel_bias_b[l], sinks_a[l], g_grp_a[l], g_grp_b[l],
                   w_out[l].astype(BF16), b_out[l], g_post_mix[l])

        x = _ffn(x, mod, g_pre_ffn2[l], g_post_ffn2[l], w_gate2[l], w_up2[l], w_down2[l], mod0=6)
    return x
```

```python
import functools
import math

import jax
import jax.numpy as jnp
from jax import lax
from jax.experimental import pallas as pl
from jax.experimental.pallas import tpu as pltpu

F32 = jnp.float32
BF16 = jnp.bfloat16

CHUNK = 64
HEAD_DIM = 64
EPS = 1e-6
NEG_INF = -1e30
H_A, KV_A = 8, 2
G_A = H_A // KV_A
BACK_A = 2
H_B = 8
BACK_B = 8
REL_CLIP = 128
N_REL = 2 * REL_CLIP + 1
QA = H_A * HEAD_DIM
KVA = KV_A * HEAD_DIM
QB = H_B * HEAD_DIM
N_MOD = 9
LOG2E = math.log2(math.e)

LANES = 128
TQ = 1024
SUB = LANES
NSUB = TQ // SUB
SCORE_AHEAD = 3
FIN_SUBS = 2
FIN_LAG = 3
PAD = BACK_B * CHUNK
PAD_A = BACK_A * CHUNK
WIN_A = SUB + PAD_A
WIN_B = SUB + PAD
NV_A = PAD_A // SUB + 1
NV_B = PAD // SUB + 1
EXT_B = PAD + SUB + CHUNK
DEN_ROWS = 16
ROLL_W = 1024
TM_FFN = 1024
FFN_SLAB = 512
W_ROWS = 256
W_SLOTS = 3
TM_PROJ = 1024
PROJ_SLAB = 512
TN_MOD = 1536
FF_CHUNK = 256
VMEM_LIMIT = 56 * 1024 * 1024

_NT = (((1,), (1,)), ((), ()))
_TN = (((0,), (0,)), ((), ()))


def _rms(x, g):
    ms = jnp.mean(x * x, axis=-1, keepdims=True)
    return x * lax.rsqrt(ms + EPS) * g


def _mod_row(mod_ref, b, k):
    d = mod_ref.shape[1] // N_MOD
    return mod_ref[pl.ds(b, 1), pl.ds(k * d, d)]


def _const_spec(shape):
    nd = len(shape)
    return pl.BlockSpec(shape, lambda *_: (0,) * nd, pipeline_mode=pl.Buffered(1))


def _mod_kernel(c_ref, w_ref, b_ref, o_ref):
    c = c_ref[...]
    a = (c * jax.nn.sigmoid(c)).astype(BF16)
    o_ref[...] = jnp.dot(a, w_ref[...].astype(BF16), preferred_element_type=F32) + b_ref[...]


def _modulation(c, w_ada, b_ada):
    bsz, d = c.shape
    n = w_ada.shape[1]
    assert n % TN_MOD == 0
    return pl.pallas_call(
        _mod_kernel,
        grid=(n // TN_MOD,),
        in_specs=[pl.BlockSpec((bsz, d), lambda j: (0, 0)),
                  pl.BlockSpec((d, TN_MOD), lambda j: (0, j)),
                  pl.BlockSpec((1, TN_MOD), lambda j: (0, j))],
        out_specs=pl.BlockSpec((bsz, TN_MOD), lambda j: (0, j)),
        out_shape=jax.ShapeDtypeStruct((bsz, n), F32),
        compiler_params=pltpu.CompilerParams(dimension_semantics=("arbitrary",),
                                             vmem_limit_bytes=VMEM_LIMIT),
        name="adaln_mod",
    )(c, w_ada, b_ada.reshape(1, n))


def _load_bf16(src_hbm, dst_ref, stage, sem):
    ncol = src_hbm.shape[1]
    nblk = src_hbm.shape[0] // W_ROWS
    nslot = stage.shape[0]

    def copy(n):
        return pltpu.make_async_copy(src_hbm.at[pl.ds(n * W_ROWS, W_ROWS), :],
                                     stage.at[n % nslot, :, pl.ds(0, ncol)], sem.at[n % nslot])

    for n in range(min(nslot - 1, nblk)):
        copy(n).start()
    for n in range(nblk):
        if n + nslot - 1 < nblk:
            copy(n + nslot - 1).start()
        copy(n).wait()
        dst_ref[n * W_ROWS:(n + 1) * W_ROWS, :] = stage[n % nslot, :, 0:ncol].astype(dst_ref.dtype)


def _ffn_kernel(x_ref, mod_ref, gpre_ref, gpost_ref, wg_hbm, wu_hbm, wd_hbm, o_ref,
                wg_ref, wu_ref, wd_ref, stage, sem, *, mod0):
    @pl.when((pl.program_id(0) == 0) & (pl.program_id(1) == 0))
    def _():
        _load_bf16(wg_hbm, wg_ref, stage, sem)
        _load_bf16(wu_hbm, wu_ref, stage, sem)
        _load_bf16(wd_hbm, wd_ref, stage, sem)

    b = pl.program_id(0)
    shift = _mod_row(mod_ref, b, mod0)
    gain_in = gpre_ref[...] * (1.0 + _mod_row(mod_ref, b, mod0 + 1))
    gain_out = (0.5 * _mod_row(mod_ref, b, mod0 + 2)) * gpost_ref[...]
    d_ff = wg_ref.shape[1]
    nslab = x_ref.shape[1] // FFN_SLAB
    nchunk = d_ff // FF_CHUNK

    def rows(s):
        return slice(s * FFN_SLAB, (s + 1) * FFN_SLAB)

    def pre(s):
        return (_rms(x_ref[0, rows(s), :], gain_in) + shift).astype(BF16)

    def post(s, acc):
        o_ref[0, rows(s), :] = x_ref[0, rows(s), :] + _rms(acc, gain_out)

    h = pre(0)
    done = None
    for s in range(nslab):
        acc = jnp.zeros((FFN_SLAB, x_ref.shape[2]), F32)
        h_next = None
        for c in range(nchunk):
            if c == nchunk // 4 and done is not None:
                post(s - 1, done)
            if c == nchunk // 2 and s + 1 < nslab:
                h_next = pre(s + 1)
            f0 = c * FF_CHUNK
            g = jnp.dot(h, wg_ref[:, f0:f0 + FF_CHUNK], preferred_element_type=F32)
            u = jnp.dot(h, wu_ref[:, f0:f0 + FF_CHUNK], preferred_element_type=F32)
            a = (g * jax.nn.sigmoid(g) * u).astype(BF16)
            acc = acc + jnp.dot(a, wd_ref[f0:f0 + FF_CHUNK, :], preferred_element_type=F32)
        done, h = acc, h_next
    post(nslab - 1, done)


def _ffn(x, mod, g_pre, g_post, wg, wu, wd, *, mod0):
    bsz, seq, d = x.shape
    d_ff = wg.shape[1]
    assert seq % TM_FFN == 0 and TM_FFN % FFN_SLAB == 0 and d_ff % FF_CHUNK == 0
    assert d % W_ROWS == 0 and d_ff % W_ROWS == 0
    blk = lambda b, j: (b, j, 0)
    hbm = pl.BlockSpec(memory_space=pl.ANY)
    return pl.pallas_call(
        functools.partial(_ffn_kernel, mod0=mod0),
        grid=(bsz, seq // TM_FFN),
        in_specs=[pl.BlockSpec((1, TM_FFN, d), blk),
                  _const_spec(mod.shape),
                  _const_spec((1, d)), _const_spec((1, d)),
                  hbm, hbm, hbm],
        out_specs=pl.BlockSpec((1, TM_FFN, d), blk),
        out_shape=jax.ShapeDtypeStruct((bsz, seq, d), F32),
        scratch_shapes=[pltpu.VMEM((d, d_ff), BF16), pltpu.VMEM((d, d_ff), BF16),
                        pltpu.VMEM((d_ff, d), BF16),
                        pltpu.VMEM((W_SLOTS, W_ROWS, max(d, d_ff)), F32),
                        pltpu.SemaphoreType.DMA((W_SLOTS,))],
        compiler_params=pltpu.CompilerParams(dimension_semantics=("arbitrary", "arbitrary"),
                                             vmem_limit_bytes=VMEM_LIMIT),
        name="ffn",
    )(x, mod, g_pre.reshape(1, d), g_post.reshape(1, d), wg, wu, wd)


def _proj_kernel(x_ref, mod_ref, gpre_ref, w_hbm, b_ref, q_ref, k_ref, vt_ref, w_ref, stage, sem):
    b = pl.program_id(0)

    @pl.when((b == 0) & (pl.program_id(1) == 0))
    def _():
        _load_bf16(w_hbm, w_ref, stage, sem)

    shift = _mod_row(mod_ref, b, 3)
    gain_in = gpre_ref[...] * (1.0 + _mod_row(mod_ref, b, 4))
    nslab = x_ref.shape[1] // PROJ_SLAB

    def rows(s):
        return slice(s * PROJ_SLAB, (s + 1) * PROJ_SLAB)

    def pre(s):
        return (_rms(x_ref[0, rows(s), :], gain_in) + shift).astype(BF16)

    low = lax.broadcasted_iota(jnp.int32, (1, LANES), 1) < HEAD_DIM
    o_ka, o_va, o_qb = QA, QA + KVA, QA + 2 * KVA
    o_kb, o_vb = o_qb + QB, o_qb + 2 * QB
    h = pre(0)
    for s in range(nslab):
        p = jnp.dot(h, w_ref[...], preferred_element_type=F32) + b_ref[...]
        h_next = pre(s + 1) if s + 1 < nslab else None
        nat = [p[:, t * LANES:(t + 1) * LANES] for t in range(QA // LANES)]
        swp = [pltpu.roll(t, HEAD_DIM, 1) for t in nat]
        qa = []
        for g in range(G_A):
            t0, t1 = g // 2, G_A // 2 + g // 2
            qa.append(jnp.where(low, nat[t0], swp[t1]) if g % 2 == 0 else jnp.where(low, swp[t0], nat[t1]))
        q = jnp.concatenate(qa + [p[:, o_qb:o_kb]], axis=1)
        q_ref[0, rows(s), :] = (q * (HEAD_DIM ** -0.5 * LOG2E)).astype(BF16)
        k_ref[0, rows(s), :] = jnp.concatenate([p[:, o_ka:o_va], p[:, o_kb:o_vb]], axis=1).astype(BF16)
        v = jnp.concatenate([p[:, o_va:o_qb], p[:, o_vb:]], axis=1)
        vt_ref[0, :, rows(s)] = v.T.astype(BF16)
        h = h_next


def _projection(x, mod, g_pre, w_all, b_all):
    bsz, seq, d = x.shape
    nq, nk, nv = QA + QB, KVA + QB, KVA + QB
    assert w_all.shape == (d, nq + nk + nv)
    assert seq % TM_PROJ == 0 and TM_PROJ % PROJ_SLAB == 0 and d % W_ROWS == 0
    blk = lambda b, j: (b, j, 0)
    return pl.pallas_call(
        _proj_kernel,
        grid=(bsz, seq // TM_PROJ),
        in_specs=[pl.BlockSpec((1, TM_PROJ, d), blk),
                  _const_spec(mod.shape),
                  _const_spec((1, d)),
                  pl.BlockSpec(memory_space=pl.ANY), _const_spec(b_all.shape)],
        out_specs=[pl.BlockSpec((1, TM_PROJ, nq), blk),
                   pl.BlockSpec((1, TM_PROJ, nk), blk),
                   pl.BlockSpec((1, nv, TM_PROJ), lambda b, j: (b, 0, j))],
        out_shape=[jax.ShapeDtypeStruct((bsz, seq, nq), BF16),
                   jax.ShapeDtypeStruct((bsz, seq, nk), BF16),
                   jax.ShapeDtypeStruct((bsz, nv, seq), BF16)],
        scratch_shapes=[pltpu.VMEM(w_all.shape, BF16),
                        pltpu.VMEM((W_SLOTS, W_ROWS, w_all.shape[1]), F32),
                        pltpu.SemaphoreType.DMA((W_SLOTS,))],
        compiler_params=pltpu.CompilerParams(dimension_semantics=("arbitrary", "arbitrary"),
                                             vmem_limit_bytes=VMEM_LIMIT),
        name="qkv_proj",
    )(x, mod, g_pre.reshape(1, d), w_all, b_all)


def _bias_tiles(rel_ref, ba_ref, bb_ref):
    j = lax.broadcasted_iota(jnp.int32, (WIN_A, G_A * SUB), 0)
    r = lax.broadcasted_iota(jnp.int32, (WIN_A, G_A * SUB), 1)
    g = r // SUB
    i = r % SUB
    for k in range(KV_A):
        slope = jnp.zeros(r.shape, F32)
        for gg in range(G_A):
            slope = jnp.where(g == gg, 2.0 ** (-8.0 * (k * G_A + gg + 1) / H_A), slope)
        for v in range(NV_A):
            off = min(v * SUB, PAD_A)
            back = off // CHUNK + i // CHUNK - j // CHUNK
            bias = -slope * jnp.abs(off + i - j).astype(F32)
            ba_ref[v, k] = jnp.where((back >= 0) & (back <= BACK_A), bias * LOG2E, NEG_INF)

    table = rel_ref[...]
    dcol = lax.broadcasted_iota(jnp.int32, (H_B, ROLL_W), 1)
    d = jnp.where(dcol < SUB, dcol, dcol - ROLL_W)
    idx = jnp.clip(PAD + d, -REL_CLIP, REL_CLIP) + REL_CLIP
    row = jnp.zeros((H_B, ROLL_W), F32)
    for n in range(N_REL):
        row = jnp.where(idx == n, table[:, n:n + 1], row)
    jb = lax.broadcasted_iota(jnp.int32, (WIN_B, SUB), 0)
    ib = lax.broadcasted_iota(jnp.int32, (WIN_B, SUB), 1)
    for h in range(H_B):
        tile = jnp.broadcast_to(row[h:h + 1, :], (EXT_B, ROLL_W))
        tile = pltpu.roll(tile, 0, 1, stride=1, stride_axis=0)[:, :SUB]
        tile = jnp.concatenate([tile, jnp.zeros((PAD + WIN_B - EXT_B, SUB), F32)], axis=0)
        for v in range(NV_B):
            off = min(v * SUB, PAD)
            back = off // CHUNK + ib // CHUNK - jb // CHUNK
            window = tile[PAD - off:PAD - off + WIN_B]
            bb_ref[v, h // 2, :, pl.ds((h % 2) * SUB, SUB)] = jnp.where(
                (back >= 0) & (back <= BACK_B), window * LOG2E, NEG_INF)


def _as_columns(g_row, width):
    t = jnp.transpose(jnp.broadcast_to(g_row, (LANES, g_row.shape[1])))
    return jnp.concatenate([t] * (width // LANES), axis=1)


def _rms_rows(xt, g_col):
    ms = jnp.mean(xt * xt, axis=0, keepdims=True)
    return xt * lax.rsqrt(ms + EPS) * g_col


def _mixer_kernel(sink_ref, rel_ref, q_ref, k_ref, vt_ref, gga_ref, ggb_ref, wo_ref, bo_ref,
                  gpost_ref, mod_ref, x_ref, o_ref, ba_ref, bb_ref):
    i = pl.program_id(1)

    @pl.when((pl.program_id(0) == 0) & (i == 0))
    def _():
        _bias_tiles(rel_ref, ba_ref, bb_ref)

    ws_b = [pl.multiple_of(jnp.maximum(i * TQ + u * SUB - PAD, 0), SUB) for u in range(NSUB)]
    ws_a = [pl.multiple_of(jnp.maximum(i * TQ + u * SUB - PAD_A, 0), SUB) for u in range(NSUB)]
    var_b = [jnp.minimum(i * NSUB + u, NV_B - 1) for u in range(NSUB)]
    var_a = [jnp.minimum(i * NSUB + u, NV_A - 1) for u in range(NSUB)]

    low = lax.broadcasted_iota(jnp.int32, (1, LANES), 1) < HEAD_DIM
    halves = (low, jnp.logical_not(low))
    col_g = lax.broadcasted_iota(jnp.int32, (1, G_A * SUB), 1) // SUB
    ones_a = jnp.ones((DEN_ROWS, WIN_A), BF16)
    ones_b = jnp.ones((DEN_ROWS, WIN_B), BF16)

    def scores_a(u, k):
        k_a = k_ref[0, pl.ds(ws_a[u], WIN_A), 0:LANES]
        qs = jnp.concatenate(
            [jnp.where(halves[k], q_ref[0, u * SUB:(u + 1) * SUB, g * LANES:(g + 1) * LANES], 0)
             for g in range(G_A)], axis=0)
        s = lax.dot_general(k_a, qs, _NT, preferred_element_type=F32)
        return s + ba_ref[var_a[u], k]

    def attend_a(u, k, s):
        sink = jnp.zeros((1, G_A * SUB), F32)
        for g in range(G_A):
            sink = jnp.where(col_g == g, sink_ref[k * G_A + g] * LOG2E, sink)
        m = jnp.maximum(jnp.max(s, axis=0, keepdims=True), sink)
        e = jnp.exp2(s - m).astype(BF16)
        vt_a = jnp.concatenate([vt_ref[0, 0:LANES, pl.ds(ws_a[u], WIN_A)], ones_a], axis=0)
        o = jnp.dot(vt_a, e, preferred_element_type=F32)
        den = o[LANES:LANES + 1] + jnp.exp2(sink - m)
        o = o[k * HEAD_DIM:(k + 1) * HEAD_DIM] * (1.0 / den)
        return [o[:, g * SUB:(g + 1) * SUB] for g in range(G_A)]

    def scores_b(u, t):
        q_t = q_ref[0, u * SUB:(u + 1) * SUB, QA + t * LANES:QA + (t + 1) * LANES]
        k_t = k_ref[0, pl.ds(ws_b[u], WIN_B), (1 + t) * LANES:(2 + t) * LANES]
        q2 = jnp.concatenate([jnp.where(halves[0], q_t, 0), jnp.where(halves[1], q_t, 0)], axis=0)
        s = lax.dot_general(k_t, q2, _NT, preferred_element_type=F32)
        return s + bb_ref[var_b[u], t]

    def attend_b(u, t, s):
        m = jnp.max(s, axis=0, keepdims=True)
        e = jnp.exp2(s - m).astype(BF16)
        vt_t = jnp.concatenate([vt_ref[0, (1 + t) * LANES:(2 + t) * LANES, pl.ds(ws_b[u], WIN_B)],
                                ones_b], axis=0)
        o = jnp.dot(vt_t, e, preferred_element_type=F32)
        rden = 1.0 / o[LANES:LANES + 1]
        return [o[:HEAD_DIM, :SUB] * rden[:, :SUB], o[HEAD_DIM:LANES, SUB:] * rden[:, SUB:]]

    groups = []
    for u in range(NSUB):
        groups += [(scores_a, attend_a, u, k) for k in range(KV_A)]
        groups += [(scores_b, attend_b, u, t) for t in range(H_B // 2)]
    per_sub = len(groups) // NSUB
    outs = []
    gain_out = _mod_row(mod_ref, pl.program_id(0), 5) * gpost_ref[...]
    gain_a = _as_columns(gga_ref[...], FIN_SUBS * SUB)
    gain_b = _as_columns(ggb_ref[...], FIN_SUBS * SUB)

    def finish(first_sub):
        oa_t, ob_t = [], []
        for u in range(first_sub, first_sub + FIN_SUBS):
            sub = outs[u * per_sub:(u + 1) * per_sub]
            oa_t.append(jnp.concatenate([sub[k][g] for k in range(KV_A) for g in range(G_A)], axis=0))
            ob_t.append(jnp.concatenate([part for pair in sub[KV_A:] for part in pair], axis=0))
        oa_t = jnp.concatenate(oa_t, axis=1)
        ob_t = jnp.concatenate(ob_t, axis=1)
        y_t = jnp.concatenate([_rms_rows(oa_t, gain_a), _rms_rows(ob_t, gain_b)], axis=0)
        z = lax.dot_general(y_t.astype(BF16), wo_ref[...], _TN, preferred_element_type=F32) + bo_ref[...]
        r = slice(first_sub * SUB, (first_sub + FIN_SUBS) * SUB)
        o_ref[0, r, :] = x_ref[0, r, :] + _rms(z, gain_out)

    pending = [grp[0](*grp[2:]) for grp in groups[:SCORE_AHEAD]]
    for n, (_, attend, u, idx) in enumerate(groups):
        if n + SCORE_AHEAD < len(groups):
            nxt = groups[n + SCORE_AHEAD]
            pending.append(nxt[0](*nxt[2:]))
        outs.append(attend(u, idx, pending.pop(0)))
        done_groups = n + 1 - FIN_LAG
        if 0 < done_groups < len(groups) and done_groups % (FIN_SUBS * per_sub) == 0:
            finish(done_groups // per_sub - FIN_SUBS)
    finish(NSUB - FIN_SUBS)


def _mixer(x, mod, q, k, vt, rel_bias, sinks, gga, ggb, wo, bo, g_post):
    bsz, seq, d = x.shape
    assert seq % TQ == 0 and TQ % SUB == 0
    tile = lambda b, i: (b, i, 0)
    whole = lambda b, i: (b, 0, 0)
    return pl.pallas_call(
        _mixer_kernel,
        grid=(bsz, seq // TQ),
        in_specs=[pl.BlockSpec(memory_space=pltpu.SMEM),
                  _const_spec(rel_bias.shape),
                  pl.BlockSpec((1, TQ, q.shape[2]), tile),
                  pl.BlockSpec((1,) + k.shape[1:], whole),
                  pl.BlockSpec((1,) + vt.shape[1:], whole),
                  _const_spec((1, QA)), _const_spec((1, QB)),
                  _const_spec(wo.shape), _const_spec((1, d)), _const_spec((1, d)),
                  _const_spec(mod.shape),
                  pl.BlockSpec((1, TQ, d), tile)],
        out_specs=pl.BlockSpec((1, TQ, d), tile),
        out_shape=jax.ShapeDtypeStruct((bsz, seq, d), F32),
        scratch_shapes=[pltpu.VMEM((NV_A, KV_A, WIN_A, G_A * SUB), F32),
                        pltpu.VMEM((NV_B, H_B // 2, WIN_B, 2 * SUB), F32)],
        compiler_params=pltpu.CompilerParams(dimension_semantics=("arbitrary", "arbitrary"),
                                             vmem_limit_bytes=VMEM_LIMIT),
        name="mixer",
    )(sinks, rel_bias, q, k, vt, gga.reshape(1, QA), ggb.reshape(1, QB), wo,
      bo.reshape(1, d), g_post.reshape(1, d), mod, x)


def kernel(x, c, w_ada, b_ada, g_pre_ffn1, w_gate1, w_up1, w_down1, g_post_ffn1, g_pre_mix, w_in, b_in, sinks_a, rel_bias_b, g_grp_a, g_grp_b, w_out, b_out, g_post_mix, g_pre_ffn2, w_gate2, w_up2, w_down2, g_post_ffn2):
    bsz, seq, d = x.shape
    depth = w_ada.shape[0]
    for l in range(depth):
        mod = _modulation(c, w_ada[l], b_ada[l])

        x = _ffn(x, mod, g_pre_ffn1[l], g_post_ffn1[l], w_gate1[l], w_up1[l], w_down1[l], mod0=0)

        q, k, vt = _projection(x, mod, g_pre_mix[l], w_in[l], b_in[l].reshape(1, -1))

        x = _mixer(x, mod, q, k, vt, rel_bias_b[l], sinks_a[l], g_grp_a[l], g_grp_b[l],
                   w_out[l].astype(BF16), b_out[l], g_post_mix[l])

        x = _ffn(x, mod, g_pre_ffn2[l], g_post_ffn2[l], w_gate2[l], w_up2[l], w_down2[l], mod0=6)
    return x
```

```python
import functools
import math

import jax
import jax.numpy as jnp
from jax import lax
from jax.experimental import pallas as pl
from jax.experimental.pallas import tpu as pltpu

F32 = jnp.float32
BF16 = jnp.bfloat16

CHUNK = 64
HEAD_DIM = 64
EPS = 1e-6
NEG_INF = -1e30
H_A, KV_A = 8, 2
G_A = H_A // KV_A
BACK_A = 2
H_B = 8
BACK_B = 8
REL_CLIP = 128
N_REL = 2 * REL_CLIP + 1
QA = H_A * HEAD_DIM
KVA = KV_A * HEAD_DIM
QB = H_B * HEAD_DIM
N_MOD = 9
LOG2E = math.log2(math.e)

LANES = 128
TQ = 1024
SUB = LANES
NSUB = TQ // SUB
SCORE_AHEAD = 3
FIN_SUBS = 2
FIN_LAG = 3
PAD = BACK_B * CHUNK
PAD_A = BACK_A * CHUNK
WIN_A = SUB + PAD_A
WIN_B = SUB + PAD
NV_A = PAD_A // SUB + 1
NV_B = PAD // SUB + 1
EXT_B = PAD + SUB + CHUNK
DEN_ROWS = 16
ROLL_W = 1024
TM_FFN = 1024
FFN_SLAB = 512
W_ROWS = 256
W_SLOTS = 3
TM_PROJ = 1024
PROJ_SLAB = 512
TN_MOD = 1536
FF_CHUNK = 256
VMEM_LIMIT = 56 * 1024 * 1024

_NT = (((1,), (1,)), ((), ()))
_TN = (((0,), (0,)), ((), ()))


def _rms(x, g):
    ms = jnp.mean(x * x, axis=-1, keepdims=True)
    return x * lax.rsqrt(ms + EPS) * g


def _mod_row(mod_ref, b, k):
    d = mod_ref.shape[1] // N_MOD
    return mod_ref[pl.ds(b, 1), pl.ds(k * d, d)]


def _const_spec(shape):
    nd = len(shape)
    return pl.BlockSpec(shape, lambda *_: (0,) * nd, pipeline_mode=pl.Buffered(1))


def _mod_kernel(c_ref, w_ref, b_ref, o_ref):
    c = c_ref[...]
    a = (c * jax.nn.sigmoid(c)).astype(BF16)
    o_ref[...] = jnp.dot(a, w_ref[...].astype(BF16), preferred_element_type=F32) + b_ref[...]


def _modulation(c, w_ada, b_ada):
    bsz, d = c.shape
    n = w_ada.shape[1]
    assert n % TN_MOD == 0
    return pl.pallas_call(
        _mod_kernel,
        grid=(n // TN_MOD,),
        in_specs=[pl.BlockSpec((bsz, d), lambda j: (0, 0)),
                  pl.BlockSpec((d, TN_MOD), lambda j: (0, j)),
                  pl.BlockSpec((1, TN_MOD), lambda j: (0, j))],
        out_specs=pl.BlockSpec((bsz, TN_MOD), lambda j: (0, j)),
        out_shape=jax.ShapeDtypeStruct((bsz, n), F32),
        compiler_params=pltpu.CompilerParams(dimension_semantics=("arbitrary",),
                                             vmem_limit_bytes=VMEM_LIMIT),
        name="adaln_mod",
    )(c, w_ada, b_ada.reshape(1, n))


def _load_bf16(src_hbm, dst_ref, stage, sem):
    ncol = src_hbm.shape[1]
    nblk = src_hbm.shape[0] // W_ROWS
    nslot = stage.shape[0]

    def copy(n):
        return pltpu.make_async_copy(src_hbm.at[pl.ds(n * W_ROWS, W_ROWS), :],
                                     stage.at[n % nslot, :, pl.ds(0, ncol)], sem.at[n % nslot])

    for n in range(min(nslot - 1, nblk)):
        copy(n).start()
    for n in range(nblk):
        if n + nslot - 1 < nblk:
            copy(n + nslot - 1).start()
        copy(n).wait()
        dst_ref[n * W_ROWS:(n + 1) * W_ROWS, :] = stage[n % nslot, :, 0:ncol].astype(dst_ref.dtype)


def _ffn_kernel(x_ref, mod_ref, gpre_ref, gpost_ref, wg_hbm, wu_hbm, wd_hbm, o_ref,
                wg_ref, wu_ref, wd_ref, stage, sem, *, mod0):
    @pl.when((pl.program_id(0) == 0) & (pl.program_id(1) == 0))
    def _():
        _load_bf16(wg_hbm, wg_ref, stage, sem)
        _load_bf16(wu_hbm, wu_ref, stage, sem)
        _load_bf16(wd_hbm, wd_ref, stage, sem)

    b = pl.program_id(0)
    shift = _mod_row(mod_ref, b, mod0)
    gain_in = gpre_ref[...] * (1.0 + _mod_row(mod_ref, b, mod0 + 1))
    gain_out = (0.5 * _mod_row(mod_ref, b, mod0 + 2)) * gpost_ref[...]
    d_ff = wg_ref.shape[1]
    nslab = x_ref.shape[1] // FFN_SLAB
    nchunk = d_ff // FF_CHUNK

    def rows(s):
        return slice(s * FFN_SLAB, (s + 1) * FFN_SLAB)

    def pre(s):
        return (_rms(x_ref[0, rows(s), :], gain_in) + shift).astype(BF16)

    def post(s, acc):
        o_ref[0, rows(s), :] = x_ref[0, rows(s), :] + _rms(acc, gain_out)

    h = pre(0)
    done = None
    for s in range(nslab):
        acts = []
        h_next = None
        for c in range(nchunk):
            if c == nchunk // 4 and done is not None:
                post(s - 1, done)
            if c == nchunk // 2 and s + 1 < nslab:
                h_next = pre(s + 1)
            f0 = c * FF_CHUNK
            g = jnp.dot(h, wg_ref[:, f0:f0 + FF_CHUNK], preferred_element_type=F32)
            u = jnp.dot(h, wu_ref[:, f0:f0 + FF_CHUNK], preferred_element_type=F32)
            acts.append((g * jax.nn.sigmoid(g) * u).astype(BF16))
        done = jnp.dot(jnp.concatenate(acts, axis=1), wd_ref[...], preferred_element_type=F32)
        h = h_next
    post(nslab - 1, done)


def _ffn(x, mod, g_pre, g_post, wg, wu, wd, *, mod0):
    bsz, seq, d = x.shape
    d_ff = wg.shape[1]
    assert seq % TM_FFN == 0 and TM_FFN % FFN_SLAB == 0 and d_ff % FF_CHUNK == 0
    assert d % W_ROWS == 0 and d_ff % W_ROWS == 0
    blk = lambda b, j: (b, j, 0)
    hbm = pl.BlockSpec(memory_space=pl.ANY)
    return pl.pallas_call(
        functools.partial(_ffn_kernel, mod0=mod0),
        grid=(bsz, seq // TM_FFN),
        in_specs=[pl.BlockSpec((1, TM_FFN, d), blk),
                  _const_spec(mod.shape),
                  _const_spec((1, d)), _const_spec((1, d)),
                  hbm, hbm, hbm],
        out_specs=pl.BlockSpec((1, TM_FFN, d), blk),
        out_shape=jax.ShapeDtypeStruct((bsz, seq, d), F32),
        scratch_shapes=[pltpu.VMEM((d, d_ff), BF16), pltpu.VMEM((d, d_ff), BF16),
                        pltpu.VMEM((d_ff, d), BF16),
                        pltpu.VMEM((W_SLOTS, W_ROWS, max(d, d_ff)), F32),
                        pltpu.SemaphoreType.DMA((W_SLOTS,))],
        compiler_params=pltpu.CompilerParams(dimension_semantics=("arbitrary", "arbitrary"),
                                             vmem_limit_bytes=VMEM_LIMIT),
        name="ffn",
    )(x, mod, g_pre.reshape(1, d), g_post.reshape(1, d), wg, wu, wd)


def _proj_kernel(x_ref, mod_ref, gpre_ref, w_hbm, b_ref, q_ref, k_ref, vt_ref, w_ref, stage, sem):
    b = pl.program_id(0)

    @pl.when((b == 0) & (pl.program_id(1) == 0))
    def _():
        _load_bf16(w_hbm, w_ref, stage, sem)

    shift = _mod_row(mod_ref, b, 3)
    gain_in = gpre_ref[...] * (1.0 + _mod_row(mod_ref, b, 4))
    nslab = x_ref.shape[1] // PROJ_SLAB

    def rows(s):
        return slice(s * PROJ_SLAB, (s + 1) * PROJ_SLAB)

    def pre(s):
        return (_rms(x_ref[0, rows(s), :], gain_in) + shift).astype(BF16)

    low = lax.broadcasted_iota(jnp.int32, (1, LANES), 1) < HEAD_DIM
    o_ka, o_va, o_qb = QA, QA + KVA, QA + 2 * KVA
    o_kb, o_vb = o_qb + QB, o_qb + 2 * QB
    h = pre(0)
    for s in range(nslab):
        p = jnp.dot(h, w_ref[...], preferred_element_type=F32) + b_ref[...]
        h_next = pre(s + 1) if s + 1 < nslab else None
        nat = [p[:, t * LANES:(t + 1) * LANES] for t in range(QA // LANES)]
        swp = [pltpu.roll(t, HEAD_DIM, 1) for t in nat]
        qa = []
        for g in range(G_A):
            t0, t1 = g // 2, G_A // 2 + g // 2
            qa.append(jnp.where(low, nat[t0], swp[t1]) if g % 2 == 0 else jnp.where(low, swp[t0], nat[t1]))
        q = jnp.concatenate(qa + [p[:, o_qb:o_kb]], axis=1)
        q_ref[0, rows(s), :] = (q * (HEAD_DIM ** -0.5 * LOG2E)).astype(BF16)
        k_ref[0, rows(s), :] = jnp.concatenate([p[:, o_ka:o_va], p[:, o_kb:o_vb]], axis=1).astype(BF16)
        v = jnp.concatenate([p[:, o_va:o_qb], p[:, o_vb:]], axis=1)
        vt_ref[0, :, rows(s)] = v.T.astype(BF16)
        h = h_next


def _projection(x, mod, g_pre, w_all, b_all):
    bsz, seq, d = x.shape
    nq, nk, nv = QA + QB, KVA + QB, KVA + QB
    assert w_all.shape == (d, nq + nk + nv)
    assert seq % TM_PROJ == 0 and TM_PROJ % PROJ_SLAB == 0 and d % W_ROWS == 0
    blk = lambda b, j: (b, j, 0)
    return pl.pallas_call(
        _proj_kernel,
        grid=(bsz, seq // TM_PROJ),
        in_specs=[pl.BlockSpec((1, TM_PROJ, d), blk),
                  _const_spec(mod.shape),
                  _const_spec((1, d)),
                  pl.BlockSpec(memory_space=pl.ANY), _const_spec(b_all.shape)],
        out_specs=[pl.BlockSpec((1, TM_PROJ, nq), blk),
                   pl.BlockSpec((1, TM_PROJ, nk), blk),
                   pl.BlockSpec((1, nv, TM_PROJ), lambda b, j: (b, 0, j))],
        out_shape=[jax.ShapeDtypeStruct((bsz, seq, nq), BF16),
                   jax.ShapeDtypeStruct((bsz, seq, nk), BF16),
                   jax.ShapeDtypeStruct((bsz, nv, seq), BF16)],
        scratch_shapes=[pltpu.VMEM(w_all.shape, BF16),
                        pltpu.VMEM((W_SLOTS, W_ROWS, w_all.shape[1]), F32),
                        pltpu.SemaphoreType.DMA((W_SLOTS,))],
        compiler_params=pltpu.CompilerParams(dimension_semantics=("arbitrary", "arbitrary"),
                                             vmem_limit_bytes=VMEM_LIMIT),
        name="qkv_proj",
    )(x, mod, g_pre.reshape(1, d), w_all, b_all)


def _bias_tiles(rel_ref, ba_ref, bb_ref):
    j = lax.broadcasted_iota(jnp.int32, (WIN_A, G_A * SUB), 0)
    r = lax.broadcasted_iota(jnp.int32, (WIN_A, G_A * SUB), 1)
    g = r // SUB
    i = r % SUB
    for k in range(KV_A):
        slope = jnp.zeros(r.shape, F32)
        for gg in range(G_A):
            slope = jnp.where(g == gg, 2.0 ** (-8.0 * (k * G_A + gg + 1) / H_A), slope)
        for v in range(NV_A):
            off = min(v * SUB, PAD_A)
            back = off // CHUNK + i // CHUNK - j // CHUNK
            bias = -slope * jnp.abs(off + i - j).astype(F32)
            ba_ref[v, k] = jnp.where((back >= 0) & (back <= BACK_A), bias * LOG2E, NEG_INF)

    table = rel_ref[...]
    dcol = lax.broadcasted_iota(jnp.int32, (H_B, ROLL_W), 1)
    d = jnp.where(dcol < SUB, dcol, dcol - ROLL_W)
    idx = jnp.clip(PAD + d, -REL_CLIP, REL_CLIP) + REL_CLIP
    row = jnp.zeros((H_B, ROLL_W), F32)
    for n in range(N_REL):
        row = jnp.where(idx == n, table[:, n:n + 1], row)
    jb = lax.broadcasted_iota(jnp.int32, (WIN_B, SUB), 0)
    ib = lax.broadcasted_iota(jnp.int32, (WIN_B, SUB), 1)
    for h in range(H_B):
        tile = jnp.broadcast_to(row[h:h + 1, :], (EXT_B, ROLL_W))
        tile = pltpu.roll(tile, 0, 1, stride=1, stride_axis=0)[:, :SUB]
        tile = jnp.concatenate([tile, jnp.zeros((PAD + WIN_B - EXT_B, SUB), F32)], axis=0)
        for v in range(NV_B):
            off = min(v * SUB, PAD)
            back = off // CHUNK + ib // CHUNK - jb // CHUNK
            window = tile[PAD - off:PAD - off + WIN_B]
            bb_ref[v, h // 2, :, pl.ds((h % 2) * SUB, SUB)] = jnp.where(
                (back >= 0) & (back <= BACK_B), window * LOG2E, NEG_INF)


def _as_columns(g_row, width):
    t = jnp.transpose(jnp.broadcast_to(g_row, (LANES, g_row.shape[1])))
    return jnp.concatenate([t] * (width // LANES), axis=1)


def _rms_rows(xt, g_col):
    ms = jnp.mean(xt * xt, axis=0, keepdims=True)
    return xt * lax.rsqrt(ms + EPS) * g_col


def _mixer_kernel(sink_ref, rel_ref, q_ref, k_ref, vt_ref, gga_ref, ggb_ref, wo_ref, bo_ref,
                  gpost_ref, mod_ref, x_ref, o_ref, ba_ref, bb_ref):
    i = pl.program_id(1)

    @pl.when((pl.program_id(0) == 0) & (i == 0))
    def _():
        _bias_tiles(rel_ref, ba_ref, bb_ref)

    ws_b = [pl.multiple_of(jnp.maximum(i * TQ + u * SUB - PAD, 0), SUB) for u in range(NSUB)]
    ws_a = [pl.multiple_of(jnp.maximum(i * TQ + u * SUB - PAD_A, 0), SUB) for u in range(NSUB)]
    var_b = [jnp.minimum(i * NSUB + u, NV_B - 1) for u in range(NSUB)]
    var_a = [jnp.minimum(i * NSUB + u, NV_A - 1) for u in range(NSUB)]

    low = lax.broadcasted_iota(jnp.int32, (1, LANES), 1) < HEAD_DIM
    halves = (low, jnp.logical_not(low))
    col_g = lax.broadcasted_iota(jnp.int32, (1, G_A * SUB), 1) // SUB
    ones_a = jnp.ones((DEN_ROWS, WIN_A), BF16)
    ones_b = jnp.ones((DEN_ROWS, WIN_B), BF16)

    def scores_a(u, k):
        k_a = k_ref[0, pl.ds(ws_a[u], WIN_A), 0:LANES]
        qs = jnp.concatenate(
            [jnp.where(halves[k], q_ref[0, u * SUB:(u + 1) * SUB, g * LANES:(g + 1) * LANES], 0)
             for g in range(G_A)], axis=0)
        s = lax.dot_general(k_a, qs, _NT, preferred_element_type=F32)
        return s + ba_ref[var_a[u], k]

    def attend_a(u, k, s):
        sink = jnp.zeros((1, G_A * SUB), F32)
        for g in range(G_A):
            sink = jnp.where(col_g == g, sink_ref[k * G_A + g] * LOG2E, sink)
        m = jnp.maximum(jnp.max(s, axis=0, keepdims=True), sink)
        e = jnp.exp2(s - m).astype(BF16)
        vt_a = jnp.concatenate([vt_ref[0, 0:LANES, pl.ds(ws_a[u], WIN_A)], ones_a], axis=0)
        o = jnp.dot(vt_a, e, preferred_element_type=F32)
        den = o[LANES:LANES + 1] + jnp.exp2(sink - m)
        o = o[k * HEAD_DIM:(k + 1) * HEAD_DIM] * (1.0 / den)
        return [o[:, g * SUB:(g + 1) * SUB] for g in range(G_A)]

    def scores_b(u, t):
        q_t = q_ref[0, u * SUB:(u + 1) * SUB, QA + t * LANES:QA + (t + 1) * LANES]
        k_t = k_ref[0, pl.ds(ws_b[u], WIN_B), (1 + t) * LANES:(2 + t) * LANES]
        q2 = jnp.concatenate([jnp.where(halves[0], q_t, 0), jnp.where(halves[1], q_t, 0)], axis=0)
        s = lax.dot_general(k_t, q2, _NT, preferred_element_type=F32)
        return s + bb_ref[var_b[u], t]

    def attend_b(u, t, s):
        m = jnp.max(s, axis=0, keepdims=True)
        e = jnp.exp2(s - m).astype(BF16)
        vt_t = jnp.concatenate([vt_ref[0, (1 + t) * LANES:(2 + t) * LANES, pl.ds(ws_b[u], WIN_B)],
                                ones_b], axis=0)
        o = jnp.dot(vt_t, e, preferred_element_type=F32)
        rden = 1.0 / o[LANES:LANES + 1]
        return [o[:HEAD_DIM, :SUB] * rden[:, :SUB], o[HEAD_DIM:LANES, SUB:] * rden[:, SUB:]]

    groups = []
    for u in range(NSUB):
        groups += [(scores_a, attend_a, u, k) for k in range(KV_A)]
        groups += [(scores_b, attend_b, u, t) for t in range(H_B // 2)]
    per_sub = len(groups) // NSUB
    outs = []
    gain_out = _mod_row(mod_ref, pl.program_id(0), 5) * gpost_ref[...]
    gain_a = _as_columns(gga_ref[...], FIN_SUBS * SUB)
    gain_b = _as_columns(ggb_ref[...], FIN_SUBS * SUB)

    def finish(first_sub):
        oa_t, ob_t = [], []
        for u in range(first_sub, first_sub + FIN_SUBS):
            sub = outs[u * per_sub:(u + 1) * per_sub]
            oa_t.append(jnp.concatenate([sub[k][g] for k in range(KV_A) for g in range(G_A)], axis=0))
            ob_t.append(jnp.concatenate([part for pair in sub[KV_A:] for part in pair], axis=0))
        oa_t = jnp.concatenate(oa_t, axis=1)
        ob_t = jnp.concatenate(ob_t, axis=1)
        y_t = jnp.concatenate([_rms_rows(oa_t, gain_a), _rms_rows(ob_t, gain_b)], axis=0)
        z = lax.dot_general(y_t.astype(BF16), wo_ref[...], _TN, preferred_element_type=F32) + bo_ref[...]
        r = slice(first_sub * SUB, (first_sub + FIN_SUBS) * SUB)
        o_ref[0, r, :] = x_ref[0, r, :] + _rms(z, gain_out)

    pending = [grp[0](*grp[2:]) for grp in groups[:SCORE_AHEAD]]
    for n, (_, attend, u, idx) in enumerate(groups):
        if n + SCORE_AHEAD < len(groups):
            nxt = groups[n + SCORE_AHEAD]
            pending.append(nxt[0](*nxt[2:]))
        outs.append(attend(u, idx, pending.pop(0)))
        done_groups = n + 1 - FIN_LAG
        if 0 < done_groups < len(groups) and done_groups % (FIN_SUBS * per_sub) == 0:
            finish(done_groups // per_sub - FIN_SUBS)
    finish(NSUB - FIN_SUBS)


def _mixer(x, mod, q, k, vt, rel_bias, sinks, gga, ggb, wo, bo, g_post):
    bsz, seq, d = x.shape
    assert seq % TQ == 0 and TQ % SUB == 0
    tile = lambda b, i: (b, i, 0)
    whole = lambda b, i: (b, 0, 0)
    return pl.pallas_call(
        _mixer_kernel,
        grid=(bsz, seq // TQ),
        in_specs=[pl.BlockSpec(memory_space=pltpu.SMEM),
                  _const_spec(rel_bias.shape),
                  pl.BlockSpec((1, TQ, q.shape[2]), tile),
                  pl.BlockSpec((1,) + k.shape[1:], whole),
                  pl.BlockSpec((1,) + vt.shape[1:], whole),
                  _const_spec((1, QA)), _const_spec((1, QB)),
                  _const_spec(wo.shape), _const_spec((1, d)), _const_spec((1, d)),
                  _const_spec(mod.shape),
                  pl.BlockSpec((1, TQ, d), tile)],
        out_specs=pl.BlockSpec((1, TQ, d), tile),
        out_shape=jax.ShapeDtypeStruct((bsz, seq, d), F32),
        scratch_shapes=[pltpu.VMEM((NV_A, KV_A, WIN_A, G_A * SUB), F32),
                        pltpu.VMEM((NV_B, H_B // 2, WIN_B, 2 * SUB), F32)],
        compiler_params=pltpu.CompilerParams(dimension_semantics=("arbitrary", "arbitrary"),
                                             vmem_limit_bytes=VMEM_LIMIT),
        name="mixer",
    )(sinks, rel_bias, q, k, vt, gga.reshape(1, QA), ggb.reshape(1, QB), wo,
      bo.reshape(1, d), g_post.reshape(1, d), mod, x)


def kernel(x, c, w_ada, b_ada, g_pre_ffn1, w_gate1, w_up1, w_down1, g_post_ffn1, g_pre_mix, w_in, b_in, sinks_a, rel_bias_b, g_grp_a, g_grp_b, w_out, b_out, g_post_mix, g_pre_ffn2, w_gate2, w_up2, w_down2, g_post_ffn2):
    bsz, seq, d = x.shape
    depth = w_ada.shape[0]
    for l in range(depth):
        mod = _modulation(c, w_ada[l], b_ada[l])

        x = _ffn(x, mod, g_pre_ffn1[l], g_post_ffn1[l], w_gate1[l], w_up1[l], w_down1[l], mod0=0)

        q, k, vt = _projection(x, mod, g_pre_mix[l], w_in[l], b_in[l].reshape(1, -1))

        x = _mixer(x, mod, q, k, vt, rel_bias_b[l], sinks_a[l], g_grp_a[l], g_grp_b[l],
                   w_out[l].astype(BF16), b_out[l], g_post_mix[l])

        x = _ffn(x, mod, g_pre_ffn2[l], g_post_ffn2[l], w_gate2[l], w_up2[l], w_down2[l], mod0=6)
    return x
```

```python
import functools
import math

import jax
import jax.numpy as jnp
from jax import lax
from jax.experimental import pallas as pl
from jax.experimental.pallas import tpu as pltpu

F32 = jnp.float32
BF16 = jnp.bfloat16

CHUNK = 64
HEAD_DIM = 64
EPS = 1e-6
NEG_INF = -1e30
H_A, KV_A = 8, 2
G_A = H_A // KV_A
BACK_A = 2
H_B = 8
BACK_B = 8
REL_CLIP = 128
N_REL = 2 * REL_CLIP + 1
QA = H_A * HEAD_DIM
KVA = KV_A * HEAD_DIM
QB = H_B * HEAD_DIM
N_MOD = 9
LOG2E = math.log2(math.e)

LANES = 128
TQ = 512
SUB = LANES
NSUB = TQ // SUB
SCORE_AHEAD = 3
FIN_SUBS = 2
FIN_LAG = 3
PAD = BACK_B * CHUNK
PAD_A = BACK_A * CHUNK
WIN_A = SUB + PAD_A
WIN_B = SUB + PAD
NV_A = PAD_A // SUB + 1
NV_B = PAD // SUB + 1
EXT_B = PAD + SUB + CHUNK
DEN_ROWS = 16
ROLL_W = 1024
TM_FFN = 1024
FFN_SLAB = 512
W_ROWS = 256
W_SLOTS = 3
PROJ_PIECE = 256
PROJ_COLS = 256
PROJ_START = 2
PROJ_W_ROWS = 128
TN_MOD = 1536
FF_CHUNK = 256
VMEM_LIMIT = 60 * 1024 * 1024

_NT = (((1,), (1,)), ((), ()))
_TN = (((0,), (0,)), ((), ()))


def _rms(x, g):
    ms = jnp.mean(x * x, axis=-1, keepdims=True)
    return x * lax.rsqrt(ms + EPS) * g


def _mod_row(mod_ref, b, k):
    d = mod_ref.shape[1] // N_MOD
    return mod_ref[pl.ds(b, 1), pl.ds(k * d, d)]


def _const_spec(shape):
    nd = len(shape)
    return pl.BlockSpec(shape, lambda *_: (0,) * nd, pipeline_mode=pl.Buffered(1))


def _mod_kernel(c_ref, w_ref, b_ref, o_ref):
    c = c_ref[...]
    a = (c * jax.nn.sigmoid(c)).astype(BF16)
    o_ref[...] = jnp.dot(a, w_ref[...].astype(BF16), preferred_element_type=F32) + b_ref[...]


def _modulation(c, w_ada, b_ada):
    bsz, d = c.shape
    n = w_ada.shape[1]
    assert n % TN_MOD == 0
    return pl.pallas_call(
        _mod_kernel,
        grid=(n // TN_MOD,),
        in_specs=[pl.BlockSpec((bsz, d), lambda j: (0, 0)),
                  pl.BlockSpec((d, TN_MOD), lambda j: (0, j)),
                  pl.BlockSpec((1, TN_MOD), lambda j: (0, j))],
        out_specs=pl.BlockSpec((bsz, TN_MOD), lambda j: (0, j)),
        out_shape=jax.ShapeDtypeStruct((bsz, n), F32),
        compiler_params=pltpu.CompilerParams(dimension_semantics=("arbitrary",),
                                             vmem_limit_bytes=VMEM_LIMIT),
        name="adaln_mod",
    )(c, w_ada, b_ada.reshape(1, n))


def _load_bf16(src_hbm, dst_ref, stage, sem):
    ncol = src_hbm.shape[1]
    nslot, nrow = stage.shape[0], stage.shape[1]
    nblk = src_hbm.shape[0] // nrow

    def copy(n):
        return pltpu.make_async_copy(src_hbm.at[pl.ds(n * nrow, nrow), :],
                                     stage.at[n % nslot, :, pl.ds(0, ncol)], sem.at[n % nslot])

    for n in range(min(nslot - 1, nblk)):
        copy(n).start()
    for n in range(nblk):
        if n + nslot - 1 < nblk:
            copy(n + nslot - 1).start()
        copy(n).wait()
        dst_ref[n * nrow:(n + 1) * nrow, :] = stage[n % nslot, :, 0:ncol].astype(dst_ref.dtype)


def _ffn_kernel(x_ref, mod_ref, gpre_ref, gpost_ref, wg_hbm, wu_hbm, wd_hbm, o_ref,
                wg_ref, wu_ref, wd_ref, stage, sem, *, mod0):
    @pl.when((pl.program_id(0) == 0) & (pl.program_id(1) == 0))
    def _():
        _load_bf16(wg_hbm, wg_ref, stage, sem)
        _load_bf16(wu_hbm, wu_ref, stage, sem)
        _load_bf16(wd_hbm, wd_ref, stage, sem)

    b = pl.program_id(0)
    shift = _mod_row(mod_ref, b, mod0)
    gain_in = gpre_ref[...] * (1.0 + _mod_row(mod_ref, b, mod0 + 1))
    gain_out = (0.5 * _mod_row(mod_ref, b, mod0 + 2)) * gpost_ref[...]
    d_ff = wg_ref.shape[1]
    nslab = x_ref.shape[1] // FFN_SLAB
    nchunk = d_ff // FF_CHUNK

    def rows(s):
        return slice(s * FFN_SLAB, (s + 1) * FFN_SLAB)

    def pre(s):
        return (_rms(x_ref[0, rows(s), :], gain_in) + shift).astype(BF16)

    def post(s, acc):
        o_ref[0, rows(s), :] = x_ref[0, rows(s), :] + _rms(acc, gain_out)

    h = pre(0)
    done = None
    for s in range(nslab):
        acts = []
        h_next = None
        for c in range(nchunk):
            if c == nchunk // 4 and done is not None:
                post(s - 1, done)
            if c == nchunk // 2 and s + 1 < nslab:
                h_next = pre(s + 1)
            f0 = c * FF_CHUNK
            g = jnp.dot(h, wg_ref[:, f0:f0 + FF_CHUNK], preferred_element_type=F32)
            u = jnp.dot(h, wu_ref[:, f0:f0 + FF_CHUNK], preferred_element_type=F32)
            acts.append((g * jax.nn.sigmoid(g) * u).astype(BF16))
        done = jnp.dot(jnp.concatenate(acts, axis=1), wd_ref[...], preferred_element_type=F32)
        h = h_next
    post(nslab - 1, done)


def _ffn(x, mod, g_pre, g_post, wg, wu, wd, *, mod0):
    bsz, seq, d = x.shape
    d_ff = wg.shape[1]
    assert seq % TM_FFN == 0 and TM_FFN % FFN_SLAB == 0 and d_ff % FF_CHUNK == 0
    assert d % W_ROWS == 0 and d_ff % W_ROWS == 0
    blk = lambda b, j: (b, j, 0)
    hbm = pl.BlockSpec(memory_space=pl.ANY)
    return pl.pallas_call(
        functools.partial(_ffn_kernel, mod0=mod0),
        grid=(bsz, seq // TM_FFN),
        in_specs=[pl.BlockSpec((1, TM_FFN, d), blk),
                  _const_spec(mod.shape),
                  _const_spec((1, d)), _const_spec((1, d)),
                  hbm, hbm, hbm],
        out_specs=pl.BlockSpec((1, TM_FFN, d), blk),
        out_shape=jax.ShapeDtypeStruct((bsz, seq, d), F32),
        scratch_shapes=[pltpu.VMEM((d, d_ff), BF16), pltpu.VMEM((d, d_ff), BF16),
                        pltpu.VMEM((d_ff, d), BF16),
                        pltpu.VMEM((W_SLOTS, W_ROWS, max(d, d_ff)), F32),
                        pltpu.SemaphoreType.DMA((W_SLOTS,))],
        compiler_params=pltpu.CompilerParams(dimension_semantics=("arbitrary", "arbitrary"),
                                             vmem_limit_bytes=VMEM_LIMIT),
        name="ffn",
    )(x, mod, g_pre.reshape(1, d), g_post.reshape(1, d), wg, wu, wd)


def _project_norm(src_ref, r0, nrows, bsrc, mod_ref, gpre_ref):
    shift = _mod_row(mod_ref, bsrc, 3)
    gain_in = gpre_ref[...] * (1.0 + _mod_row(mod_ref, bsrc, 4))
    return (_rms(src_ref[0, r0:r0 + nrows, :], gain_in) + shift).astype(BF16)


def _project_cols(h, w_ref, b_ref, c):
    cols = slice(c * PROJ_COLS, (c + 1) * PROJ_COLS)
    return jnp.dot(h, w_ref[:, cols], preferred_element_type=F32) + b_ref[:, cols]


def _project_store(blocks, r0, q_s, k_s, vt_s, qslot, kslot, krow):
    p = jnp.concatenate(blocks, axis=1)
    nrows = p.shape[0]
    low = lax.broadcasted_iota(jnp.int32, (1, LANES), 1) < HEAD_DIM
    o_ka, o_va, o_qb = QA, QA + KVA, QA + 2 * KVA
    o_kb, o_vb = o_qb + QB, o_qb + 2 * QB
    nat = [p[:, t * LANES:(t + 1) * LANES] for t in range(QA // LANES)]
    swp = [pltpu.roll(t, HEAD_DIM, 1) for t in nat]
    qa = []
    for g in range(G_A):
        t0, t1 = g // 2, G_A // 2 + g // 2
        qa.append(jnp.where(low, nat[t0], swp[t1]) if g % 2 == 0 else jnp.where(low, swp[t0], nat[t1]))
    q = (jnp.concatenate(qa + [p[:, o_qb:o_kb]], axis=1) * (HEAD_DIM ** -0.5 * LOG2E)).astype(BF16)
    k = jnp.concatenate([p[:, o_ka:o_va], p[:, o_kb:o_vb]], axis=1).astype(BF16)
    vt = jnp.concatenate([p[:, o_va:o_qb], p[:, o_vb:]], axis=1).T.astype(BF16)

    def store():
        start = krow + r0
        seq_rows = pl.ds(start if isinstance(start, int) else pl.multiple_of(start, nrows), nrows)
        q_s[qslot, r0:r0 + nrows, :] = q
        k_s[kslot, seq_rows, :] = k
        vt_s[kslot, :, seq_rows] = vt

    return store


def _bias_tiles(rel_ref, ba_ref, bb_ref):
    j = lax.broadcasted_iota(jnp.int32, (WIN_A, G_A * SUB), 0)
    r = lax.broadcasted_iota(jnp.int32, (WIN_A, G_A * SUB), 1)
    g = r // SUB
    i = r % SUB
    for k in range(KV_A):
        slope = jnp.zeros(r.shape, F32)
        for gg in range(G_A):
            slope = jnp.where(g == gg, 2.0 ** (-8.0 * (k * G_A + gg + 1) / H_A), slope)
        for v in range(NV_A):
            off = min(v * SUB, PAD_A)
            back = off // CHUNK + i // CHUNK - j // CHUNK
            bias = -slope * jnp.abs(off + i - j).astype(F32)
            ba_ref[v, k] = jnp.where((back >= 0) & (back <= BACK_A), bias * LOG2E, NEG_INF)

    table = rel_ref[...]
    dcol = lax.broadcasted_iota(jnp.int32, (H_B, ROLL_W), 1)
    d = jnp.where(dcol < SUB, dcol, dcol - ROLL_W)
    idx = jnp.clip(PAD + d, -REL_CLIP, REL_CLIP) + REL_CLIP
    row = jnp.zeros((H_B, ROLL_W), F32)
    for n in range(N_REL):
        row = jnp.where(idx == n, table[:, n:n + 1], row)
    jb = lax.broadcasted_iota(jnp.int32, (WIN_B, SUB), 0)
    ib = lax.broadcasted_iota(jnp.int32, (WIN_B, SUB), 1)
    for h in range(H_B):
        tile = jnp.broadcast_to(row[h:h + 1, :], (EXT_B, ROLL_W))
        tile = pltpu.roll(tile, 0, 1, stride=1, stride_axis=0)[:, :SUB]
        tile = jnp.concatenate([tile, jnp.zeros((PAD + WIN_B - EXT_B, SUB), F32)], axis=0)
        for v in range(NV_B):
            off = min(v * SUB, PAD)
            back = off // CHUNK + ib // CHUNK - jb // CHUNK
            window = tile[PAD - off:PAD - off + WIN_B]
            bb_ref[v, h // 2, :, pl.ds((h % 2) * SUB, SUB)] = jnp.where(
                (back >= 0) & (back <= BACK_B), window * LOG2E, NEG_INF)


def _as_columns(g_row, width):
    t = jnp.transpose(jnp.broadcast_to(g_row, (LANES, g_row.shape[1])))
    return jnp.concatenate([t] * (width // LANES), axis=1)


def _rms_rows(xt, g_col):
    ms = jnp.mean(xt * xt, axis=0, keepdims=True)
    return xt * lax.rsqrt(ms + EPS) * g_col


def _mixer_kernel(sink_ref, rel_ref, gga_ref, ggb_ref, wo_ref, bo_ref, gpost_ref, mod_ref, gpre_ref,
                  w_hbm, bin_ref, x_ref, xn_ref, o_ref, ba_ref, bb_ref, q_s, k_s, vt_s, w_ref, stage, sem):
    b = pl.program_id(0)
    i = pl.program_id(1)
    nb, nt = pl.num_programs(0), pl.num_programs(1)

    last_tile = (i == nt - 1).astype(jnp.int32)
    q_cur = lax.rem(b * nt + i, 2)
    q_nxt = 1 - q_cur
    k_cur = lax.rem(b, 2)
    k_nxt = lax.rem(b + last_tile, 2)
    b_nxt = jnp.minimum(b + last_tile, nb - 1)
    row_nxt = pl.multiple_of((1 - last_tile) * (i + 1) * TQ, TQ)

    ncb = w_ref.shape[1] // PROJ_COLS

    @pl.when((b == 0) & (i == 0))
    def _():
        _bias_tiles(rel_ref, ba_ref, bb_ref)
        _load_bf16(w_hbm, w_ref, stage, sem)
        for r0 in range(0, TQ, PROJ_PIECE):
            h0 = _project_norm(x_ref, r0, PROJ_PIECE, 0, mod_ref, gpre_ref)
            _project_store([_project_cols(h0, w_ref, bin_ref, c) for c in range(ncb)],
                           r0, q_s, k_s, vt_s, 0, 0, 0)()
        for slot in range(2):
            k_s[slot, TQ:max(WIN_A, WIN_B), :] = jnp.zeros((max(WIN_A, WIN_B) - TQ, k_s.shape[2]), BF16)
            vt_s[slot, :, TQ:max(WIN_A, WIN_B)] = jnp.zeros((vt_s.shape[1], max(WIN_A, WIN_B) - TQ), BF16)

    ws_b = [pl.multiple_of(jnp.maximum(i * TQ + u * SUB - PAD, 0), SUB) for u in range(NSUB)]
    ws_a = [pl.multiple_of(jnp.maximum(i * TQ + u * SUB - PAD_A, 0), SUB) for u in range(NSUB)]
    var_b = [jnp.minimum(i * NSUB + u, NV_B - 1) for u in range(NSUB)]
    var_a = [jnp.minimum(i * NSUB + u, NV_A - 1) for u in range(NSUB)]

    low = lax.broadcasted_iota(jnp.int32, (1, LANES), 1) < HEAD_DIM
    halves = (low, jnp.logical_not(low))
    col_g = lax.broadcasted_iota(jnp.int32, (1, G_A * SUB), 1) // SUB
    ones_a = jnp.ones((DEN_ROWS, WIN_A), BF16)
    ones_b = jnp.ones((DEN_ROWS, WIN_B), BF16)

    def scores_a(u, k):
        k_a = k_s[k_cur,pl.ds(ws_a[u], WIN_A), 0:LANES]
        qs = jnp.concatenate(
            [jnp.where(halves[k], q_s[q_cur,u * SUB:(u + 1) * SUB, g * LANES:(g + 1) * LANES], 0)
             for g in range(G_A)], axis=0)
        s = lax.dot_general(k_a, qs, _NT, preferred_element_type=F32)
        return s + ba_ref[var_a[u], k]

    def attend_a(u, k, s):
        sink = jnp.zeros((1, G_A * SUB), F32)
        for g in range(G_A):
            sink = jnp.where(col_g == g, sink_ref[k * G_A + g] * LOG2E, sink)
        m = jnp.maximum(jnp.max(s, axis=0, keepdims=True), sink)
        e = jnp.exp2(s - m).astype(BF16)
        vt_a = jnp.concatenate([vt_s[k_cur,0:LANES, pl.ds(ws_a[u], WIN_A)], ones_a], axis=0)
        o = jnp.dot(vt_a, e, preferred_element_type=F32)
        den = o[LANES:LANES + 1] + jnp.exp2(sink - m)
        o = o[k * HEAD_DIM:(k + 1) * HEAD_DIM] * (1.0 / den)
        return [o[:, g * SUB:(g + 1) * SUB] for g in range(G_A)]

    def scores_b(u, t):
        q_t = q_s[q_cur,u * SUB:(u + 1) * SUB, QA + t * LANES:QA + (t + 1) * LANES]
        k_t = k_s[k_cur,pl.ds(ws_b[u], WIN_B), (1 + t) * LANES:(2 + t) * LANES]
        q2 = jnp.concatenate([jnp.where(halves[0], q_t, 0), jnp.where(halves[1], q_t, 0)], axis=0)
        s = lax.dot_general(k_t, q2, _NT, preferred_element_type=F32)
        return s + bb_ref[var_b[u], t]

    def attend_b(u, t, s):
        m = jnp.max(s, axis=0, keepdims=True)
        e = jnp.exp2(s - m).astype(BF16)
        vt_t = jnp.concatenate([vt_s[k_cur,(1 + t) * LANES:(2 + t) * LANES, pl.ds(ws_b[u], WIN_B)],
                                ones_b], axis=0)
        o = jnp.dot(vt_t, e, preferred_element_type=F32)
        rden = 1.0 / o[LANES:LANES + 1]
        return [o[:HEAD_DIM, :SUB] * rden[:, :SUB], o[HEAD_DIM:LANES, SUB:] * rden[:, SUB:]]

    groups = []
    for u in range(NSUB):
        groups += [(scores_a, attend_a, u, k) for k in range(KV_A)]
        groups += [(scores_b, attend_b, u, t) for t in range(H_B // 2)]
    per_sub = len(groups) // NSUB
    outs = []
    gain_out = _mod_row(mod_ref, pl.program_id(0), 5) * gpost_ref[...]
    gain_a = _as_columns(gga_ref[...], FIN_SUBS * SUB)
    gain_b = _as_columns(ggb_ref[...], FIN_SUBS * SUB)

    def finish(first_sub):
        oa_t, ob_t = [], []
        for u in range(first_sub, first_sub + FIN_SUBS):
            sub = outs[u * per_sub:(u + 1) * per_sub]
            oa_t.append(jnp.concatenate([sub[k][g] for k in range(KV_A) for g in range(G_A)], axis=0))
            ob_t.append(jnp.concatenate([part for pair in sub[KV_A:] for part in pair], axis=0))
        oa_t = jnp.concatenate(oa_t, axis=1)
        ob_t = jnp.concatenate(ob_t, axis=1)
        y_t = jnp.concatenate([_rms_rows(oa_t, gain_a), _rms_rows(ob_t, gain_b)], axis=0)
        z = lax.dot_general(y_t.astype(BF16), wo_ref[...], _TN, preferred_element_type=F32) + bo_ref[...]
        r = slice(first_sub * SUB, (first_sub + FIN_SUBS) * SUB)
        o_ref[0, r, :] = x_ref[0, r, :] + _rms(z, gain_out)

    assert PROJ_START + ncb * (TQ // PROJ_PIECE) <= len(groups)
    stores, blocks, h_nxt = [], [], None
    pending = [grp[0](*grp[2:]) for grp in groups[:SCORE_AHEAD]]
    for n, (_, attend, u, idx) in enumerate(groups):
        if n + SCORE_AHEAD < len(groups):
            nxt = groups[n + SCORE_AHEAD]
            pending.append(nxt[0](*nxt[2:]))
        outs.append(attend(u, idx, pending.pop(0)))
        if PROJ_START <= n < PROJ_START + ncb * (TQ // PROJ_PIECE):
            piece, c = divmod(n - PROJ_START, ncb)
            if c == 0:
                h_nxt = _project_norm(xn_ref, piece * PROJ_PIECE, PROJ_PIECE, b_nxt, mod_ref, gpre_ref)
                blocks = []
            blocks.append(_project_cols(h_nxt, w_ref, bin_ref, c))
            if c == ncb - 1:
                stores.append(_project_store(blocks, piece * PROJ_PIECE, q_s, k_s, vt_s, q_nxt, k_nxt, row_nxt))
        done_groups = n + 1 - FIN_LAG
        if 0 < done_groups < len(groups) and done_groups % (FIN_SUBS * per_sub) == 0:
            finish(done_groups // per_sub - FIN_SUBS)
    finish(NSUB - FIN_SUBS)
    for store in stores:
        store()


def _mixer(x, mod, g_pre, w_in, b_in, rel_bias, sinks, gga, ggb, wo, bo, g_post):
    bsz, seq, d = x.shape
    nt = seq // TQ
    nq, nkv = QA + QB, KVA + QB
    assert seq % TQ == 0 and TQ % SUB == 0 and TQ % PROJ_PIECE == 0 and d % PROJ_W_ROWS == 0
    assert w_in.shape == (d, nq + 2 * nkv)
    tile = lambda b, i: (b, i, 0)
    tile_nxt = lambda b, i: (jnp.minimum(b + (i + 1) // nt, bsz - 1), (i + 1) % nt, 0)
    return pl.pallas_call(
        _mixer_kernel,
        grid=(bsz, nt),
        in_specs=[pl.BlockSpec(memory_space=pltpu.SMEM),
                  _const_spec(rel_bias.shape),
                  _const_spec((1, QA)), _const_spec((1, QB)),
                  _const_spec(wo.shape), _const_spec((1, d)), _const_spec((1, d)),
                  _const_spec(mod.shape), _const_spec((1, d)),
                  pl.BlockSpec(memory_space=pl.ANY), _const_spec(b_in.shape),
                  pl.BlockSpec((1, TQ, d), tile),
                  pl.BlockSpec((1, TQ, d), tile_nxt)],
        out_specs=pl.BlockSpec((1, TQ, d), tile),
        out_shape=jax.ShapeDtypeStruct((bsz, seq, d), F32),
        scratch_shapes=[pltpu.VMEM((NV_A, KV_A, WIN_A, G_A * SUB), F32),
                        pltpu.VMEM((NV_B, H_B // 2, WIN_B, 2 * SUB), F32),
                        pltpu.VMEM((2, TQ, nq), BF16),
                        pltpu.VMEM((2, seq, nkv), BF16),
                        pltpu.VMEM((2, nkv, seq), BF16),
                        pltpu.VMEM(w_in.shape, BF16),
                        pltpu.VMEM((W_SLOTS, PROJ_W_ROWS, w_in.shape[1]), F32),
                        pltpu.SemaphoreType.DMA((W_SLOTS,))],
        compiler_params=pltpu.CompilerParams(dimension_semantics=("arbitrary", "arbitrary"),
                                             vmem_limit_bytes=VMEM_LIMIT),
        name="mixer",
    )(sinks, rel_bias, gga.reshape(1, QA), ggb.reshape(1, QB), wo, bo.reshape(1, d),
      g_post.reshape(1, d), mod, g_pre.reshape(1, d), w_in, b_in, x, x)


def kernel(x, c, w_ada, b_ada, g_pre_ffn1, w_gate1, w_up1, w_down1, g_post_ffn1, g_pre_mix, w_in, b_in, sinks_a, rel_bias_b, g_grp_a, g_grp_b, w_out, b_out, g_post_mix, g_pre_ffn2, w_gate2, w_up2, w_down2, g_post_ffn2):
    bsz, seq, d = x.shape
    depth = w_ada.shape[0]
    for l in range(depth):
        mod = _modulation(c, w_ada[l], b_ada[l])

        x = _ffn(x, mod, g_pre_ffn1[l], g_post_ffn1[l], w_gate1[l], w_up1[l], w_down1[l], mod0=0)

        x = _mixer(x, mod, g_pre_mix[l], w_in[l], b_in[l].reshape(1, -1),
                   rel_bias_b[l], sinks_a[l], g_grp_a[l], g_grp_b[l],
                   w_out[l].astype(BF16), b_out[l], g_post_mix[l])

        x = _ffn(x, mod, g_pre_ffn2[l], g_post_ffn2[l], w_gate2[l], w_up2[l], w_down2[l], mod0=6)
    return x
```

```python
import functools
import math

import jax
import jax.numpy as jnp
from jax import lax
from jax.experimental import pallas as pl
from jax.experimental.pallas import tpu as pltpu

F32 = jnp.float32
BF16 = jnp.bfloat16

CHUNK = 64
HEAD_DIM = 64
EPS = 1e-6
NEG_INF = -1e30
H_A, KV_A = 8, 2
G_A = H_A // KV_A
BACK_A = 2
H_B = 8
BACK_B = 8
REL_CLIP = 128
N_REL = 2 * REL_CLIP + 1
QA = H_A * HEAD_DIM
KVA = KV_A * HEAD_DIM
QB = H_B * HEAD_DIM
N_MOD = 9
LOG2E = math.log2(math.e)

LANES = 128
TQ = 1024
SUB = LANES
NSUB = TQ // SUB
SCORE_AHEAD = 3
FIN_SUBS = 2
FIN_LAG = 3
PAD = BACK_B * CHUNK
PAD_A = BACK_A * CHUNK
WIN_A = SUB + PAD_A
WIN_B = SUB + PAD
NV_A = PAD_A // SUB + 1
NV_B = PAD // SUB + 1
EXT_B = PAD + SUB + CHUNK
DEN_ROWS = 16
ROLL_W = 1024
TM_FFN = 1024
FFN_SLAB = 512
W_ROWS = 256
W_SLOTS = 3
TM_PROJ = 1024
PROJ_SLAB = 512
MOD_ROWS = 128
FF_CHUNK = 256
VMEM_LIMIT = 56 * 1024 * 1024

_NT = (((1,), (1,)), ((), ()))
_TN = (((0,), (0,)), ((), ()))


def _rms(x, g):
    ms = jnp.mean(x * x, axis=-1, keepdims=True)
    return x * lax.rsqrt(ms + EPS) * g


def _mod_row(mod_ref, b, k):
    d = mod_ref.shape[1] // N_MOD
    return mod_ref[pl.ds(b, 1), pl.ds(k * d, d)]


def _const_spec(shape):
    nd = len(shape)
    return pl.BlockSpec(shape, lambda *_: (0,) * nd, pipeline_mode=pl.Buffered(1))


def _mod_kernel(c_ref, w_hbm, b_ref, o_ref, stage, sem):
    c = c_ref[...]
    a = (c * jax.nn.sigmoid(c)).astype(BF16)
    nslot, nrow = stage.shape[0], stage.shape[1]
    nblk = w_hbm.shape[0] // nrow

    def copy(n):
        return pltpu.make_async_copy(w_hbm.at[pl.ds(n * nrow, nrow), :], stage.at[n % nslot],
                                     sem.at[n % nslot])

    for n in range(min(nslot - 1, nblk)):
        copy(n).start()
    acc = jnp.zeros(o_ref.shape, F32)
    for n in range(nblk):
        if n + nslot - 1 < nblk:
            copy(n + nslot - 1).start()
        copy(n).wait()
        acc = acc + jnp.dot(a[:, n * nrow:(n + 1) * nrow], stage[n % nslot].astype(BF16),
                            preferred_element_type=F32)
    o_ref[...] = acc + b_ref[...]


def _modulation(c, w_ada, b_ada):
    bsz, d = c.shape
    n = w_ada.shape[1]
    assert d % MOD_ROWS == 0
    return pl.pallas_call(
        _mod_kernel,
        grid=(1,),
        in_specs=[pl.BlockSpec((bsz, d), lambda j: (0, 0)),
                  pl.BlockSpec(memory_space=pl.ANY),
                  pl.BlockSpec((1, n), lambda j: (0, 0))],
        out_specs=pl.BlockSpec((bsz, n), lambda j: (0, 0)),
        out_shape=jax.ShapeDtypeStruct((bsz, n), F32),
        scratch_shapes=[pltpu.VMEM((W_SLOTS, MOD_ROWS, n), F32),
                        pltpu.SemaphoreType.DMA((W_SLOTS,))],
        compiler_params=pltpu.CompilerParams(dimension_semantics=("arbitrary",),
                                             vmem_limit_bytes=VMEM_LIMIT),
        name="adaln_mod",
    )(c, w_ada, b_ada.reshape(1, n))


def _load_bf16(src_hbm, dst_ref, stage, sem):
    ncol = src_hbm.shape[1]
    nblk = src_hbm.shape[0] // W_ROWS
    nslot = stage.shape[0]

    def copy(n):
        return pltpu.make_async_copy(src_hbm.at[pl.ds(n * W_ROWS, W_ROWS), :],
                                     stage.at[n % nslot, :, pl.ds(0, ncol)], sem.at[n % nslot])

    for n in range(min(nslot - 1, nblk)):
        copy(n).start()
    for n in range(nblk):
        if n + nslot - 1 < nblk:
            copy(n + nslot - 1).start()
        copy(n).wait()
        dst_ref[n * W_ROWS:(n + 1) * W_ROWS, :] = stage[n % nslot, :, 0:ncol].astype(dst_ref.dtype)


def _ffn_kernel(x_ref, mod_ref, gpre_ref, gpost_ref, wg_hbm, wu_hbm, wd_hbm, o_ref,
                wg_ref, wu_ref, wd_ref, stage, sem, *, mod0):
    @pl.when((pl.program_id(0) == 0) & (pl.program_id(1) == 0))
    def _():
        _load_bf16(wg_hbm, wg_ref, stage, sem)
        _load_bf16(wu_hbm, wu_ref, stage, sem)
        _load_bf16(wd_hbm, wd_ref, stage, sem)

    b = pl.program_id(0)
    shift = _mod_row(mod_ref, b, mod0)
    gain_in = gpre_ref[...] * (1.0 + _mod_row(mod_ref, b, mod0 + 1))
    gain_out = (0.5 * _mod_row(mod_ref, b, mod0 + 2)) * gpost_ref[...]
    d_ff = wg_ref.shape[1]
    nslab = x_ref.shape[1] // FFN_SLAB
    nchunk = d_ff // FF_CHUNK

    def rows(s):
        return slice(s * FFN_SLAB, (s + 1) * FFN_SLAB)

    def pre(s):
        return (_rms(x_ref[0, rows(s), :], gain_in) + shift).astype(BF16)

    def post(s, acc):
        o_ref[0, rows(s), :] = x_ref[0, rows(s), :] + _rms(acc, gain_out)

    h = pre(0)
    done = None
    for s in range(nslab):
        acts = []
        h_next = None
        for c in range(nchunk):
            if c == nchunk // 4 and done is not None:
                post(s - 1, done)
            if c == nchunk // 2 and s + 1 < nslab:
                h_next = pre(s + 1)
            f0 = c * FF_CHUNK
            g = jnp.dot(h, wg_ref[:, f0:f0 + FF_CHUNK], preferred_element_type=F32)
            u = jnp.dot(h, wu_ref[:, f0:f0 + FF_CHUNK], preferred_element_type=F32)
            acts.append((g * jax.nn.sigmoid(g) * u).astype(BF16))
        done = jnp.dot(jnp.concatenate(acts, axis=1), wd_ref[...], preferred_element_type=F32)
        h = h_next
    post(nslab - 1, done)


def _ffn(x, mod, g_pre, g_post, wg, wu, wd, *, mod0):
    bsz, seq, d = x.shape
    d_ff = wg.shape[1]
    assert seq % TM_FFN == 0 and TM_FFN % FFN_SLAB == 0 and d_ff % FF_CHUNK == 0
    assert d % W_ROWS == 0 and d_ff % W_ROWS == 0
    blk = lambda b, j: (b, j, 0)
    hbm = pl.BlockSpec(memory_space=pl.ANY)
    return pl.pallas_call(
        functools.partial(_ffn_kernel, mod0=mod0),
        grid=(bsz, seq // TM_FFN),
        in_specs=[pl.BlockSpec((1, TM_FFN, d), blk),
                  _const_spec(mod.shape),
                  _const_spec((1, d)), _const_spec((1, d)),
                  hbm, hbm, hbm],
        out_specs=pl.BlockSpec((1, TM_FFN, d), blk),
        out_shape=jax.ShapeDtypeStruct((bsz, seq, d), F32),
        scratch_shapes=[pltpu.VMEM((d, d_ff), BF16), pltpu.VMEM((d, d_ff), BF16),
                        pltpu.VMEM((d_ff, d), BF16),
                        pltpu.VMEM((W_SLOTS, W_ROWS, max(d, d_ff)), F32),
                        pltpu.SemaphoreType.DMA((W_SLOTS,))],
        compiler_params=pltpu.CompilerParams(dimension_semantics=("arbitrary", "arbitrary"),
                                             vmem_limit_bytes=VMEM_LIMIT),
        name="ffn",
    )(x, mod, g_pre.reshape(1, d), g_post.reshape(1, d), wg, wu, wd)


def _proj_kernel(x_ref, mod_ref, gpre_ref, w_hbm, b_ref, q_ref, k_ref, vt_ref, w_ref, stage, sem):
    b = pl.program_id(0)

    @pl.when((b == 0) & (pl.program_id(1) == 0))
    def _():
        _load_bf16(w_hbm, w_ref, stage, sem)

    shift = _mod_row(mod_ref, b, 3)
    gain_in = gpre_ref[...] * (1.0 + _mod_row(mod_ref, b, 4))
    nslab = x_ref.shape[1] // PROJ_SLAB

    def rows(s):
        return slice(s * PROJ_SLAB, (s + 1) * PROJ_SLAB)

    def pre(s):
        return (_rms(x_ref[0, rows(s), :], gain_in) + shift).astype(BF16)

    low = lax.broadcasted_iota(jnp.int32, (1, LANES), 1) < HEAD_DIM
    o_ka, o_va, o_qb = QA, QA + KVA, QA + 2 * KVA
    o_kb, o_vb = o_qb + QB, o_qb + 2 * QB
    h = pre(0)
    for s in range(nslab):
        p = jnp.dot(h, w_ref[...], preferred_element_type=F32) + b_ref[...]
        h_next = pre(s + 1) if s + 1 < nslab else None
        nat = [p[:, t * LANES:(t + 1) * LANES] for t in range(QA // LANES)]
        swp = [pltpu.roll(t, HEAD_DIM, 1) for t in nat]
        qa = []
        for g in range(G_A):
            t0, t1 = g // 2, G_A // 2 + g // 2
            qa.append(jnp.where(low, nat[t0], swp[t1]) if g % 2 == 0 else jnp.where(low, swp[t0], nat[t1]))
        q = jnp.concatenate(qa + [p[:, o_qb:o_kb]], axis=1)
        q_ref[0, rows(s), :] = (q * (HEAD_DIM ** -0.5 * LOG2E)).astype(BF16)
        k_ref[0, rows(s), :] = jnp.concatenate([p[:, o_ka:o_va], p[:, o_kb:o_vb]], axis=1).astype(BF16)
        v = jnp.concatenate([p[:, o_va:o_qb], p[:, o_vb:]], axis=1)
        vt_ref[0, :, rows(s)] = v.T.astype(BF16)
        h = h_next


def _projection(x, mod, g_pre, w_all, b_all):
    bsz, seq, d = x.shape
    nq, nk, nv = QA + QB, KVA + QB, KVA + QB
    assert w_all.shape == (d, nq + nk + nv)
    assert seq % TM_PROJ == 0 and TM_PROJ % PROJ_SLAB == 0 and d % W_ROWS == 0
    blk = lambda b, j: (b, j, 0)
    return pl.pallas_call(
        _proj_kernel,
        grid=(bsz, seq // TM_PROJ),
        in_specs=[pl.BlockSpec((1, TM_PROJ, d), blk),
                  _const_spec(mod.shape),
                  _const_spec((1, d)),
                  pl.BlockSpec(memory_space=pl.ANY), _const_spec(b_all.shape)],
        out_specs=[pl.BlockSpec((1, TM_PROJ, nq), blk),
                   pl.BlockSpec((1, TM_PROJ, nk), blk),
                   pl.BlockSpec((1, nv, TM_PROJ), lambda b, j: (b, 0, j))],
        out_shape=[jax.ShapeDtypeStruct((bsz, seq, nq), BF16),
                   jax.ShapeDtypeStruct((bsz, seq, nk), BF16),
                   jax.ShapeDtypeStruct((bsz, nv, seq), BF16)],
        scratch_shapes=[pltpu.VMEM(w_all.shape, BF16),
                        pltpu.VMEM((W_SLOTS, W_ROWS, w_all.shape[1]), F32),
                        pltpu.SemaphoreType.DMA((W_SLOTS,))],
        compiler_params=pltpu.CompilerParams(dimension_semantics=("arbitrary", "arbitrary"),
                                             vmem_limit_bytes=VMEM_LIMIT),
        name="qkv_proj",
    )(x, mod, g_pre.reshape(1, d), w_all, b_all)


def _bias_tiles(rel_ref, ba_ref, bb_ref):
    j = lax.broadcasted_iota(jnp.int32, (WIN_A, G_A * SUB), 0)
    r = lax.broadcasted_iota(jnp.int32, (WIN_A, G_A * SUB), 1)
    g = r // SUB
    i = r % SUB
    for k in range(KV_A):
        slope = jnp.zeros(r.shape, F32)
        for gg in range(G_A):
            slope = jnp.where(g == gg, 2.0 ** (-8.0 * (k * G_A + gg + 1) / H_A), slope)
        for v in range(NV_A):
            off = min(v * SUB, PAD_A)
            back = off // CHUNK + i // CHUNK - j // CHUNK
            bias = -slope * jnp.abs(off + i - j).astype(F32)
            ba_ref[v, k] = jnp.where((back >= 0) & (back <= BACK_A), bias * LOG2E, NEG_INF)

    table = rel_ref[...]
    dcol = lax.broadcasted_iota(jnp.int32, (H_B, ROLL_W), 1)
    d = jnp.where(dcol < SUB, dcol, dcol - ROLL_W)
    idx = jnp.clip(PAD + d, -REL_CLIP, REL_CLIP) + REL_CLIP
    row = jnp.zeros((H_B, ROLL_W), F32)
    for n in range(N_REL):
        row = jnp.where(idx == n, table[:, n:n + 1], row)
    jb = lax.broadcasted_iota(jnp.int32, (WIN_B, SUB), 0)
    ib = lax.broadcasted_iota(jnp.int32, (WIN_B, SUB), 1)
    for h in range(H_B):
        tile = jnp.broadcast_to(row[h:h + 1, :], (EXT_B, ROLL_W))
        tile = pltpu.roll(tile, 0, 1, stride=1, stride_axis=0)[:, :SUB]
        tile = jnp.concatenate([tile, jnp.zeros((PAD + WIN_B - EXT_B, SUB), F32)], axis=0)
        for v in range(NV_B):
            off = min(v * SUB, PAD)
            back = off // CHUNK + ib // CHUNK - jb // CHUNK
            window = tile[PAD - off:PAD - off + WIN_B]
            bb_ref[v, h // 2, :, pl.ds((h % 2) * SUB, SUB)] = jnp.where(
                (back >= 0) & (back <= BACK_B), window * LOG2E, NEG_INF)


def _as_columns(g_row, width):
    t = jnp.transpose(jnp.broadcast_to(g_row, (LANES, g_row.shape[1])))
    return jnp.concatenate([t] * (width // LANES), axis=1)


def _rms_rows(xt, g_col):
    ms = jnp.mean(xt * xt, axis=0, keepdims=True)
    return xt * lax.rsqrt(ms + EPS) * g_col


def _mixer_kernel(sink_ref, rel_ref, q_ref, k_ref, vt_ref, gga_ref, ggb_ref, wo_ref, bo_ref,
                  gpost_ref, mod_ref, x_ref, o_ref, ba_ref, bb_ref):
    i = pl.program_id(1)

    @pl.when((pl.program_id(0) == 0) & (i == 0))
    def _():
        _bias_tiles(rel_ref, ba_ref, bb_ref)

    ws_b = [pl.multiple_of(jnp.maximum(i * TQ + u * SUB - PAD, 0), SUB) for u in range(NSUB)]
    ws_a = [pl.multiple_of(jnp.maximum(i * TQ + u * SUB - PAD_A, 0), SUB) for u in range(NSUB)]
    var_b = [jnp.minimum(i * NSUB + u, NV_B - 1) for u in range(NSUB)]
    var_a = [jnp.minimum(i * NSUB + u, NV_A - 1) for u in range(NSUB)]

    low = lax.broadcasted_iota(jnp.int32, (1, LANES), 1) < HEAD_DIM
    halves = (low, jnp.logical_not(low))
    col_g = lax.broadcasted_iota(jnp.int32, (1, G_A * SUB), 1) // SUB
    ones_a = jnp.ones((DEN_ROWS, WIN_A), BF16)
    ones_b = jnp.ones((DEN_ROWS, WIN_B), BF16)

    def scores_a(u, k):
        k_a = k_ref[0, pl.ds(ws_a[u], WIN_A), 0:LANES]
        qs = jnp.concatenate(
            [jnp.where(halves[k], q_ref[0, u * SUB:(u + 1) * SUB, g * LANES:(g + 1) * LANES], 0)
             for g in range(G_A)], axis=0)
        s = lax.dot_general(k_a, qs, _NT, preferred_element_type=F32)
        return s + ba_ref[var_a[u], k]

    def attend_a(u, k, s):
        sink = jnp.zeros((1, G_A * SUB), F32)
        for g in range(G_A):
            sink = jnp.where(col_g == g, sink_ref[k * G_A + g] * LOG2E, sink)
        m = jnp.maximum(jnp.max(s, axis=0, keepdims=True), sink)
        e = jnp.exp2(s - m).astype(BF16)
        vt_a = jnp.concatenate([vt_ref[0, 0:LANES, pl.ds(ws_a[u], WIN_A)], ones_a], axis=0)
        o = jnp.dot(vt_a, e, preferred_element_type=F32)
        den = o[LANES:LANES + 1] + jnp.exp2(sink - m)
        o = o[k * HEAD_DIM:(k + 1) * HEAD_DIM] * (1.0 / den)
        return [o[:, g * SUB:(g + 1) * SUB] for g in range(G_A)]

    def scores_b(u, t):
        q_t = q_ref[0, u * SUB:(u + 1) * SUB, QA + t * LANES:QA + (t + 1) * LANES]
        k_t = k_ref[0, pl.ds(ws_b[u], WIN_B), (1 + t) * LANES:(2 + t) * LANES]
        q2 = jnp.concatenate([jnp.where(halves[0], q_t, 0), jnp.where(halves[1], q_t, 0)], axis=0)
        s = lax.dot_general(k_t, q2, _NT, preferred_element_type=F32)
        return s + bb_ref[var_b[u], t]

    def attend_b(u, t, s):
        m = jnp.max(s, axis=0, keepdims=True)
        e = jnp.exp2(s - m).astype(BF16)
        vt_t = jnp.concatenate([vt_ref[0, (1 + t) * LANES:(2 + t) * LANES, pl.ds(ws_b[u], WIN_B)],
                                ones_b], axis=0)
        o = jnp.dot(vt_t, e, preferred_element_type=F32)
        rden = 1.0 / o[LANES:LANES + 1]
        return [o[:HEAD_DIM, :SUB] * rden[:, :SUB], o[HEAD_DIM:LANES, SUB:] * rden[:, SUB:]]

    groups = []
    for u in range(NSUB):
        groups += [(scores_a, attend_a, u, k) for k in range(KV_A)]
        groups += [(scores_b, attend_b, u, t) for t in range(H_B // 2)]
    per_sub = len(groups) // NSUB
    outs = []
    gain_out = _mod_row(mod_ref, pl.program_id(0), 5) * gpost_ref[...]
    gain_a = _as_columns(gga_ref[...], FIN_SUBS * SUB)
    gain_b = _as_columns(ggb_ref[...], FIN_SUBS * SUB)

    def finish(first_sub):
        oa_t, ob_t = [], []
        for u in range(first_sub, first_sub + FIN_SUBS):
            sub = outs[u * per_sub:(u + 1) * per_sub]
            oa_t.append(jnp.concatenate([sub[k][g] for k in range(KV_A) for g in range(G_A)], axis=0))
            ob_t.append(jnp.concatenate([part for pair in sub[KV_A:] for part in pair], axis=0))
        oa_t = jnp.concatenate(oa_t, axis=1)
        ob_t = jnp.concatenate(ob_t, axis=1)
        y_t = jnp.concatenate([_rms_rows(oa_t, gain_a), _rms_rows(ob_t, gain_b)], axis=0)
        z = lax.dot_general(y_t.astype(BF16), wo_ref[...], _TN, preferred_element_type=F32) + bo_ref[...]
        r = slice(first_sub * SUB, (first_sub + FIN_SUBS) * SUB)
        o_ref[0, r, :] = x_ref[0, r, :] + _rms(z, gain_out)

    pending = [grp[0](*grp[2:]) for grp in groups[:SCORE_AHEAD]]
    for n, (_, attend, u, idx) in enumerate(groups):
        if n + SCORE_AHEAD < len(groups):
            nxt = groups[n + SCORE_AHEAD]
            pending.append(nxt[0](*nxt[2:]))
        outs.append(attend(u, idx, pending.pop(0)))
        done_groups = n + 1 - FIN_LAG
        if 0 < done_groups < len(groups) and done_groups % (FIN_SUBS * per_sub) == 0:
            finish(done_groups // per_sub - FIN_SUBS)
    finish(NSUB - FIN_SUBS)


def _mixer(x, mod, q, k, vt, rel_bias, sinks, gga, ggb, wo, bo, g_post):
    bsz, seq, d = x.shape
    assert seq % TQ == 0 and TQ % SUB == 0
    tile = lambda b, i: (b, i, 0)
    whole = lambda b, i: (b, 0, 0)
    return pl.pallas_call(
        _mixer_kernel,
        grid=(bsz, seq // TQ),
        in_specs=[pl.BlockSpec(memory_space=pltpu.SMEM),
                  _const_spec(rel_bias.shape),
                  pl.BlockSpec((1, TQ, q.shape[2]), tile),
                  pl.BlockSpec((1,) + k.shape[1:], whole),
                  pl.BlockSpec((1,) + vt.shape[1:], whole),
                  _const_spec((1, QA)), _const_spec((1, QB)),
                  _const_spec(wo.shape), _const_spec((1, d)), _const_spec((1, d)),
                  _const_spec(mod.shape),
                  pl.BlockSpec((1, TQ, d), tile)],
        out_specs=pl.BlockSpec((1, TQ, d), tile),
        out_shape=jax.ShapeDtypeStruct((bsz, seq, d), F32),
        scratch_shapes=[pltpu.VMEM((NV_A, KV_A, WIN_A, G_A * SUB), F32),
                        pltpu.VMEM((NV_B, H_B // 2, WIN_B, 2 * SUB), F32)],
        compiler_params=pltpu.CompilerParams(dimension_semantics=("arbitrary", "arbitrary"),
                                             vmem_limit_bytes=VMEM_LIMIT),
        name="mixer",
    )(sinks, rel_bias, q, k, vt, gga.reshape(1, QA), ggb.reshape(1, QB), wo,
      bo.reshape(1, d), g_post.reshape(1, d), mod, x)


def kernel(x, c, w_ada, b_ada, g_pre_ffn1, w_gate1, w_up1, w_down1, g_post_ffn1, g_pre_mix, w_in, b_in, sinks_a, rel_bias_b, g_grp_a, g_grp_b, w_out, b_out, g_post_mix, g_pre_ffn2, w_gate2, w_up2, w_down2, g_post_ffn2):
    bsz, seq, d = x.shape
    depth = w_ada.shape[0]
    for l in range(depth):
        mod = _modulation(c, w_ada[l], b_ada[l])

        x = _ffn(x, mod, g_pre_ffn1[l], g_post_ffn1[l], w_gate1[l], w_up1[l], w_down1[l], mod0=0)

        q, k, vt = _projection(x, mod, g_pre_mix[l], w_in[l], b_in[l].reshape(1, -1))

        x = _mixer(x, mod, q, k, vt, rel_bias_b[l], sinks_a[l], g_grp_a[l], g_grp_b[l],
                   w_out[l].astype(BF16), b_out[l], g_post_mix[l])

        x = _ffn(x, mod, g_pre_ffn2[l], g_post_ffn2[l], w_gate2[l], w_up2[l], w_down2[l], mod0=6)
    return x
```
